```python
import jax, jax.numpy as jnp
from jax import lax
import numpy as np

D_MODEL = 1024
BATCH = 8
SEQ = 8192
DEPTH = 1

PLE_DIM = 256
CONV_DIM = D_MODEL
CONV_K = 3
RET_HEADS = 8
RET_DK = D_MODEL // 16
RET_DV = 2 * RET_DK
QK_W = RET_HEADS * RET_DK
V_W = RET_HEADS * RET_DV
RET_CHUNK = 128
ROPE_BASE = 10000.0
N_GROUPS = 4
EXPERTS_PER_GROUP = 8
N_EXPERTS = N_GROUPS * EXPERTS_PER_GROUP
TOP_K_IN_GROUP = 2
D_EXPERT = D_MODEL // 2
MOE_BLOCK = 128
EPS = 1e-6
W_IN_WIDTHS = (CONV_DIM, CONV_DIM, CONV_DIM, QK_W, QK_W, V_W, V_W, D_MODEL, D_MODEL)
W_IN_TOTAL = sum(W_IN_WIDTHS)

kernel_name = 'hybrid_conv_retention_hmoe'


def rmsnorm(t, g):
    tf = t.astype(jnp.float32)
    tf = tf * lax.rsqrt(jnp.mean(tf * tf, axis=-1, keepdims=True) + EPS)
    return (tf * g.astype(jnp.float32)).astype(t.dtype)


def rope_tables(s_len):
    inv = ROPE_BASE ** (-jnp.arange(0, RET_DK, 2, dtype=jnp.float32) / RET_DK)
    ang = jnp.arange(s_len, dtype=jnp.float32)[:, None] * inv[None, :]
    return jnp.cos(ang), jnp.sin(ang)


def rope(t, cos, sin):
    t1, t2 = jnp.split(t, 2, axis=-1)
    c = cos[None, :, None, :]
    s = sin[None, :, None, :]
    return jnp.concatenate([t1 * c - t2 * s, t1 * s + t2 * c], axis=-1).astype(t.dtype)


def causal_dwconv(u, w):
    k_w = w.shape[0]
    s_len = u.shape[1]
    up = jnp.pad(u, ((0, 0), (k_w - 1, 0), (0, 0)))
    return sum(up[:, j:j + s_len] * w[j] for j in range(k_w))


def retention(q, k, v):
    b_, s_len, h_, dk = q.shape
    dv = v.shape[-1]
    n_chunks = s_len // RET_CHUNK
    log_gamma = jnp.log1p(-jnp.exp2(-5.0 - jnp.arange(h_, dtype=jnp.float32)))
    pos = jnp.arange(RET_CHUNK, dtype=jnp.float32)
    diff = pos[:, None] - pos[None, :]
    decay_mask = jnp.where((diff >= 0)[None],
                           jnp.exp(log_gamma[:, None, None] * jnp.maximum(diff, 0.0)[None]), 0.0)
    q_decay = jnp.exp(log_gamma[:, None] * (pos[None, :] + 1.0))
    k_decay = jnp.exp(log_gamma[:, None] * (RET_CHUNK - 1.0 - pos[None, :]))
    chunk_decay = jnp.exp(log_gamma * RET_CHUNK)

    def to_chunks(t):
        return t.astype(jnp.float32).reshape(b_, n_chunks, RET_CHUNK, h_, t.shape[-1]).transpose(1, 0, 3, 2, 4)

    qc, kc, vc = to_chunks(q), to_chunks(k), to_chunks(v)

    def step(state, inp):
        qb, kb, vb = inp
        scores = jnp.einsum('bhid,bhjd->bhij', qb, kb) * decay_mask
        inner = jnp.einsum('bhij,bhjv->bhiv', scores, vb)
        cross = jnp.einsum('bhid,bhdv->bhiv', qb, state) * q_decay[None, :, :, None]
        state = state * chunk_decay[None, :, None, None] + jnp.einsum(
            'bhjd,bhjv->bhdv', kb * k_decay[None, :, :, None], vb)
        return state, inner + cross

    s0 = jnp.zeros((b_, h_, dk, dv), jnp.float32)
    _, out = lax.scan(step, s0, (qc, kc, vc))
    return out.transpose(1, 0, 3, 2, 4).reshape(b_, s_len, h_, dv).astype(v.dtype)


def hier_moe(h, w_rg, b_rg, w_re, b_re, w_gate, w_up, w_down):
    b_, s_len, d = h.shape
    n_tok = b_ * s_len
    xt = h.reshape(n_tok, d)
    g_logits = (xt @ w_rg).astype(jnp.float32) + b_rg
    g_probs = jax.nn.softmax(g_logits, axis=-1)
    grp = jnp.argmax(g_logits, axis=-1).astype(jnp.int32)
    g_w = jnp.take_along_axis(g_probs, grp[:, None], axis=-1)
    e_logits = ((xt @ w_re).astype(jnp.float32) + b_re).reshape(n_tok, N_GROUPS, EXPERTS_PER_GROUP)
    e_in = jnp.take_along_axis(e_logits, grp[:, None, None], axis=1)[:, 0]
    top_l, top_i = lax.top_k(e_in, TOP_K_IN_GROUP)
    e_w = jax.nn.softmax(top_l, axis=-1) * g_w
    expert_ids = grp[:, None] * EXPERTS_PER_GROUP + top_i.astype(jnp.int32)

    n_asg = n_tok * TOP_K_IN_GROUP
    flat_e = expert_ids.reshape(n_asg)
    flat_tok = jnp.repeat(jnp.arange(n_tok, dtype=jnp.int32), TOP_K_IN_GROUP)
    flat_w = e_w.reshape(n_asg)
    order = jnp.argsort(flat_e)
    se, stok, sw = flat_e[order], flat_tok[order], flat_w[order]
    counts = jnp.bincount(se, length=N_EXPERTS).astype(jnp.int32)
    starts = jnp.cumsum(counts) - counts
    padded = (counts + MOE_BLOCK - 1) // MOE_BLOCK * MOE_BLOCK
    pends = jnp.cumsum(padded)
    pstarts = pends - padded
    dest = pstarts[se] + jnp.arange(n_asg, dtype=jnp.int32) - starts[se]
    n_blocks = (n_asg + N_EXPERTS * (MOE_BLOCK - 1) + MOE_BLOCK - 1) // MOE_BLOCK
    n_rows = n_blocks * MOE_BLOCK
    row_tok = jnp.full((n_rows,), n_tok, jnp.int32).at[dest].set(stok)
    row_w = jnp.zeros((n_rows,), jnp.float32).at[dest].set(sw)
    blk_e = jnp.minimum(jnp.searchsorted(pends, jnp.arange(n_blocks, dtype=jnp.int32) * MOE_BLOCK,
                                         side='right'), N_EXPERTS - 1).astype(jnp.int32)
    x_pad = jnp.concatenate([xt, jnp.zeros((1, d), xt.dtype)], axis=0)

    def expert_block(args):
        tok, e = args
        xb = x_pad[tok]
        hid = jax.nn.silu(xb @ w_gate[e]) * (xb @ w_up[e])
        return hid @ w_down[e]

    yb = lax.map(expert_block, (row_tok.reshape(n_blocks, MOE_BLOCK), blk_e))
    y = (yb.reshape(n_rows, d) * row_w[:, None].astype(yb.dtype)).astype(h.dtype)
    out = jnp.zeros((n_tok + 1, d), h.dtype).at[row_tok].add(y)
    return out[:n_tok].reshape(b_, s_len, d)


def setup_inputs(seed: int = 0) -> dict:
    key = jax.random.key(seed)
    ks = jax.random.split(key, 24)
    f32 = jnp.float32

    def nrm(k, shape, scale):
        return jax.random.normal(k, shape, f32) * scale

    def gain(k, shape):
        return 1.0 + 0.02 * jax.random.normal(k, shape, f32)

    L = DEPTH
    return {
        'x': nrm(ks[0], (BATCH, SEQ, D_MODEL), 1.0),
        'p': nrm(ks[1], (DEPTH, BATCH, SEQ, PLE_DIM), 1.0),
        'g_mix': gain(ks[2], (L, D_MODEL)),
        'w_in': nrm(ks[3], (L, D_MODEL, W_IN_TOTAL), D_MODEL ** -0.5),
        'conv_w': nrm(ks[4], (L, CONV_K, CONV_DIM), CONV_K ** -0.5),
        'g_ret': gain(ks[5], (L, V_W)),
        'w_out_conv': nrm(ks[6], (L, CONV_DIM, D_MODEL), CONV_DIM ** -0.5),
        'w_out_ret': nrm(ks[7], (L, V_W, D_MODEL), V_W ** -0.5),
        'w_o': nrm(ks[8], (L, D_MODEL, D_MODEL), D_MODEL ** -0.5),
        'g_moe': gain(ks[9], (L, D_MODEL)),
        'w_rg': nrm(ks[10], (L, D_MODEL, N_GROUPS), D_MODEL ** -0.5),
        'b_rg': nrm(ks[11], (L, N_GROUPS), 0.01),
        'w_re': nrm(ks[12], (L, D_MODEL, N_EXPERTS), D_MODEL ** -0.5),
        'b_re': nrm(ks[13], (L, N_EXPERTS), 0.01),
        'w_exp_gate': nrm(ks[14], (L, N_EXPERTS, D_MODEL, D_EXPERT), D_MODEL ** -0.5),
        'w_exp_up': nrm(ks[15], (L, N_EXPERTS, D_MODEL, D_EXPERT), D_MODEL ** -0.5),
        'w_exp_down': nrm(ks[16], (L, N_EXPERTS, D_EXPERT, D_MODEL), D_EXPERT ** -0.5),
        'g_ple_in': gain(ks[17], (L, D_MODEL)),
        'w_ple_gate': nrm(ks[18], (L, D_MODEL, D_MODEL), D_MODEL ** -0.5),
        'w_ple_proj': nrm(ks[19], (L, PLE_DIM, D_MODEL), PLE_DIM ** -0.5),
        'g_ple_post': gain(ks[20], (L, D_MODEL)),
        'g_final': gain(ks[21], (D_MODEL,)),
    }


def reference(x, p, g_mix, w_in, conv_w, g_ret, w_out_conv, w_out_ret, w_o, g_moe, w_rg, b_rg, w_re, b_re,
              w_exp_gate, w_exp_up, w_exp_down, g_ple_in, w_ple_gate, w_ple_proj, g_ple_post, g_final):
    b_, s_len, _ = x.shape
    cos, sin = rope_tables(s_len)
    splits = [int(s) for s in np.cumsum(W_IN_WIDTHS)[:-1]]
    for i in range(DEPTH):
        h = rmsnorm(x, g_mix[i])
        u, c_g, b_g, q, k, v, sg, m_c, m_r = jnp.split(h @ w_in[i], splits, axis=-1)
        y_conv = (b_g * causal_dwconv(c_g * u, conv_w[i])) @ w_out_conv[i]
        q = rope(q.reshape(b_, s_len, RET_HEADS, RET_DK), cos, sin)
        k = rope(k.reshape(b_, s_len, RET_HEADS, RET_DK), cos, sin) * (RET_DK ** -0.5)
        o = retention(q, k, v.reshape(b_, s_len, RET_HEADS, RET_DV))
        o = rmsnorm(o, g_ret[i].reshape(RET_HEADS, RET_DV)).reshape(b_, s_len, V_W)
        y_ret = (jax.nn.silu(sg) * o) @ w_out_ret[i]
        mixed = jax.nn.sigmoid(m_c) * y_conv + jax.nn.sigmoid(m_r) * y_ret
        x = x + mixed @ w_o[i]
        x = x + hier_moe(rmsnorm(x, g_moe[i]), w_rg[i], b_rg[i], w_re[i], b_re[i],
                         w_exp_gate[i], w_exp_up[i], w_exp_down[i])
        ple = rmsnorm(p[i] @ w_ple_proj[i], g_ple_post[i])
        x = x + jax.nn.sigmoid(rmsnorm(x, g_ple_in[i]) @ w_ple_gate[i]) * ple
    return rmsnorm(x, g_final)
```

```python
import functools

import jax
import jax.numpy as jnp
import numpy as np
from jax import lax
from jax.experimental import pallas as pl
from jax.experimental.pallas import tpu as pltpu

EPS = 1e-6
CONV_K = 3
RET_HEADS = 8
RET_DK = 64
RET_DV = 128
RET_CHUNK = 128
ROPE_BASE = 10000.0
N_GROUPS = 4
EXPERTS_PER_GROUP = 8
N_EXPERTS = N_GROUPS * EXPERTS_PER_GROUP
MOE_BLOCK = 128
LANES = 128
SUBLANES = 8
ROUTER_ROWS = 48
MIX_TILE = 256
COMBINE_TILE = 256
INVERSE_TILE = 2048
VMEM_LIMIT = 56 * 1024 * 1024

_bf16 = jnp.bfloat16
_f32 = jnp.float32


def _sigmoid(v):
    return 1.0 / (1.0 + jnp.exp(-v))


def _rms(v, g):
    ms = jnp.mean(v * v, axis=-1, keepdims=True)
    return v * lax.rsqrt(ms + EPS) * g


def _dot(a, b):
    return jnp.dot(a, b, preferred_element_type=_f32)


def _const_spec(shape):
    nd = len(shape)
    return pl.BlockSpec(shape, lambda *_: (0,) * nd, pipeline_mode=pl.Buffered(1))


def _mixer_kernel(x_ref, gmix_ref, win_ref, convw_ref, cos_ref, sin_ref, dmask_ref, qd_ref, kdt_ref,
                  sdec_ref, bmask_ref, gret_ref, woc_ref, wor_ref, wo_ref, gmoe_ref, wrt_ref, rbias_ref,
                  tri_ref,
                  x1_ref, eid_ref, ew_ref, rank_ref, cnt_ref,
                  hb_ref, qr_ref, kr_ref, vb_ref, o_ref, og_ref, acc_ref, cuc_ref, state_ref, carry_ref):
    ts, d = x_ref.shape
    t = pl.program_id(1)

    @pl.when(t == 0)
    def _():
        cuc_ref[...] = jnp.zeros_like(cuc_ref)
        state_ref[...] = jnp.zeros_like(state_ref)
        carry_ref[...] = jnp.zeros_like(carry_ref)

    x = x_ref[...]
    hb_ref[...] = _rms(x, gmix_ref[...]).astype(_bf16)
    hb = hb_ref[...]

    def proj(lo, hi):
        return _dot(hb, win_ref[:, lo:hi])

    cu = proj(d, 2 * d) * proj(0, d)
    prev = cuc_ref[...]
    p1 = prev[SUBLANES - 1:SUBLANES, :]
    p2 = prev[SUBLANES - 2:SUBLANES - 1, :]
    rows = lax.broadcasted_iota(jnp.int32, (ts, d), 0)
    s1 = jnp.where(rows == 0, p1, pltpu.roll(cu, 1, 0))
    s2 = jnp.where(rows == 0, p2, jnp.where(rows == 1, p1, pltpu.roll(cu, 2, 0)))
    conv = convw_ref[0:1, :] * s2 + convw_ref[1:2, :] * s1 + convw_ref[2:3, :] * cu
    cuc_ref[...] = cu[ts - SUBLANES:ts, :]
    a = (proj(2 * d, 3 * d) * conv).astype(_bf16)
    acc_ref[...] = _sigmoid(proj(6 * d, 7 * d)) * _dot(a, woc_ref[...])

    qk0 = 3 * d
    cosv = cos_ref[...]
    sinv = sin_ref[...]
    lane = lax.broadcasted_iota(jnp.int32, (ts, LANES), 1)
    first_half = (lane % RET_DK) < (RET_DK // 2)
    for g in range(4):
        for dst, base, scale in ((qr_ref, qk0, None), (kr_ref, qk0 + 4 * LANES, RET_DK ** -0.5)):
            z = proj(base + g * LANES, base + (g + 1) * LANES)
            zs = jnp.where(first_half, pltpu.roll(z, LANES - RET_DK // 2, 1), pltpu.roll(z, RET_DK // 2, 1))
            r = z * cosv + zs * sinv
            if scale is not None:
                r = r * scale
            dst[:, g * LANES:(g + 1) * LANES] = r
    vb_ref[...] = proj(4 * d, 5 * d).astype(_bf16)

    c = RET_CHUNK
    lane_c = lax.broadcasted_iota(jnp.int32, (c, LANES), 1)
    even = lane_c < RET_DK
    for ci in range(ts // c):
        r0 = ci * c
        for j in range(RET_HEADS // 2):
            q2 = qr_ref[r0:r0 + c, j * LANES:(j + 1) * LANES]
            k2 = kr_ref[r0:r0 + c, j * LANES:(j + 1) * LANES]
            v2 = vb_ref[r0:r0 + c, 2 * j * RET_DV:(2 * j + 2) * RET_DV]
            kt = k2.T
            qq = jnp.concatenate([jnp.where(even, q2, 0.0), jnp.where(even, 0.0, q2)], axis=0).astype(_bf16)
            sc = _dot(qq, kt.astype(_bf16))
            pe = (sc[:c] * dmask_ref[2 * j]).astype(_bf16)
            po = (sc[c:] * dmask_ref[2 * j + 1]).astype(_bf16)
            inner = jnp.concatenate([_dot(pe, v2[:, :RET_DV]), _dot(po, v2[:, RET_DV:])], axis=1)
            st = state_ref[j]
            cross = _dot(q2.astype(_bf16), st.astype(_bf16)) * qd_ref[:, 2 * j * RET_DV:(2 * j + 2) * RET_DV]
            o_ref[r0:r0 + c, 2 * j * RET_DV:(2 * j + 2) * RET_DV] = inner + cross
            upd = _dot((kt * kdt_ref[j]).astype(_bf16), v2)
            state_ref[j] = st * sdec_ref[j] + upd * bmask_ref[...]

    for h in range(RET_HEADS):
        sl = slice(h * RET_DV, (h + 1) * RET_DV)
        sg = proj(5 * d + h * RET_DV, 5 * d + (h + 1) * RET_DV)
        on = _rms(o_ref[:, sl], gret_ref[:, sl])
        og_ref[:, sl] = (sg * _sigmoid(sg) * on).astype(_bf16)
    yr = _dot(og_ref[...], wor_ref[...])
    mixed = acc_ref[...] + _sigmoid(proj(7 * d, 8 * d)) * yr
    x1 = x + _dot(mixed.astype(_bf16), wo_ref[...])
    x1_ref[...] = x1

    h2 = _rms(x1, gmoe_ref[...]).astype(_bf16)
    lt = lax.dot_general(wrt_ref[...], h2, (((1,), (1,)), ((), ())), preferred_element_type=_f32)
    lt = lt + rbias_ref[...]
    g0, g1, g2, g3 = (lt[i:i + 1, :] for i in range(N_GROUPS))
    gmax = jnp.maximum(jnp.maximum(g0, g1), jnp.maximum(g2, g3))
    grp = jnp.where(g0 == gmax, 0, jnp.where(g1 == gmax, 1, jnp.where(g2 == gmax, 2, 3)))
    gsum = jnp.exp(g0 - gmax) + jnp.exp(g1 - gmax) + jnp.exp(g2 - gmax) + jnp.exp(g3 - gmax)
    g_w = 1.0 / gsum
    e_in = lt[SUBLANES:2 * SUBLANES, :]
    for g in range(1, N_GROUPS):
        e_in = jnp.where(grp == g, lt[(g + 1) * SUBLANES:(g + 2) * SUBLANES, :], e_in)
    ridx = lax.broadcasted_iota(jnp.int32, (EXPERTS_PER_GROUP, ts), 0)
    top1 = jnp.max(e_in, axis=0, keepdims=True)
    i1 = jnp.min(jnp.where(e_in == top1, ridx, EXPERTS_PER_GROUP), axis=0, keepdims=True)
    e_m = jnp.where(ridx == i1, -jnp.inf, e_in)
    top2 = jnp.max(e_m, axis=0, keepdims=True)
    i2 = jnp.min(jnp.where(e_m == top2, ridx, EXPERTS_PER_GROUP), axis=0, keepdims=True)
    ex = jnp.exp(top2 - top1)
    den = 1.0 + ex
    id0 = grp * EXPERTS_PER_GROUP + i1
    id1 = grp * EXPERTS_PER_GROUP + i2
    eid_ref[0:1, :] = id0
    eid_ref[1:2, :] = id1
    ew_ref[0:1, :] = (1.0 / den) * g_w
    ew_ref[1:2, :] = (ex / den) * g_w

    eidx = lax.broadcasted_iota(jnp.int32, (N_EXPERTS, ts), 0)
    oh0 = (eidx == id0).astype(_f32)
    oh1 = (eidx == id1).astype(_f32)
    cnt = (oh0 + oh1).astype(_bf16)
    before = carry_ref[...] + _dot(cnt, tri_ref[0])
    rank_ref[0:1, :] = jnp.sum(oh0 * before, axis=0, keepdims=True).astype(jnp.int32)
    rank_ref[1:2, :] = jnp.sum(oh1 * before, axis=0, keepdims=True).astype(jnp.int32)
    total = carry_ref[...] + _dot(cnt, tri_ref[1])
    carry_ref[...] = total
    cnt_ref[...] = total[:, :LANES].astype(jnp.int32)


def _retention_tables():
    c = RET_CHUNK
    log_gamma = np.log1p(-np.exp2(-5.0 - np.arange(RET_HEADS, dtype=np.float64)))
    pos = np.arange(c, dtype=np.float64)
    diff = pos[:, None] - pos[None, :]
    dmask = np.where((diff >= 0)[None], np.exp(log_gamma[:, None, None] * np.maximum(diff, 0.0)[None]), 0.0)
    q_decay = np.exp(log_gamma[:, None] * (pos[None, :] + 1.0))
    k_decay = np.exp(log_gamma[:, None] * (c - 1.0 - pos[None, :]))
    chunk_decay = np.exp(log_gamma * c)
    qd = np.repeat(q_decay.T, RET_DV, axis=1)
    kdt = np.repeat(k_decay.reshape(RET_HEADS // 2, 2, 1, c), RET_DK, axis=2).reshape(RET_HEADS // 2, 2 * RET_DK, c)
    sdec = np.repeat(chunk_decay.reshape(RET_HEADS // 2, 2, 1), RET_DK, axis=2).reshape(RET_HEADS // 2, 2 * RET_DK, 1)
    sdec = np.broadcast_to(sdec, (RET_HEADS // 2, 2 * RET_DK, 2 * RET_DV))
    rr = np.arange(2 * RET_DK)[:, None] // RET_DK
    cc = np.arange(2 * RET_DV)[None, :] // RET_DV
    bmask = (rr == cc).astype(np.float64)
    return tuple(jnp.asarray(np.ascontiguousarray(v), _f32) for v in (dmask, qd, kdt, sdec, bmask))


def _rope_tables(s_len):
    inv = ROPE_BASE ** (-jnp.arange(0, RET_DK, 2, dtype=_f32) / RET_DK)
    ang = jnp.arange(s_len, dtype=_f32)[:, None] * inv[None, :]
    cos, sin = jnp.cos(ang), jnp.sin(ang)
    cos_t = jnp.tile(cos, (1, LANES // (RET_DK // 2)))
    sin_t = jnp.tile(jnp.concatenate([-sin, sin], axis=1), (1, LANES // RET_DK))
    return cos_t, sin_t


def _mixer(x, g_mix, w_in, conv_w, g_ret, w_out_conv, w_out_ret, w_o, g_moe, w_rg, b_rg, w_re, b_re):
    b, s, d = x.shape
    ts = MIX_TILE
    nt = s // ts
    cos_t, sin_t = _rope_tables(s)
    dmask, qd, kdt, sdec, bmask = _retention_tables()
    wrt = jnp.zeros((ROUTER_ROWS, d), _f32)
    wrt = wrt.at[:N_GROUPS].set(w_rg.T).at[SUBLANES:SUBLANES + N_EXPERTS].set(w_re.T).astype(_bf16)
    rb = jnp.zeros((ROUTER_ROWS,), _f32).at[:N_GROUPS].set(b_rg).at[SUBLANES:SUBLANES + N_EXPERTS].set(b_re)
    rbias = jnp.broadcast_to(rb[:, None], (ROUTER_ROWS, ts))
    ii = np.arange(ts)
    tri = jnp.asarray(np.stack([(ii[:, None] < ii[None, :]), np.ones((ts, ts), bool)]), _bf16)

    tile3 = lambda w: pl.BlockSpec((None, ts, w), lambda bi, ti: (bi, ti, 0))
    route = lambda: pl.BlockSpec((None, 2, ts), lambda bi, ti: (bi, 0, ti))
    in_specs = [
        tile3(d),
        _const_spec((1, d)),
        _const_spec(w_in.shape),
        _const_spec(conv_w.shape),
        pl.BlockSpec((ts, LANES), lambda bi, ti: (ti, 0)),
        pl.BlockSpec((ts, LANES), lambda bi, ti: (ti, 0)),
        _const_spec(dmask.shape), _const_spec(qd.shape), _const_spec(kdt.shape), _const_spec(sdec.shape),
        _const_spec(bmask.shape),
        _const_spec((1, d)),
        _const_spec((d, d)), _const_spec((d, d)), _const_spec((d, d)),
        _const_spec((1, d)),
        _const_spec(wrt.shape), _const_spec(rbias.shape), _const_spec(tri.shape),
    ]
    out_shape = [
        jax.ShapeDtypeStruct((b, s, d), _f32),
        jax.ShapeDtypeStruct((b, 2, s), jnp.int32),
        jax.ShapeDtypeStruct((b, 2, s), _f32),
        jax.ShapeDtypeStruct((b, 2, s), jnp.int32),
        jax.ShapeDtypeStruct((b, N_EXPERTS, LANES), jnp.int32),
    ]
    out_specs = [tile3(d), route(), route(), route(),
                 pl.BlockSpec((None, N_EXPERTS, LANES), lambda bi, ti: (bi, 0, 0))]
    scratch = [
        pltpu.VMEM((ts, d), _bf16),
        pltpu.VMEM((ts, RET_HEADS * RET_DK), _f32),
        pltpu.VMEM((ts, RET_HEADS * RET_DK), _f32),
        pltpu.VMEM((ts, RET_HEADS * RET_DV), _bf16),
        pltpu.VMEM((ts, RET_HEADS * RET_DV), _f32),
        pltpu.VMEM((ts, RET_HEADS * RET_DV), _bf16),
        pltpu.VMEM((ts, d), _f32),
        pltpu.VMEM((SUBLANES, d), _f32),
        pltpu.VMEM((RET_HEADS // 2, 2 * RET_DK, 2 * RET_DV), _f32),
        pltpu.VMEM((N_EXPERTS, ts), _f32),
    ]
    return pl.pallas_call(
        _mixer_kernel,
        grid=(b, nt),
        in_specs=in_specs,
        out_specs=out_specs,
        out_shape=out_shape,
        scratch_shapes=scratch,
        compiler_params=pltpu.CompilerParams(
            dimension_semantics=("arbitrary", "arbitrary"), vmem_limit_bytes=VMEM_LIMIT),
        name="mixer_router",
    )(x, g_mix.reshape(1, d), w_in.astype(_bf16), conv_w, cos_t, sin_t, dmask, qd, kdt, sdec, bmask,
      g_ret.reshape(1, d), w_out_conv.astype(_bf16), w_out_ret.astype(_bf16), w_o.astype(_bf16),
      g_moe.reshape(1, d), wrt, rbias, tri)


def _inverse_kernel(dest_ref, tok_ref, *, seq, tile):
    bi = pl.program_id(0)
    ti = pl.program_id(1)
    n_rows = tok_ref.shape[1]
    base = bi * seq

    @pl.when(ti == 0)
    def _():
        def fill(i, _):
            tok_ref[0, i] = base
            return 0
        lax.fori_loop(0, n_rows, fill, 0, unroll=8)

    def put(i, _):
        tok = base + ti * tile + i
        tok_ref[0, dest_ref[0, i]] = tok
        tok_ref[0, dest_ref[1, i]] = tok
        return 0
    lax.fori_loop(0, tile, put, 0, unroll=8)


def _inverse(dest_local, n_rows):
    b, _, s = dest_local.shape
    tile = min(INVERSE_TILE, s)
    return pl.pallas_call(
        functools.partial(_inverse_kernel, seq=s, tile=tile),
        grid=(b, s // tile),
        in_specs=[pl.BlockSpec((None, 2, tile), lambda bi, ti: (bi, 0, ti), memory_space=pltpu.SMEM)],
        out_specs=pl.BlockSpec((None, 1, n_rows), lambda bi, ti: (bi, 0, 0), memory_space=pltpu.SMEM),
        out_shape=jax.ShapeDtypeStruct((b, 1, n_rows), jnp.int32),
        compiler_params=pltpu.CompilerParams(dimension_semantics=("arbitrary", "arbitrary")),
        name="slot_to_token",
    )(dest_local)


def _gather_rows(idx_ref, src_ref, dst_ref, sem, n):
    for r in range(n):
        pltpu.make_async_copy(src_ref.at[pl.ds(idx_ref[0, r], 1)], dst_ref.at[pl.ds(r, 1)], sem).start()


def _wait_rows(src_ref, dst_ref, sem, n):
    for r in range(n):
        pltpu.make_async_copy(src_ref.at[pl.ds(0, 1)], dst_ref.at[pl.ds(r, 1)], sem).wait()


def _expert_kernel(blk_e_ref, used_ref, tok_ref, tok_next_ref, x_ref, gmoe_ref, wg_ref, wu_ref, wd_ref,
                   y_ref, xg_ref, sem):
    i = pl.program_id(0)
    nb = pl.num_programs(0)
    slot = i % 2
    rows = xg_ref.shape[1]

    @pl.when(jnp.logical_and(i == 0, used_ref[0] == 1))
    def _():
        _gather_rows(tok_ref, x_ref, xg_ref.at[0], sem.at[0], rows)

    @pl.when(jnp.logical_and(i + 1 < nb, used_ref[jnp.minimum(i + 1, nb - 1)] == 1))
    def _():
        _gather_rows(tok_next_ref, x_ref, xg_ref.at[1 - slot], sem.at[1 - slot], rows)

    @pl.when(used_ref[i] == 1)
    def _():
        _wait_rows(x_ref, xg_ref.at[slot], sem.at[slot], rows)
        hb = _rms(xg_ref[slot], gmoe_ref[...]).astype(_bf16)
        gate = _dot(hb, wg_ref[...])
        up = _dot(hb, wu_ref[...])
        hid = (gate * _sigmoid(gate) * up).astype(_bf16)
        y_ref[...] = _dot(hid, wd_ref[...])

    @pl.when(used_ref[i] == 0)
    def _():
        y_ref[...] = jnp.zeros_like(y_ref)


def _experts(x1_flat, g_moe, w_gate, w_up, w_down, blk_e, used, row_tok):
    t, d = x1_flat.shape
    nb = blk_e.shape[0]
    de = w_gate.shape[-1]
    tok3 = row_tok.reshape(nb, 1, MOE_BLOCK)
    grid_spec = pltpu.PrefetchScalarGridSpec(
        num_scalar_prefetch=2,
        grid=(nb,),
        in_specs=[
            pl.BlockSpec((None, 1, MOE_BLOCK), lambda i, be, us: (i, 0, 0), memory_space=pltpu.SMEM),
            pl.BlockSpec((None, 1, MOE_BLOCK), lambda i, be, us: (jnp.minimum(i + 1, nb - 1), 0, 0),
                         memory_space=pltpu.SMEM),
            pl.BlockSpec(memory_space=pl.ANY),
            pl.BlockSpec((1, d), lambda i, be, us: (0, 0)),
            pl.BlockSpec((None, d, de), lambda i, be, us: (be[i], 0, 0)),
            pl.BlockSpec((None, d, de), lambda i, be, us: (be[i], 0, 0)),
            pl.BlockSpec((None, de, d), lambda i, be, us: (be[i], 0, 0)),
        ],
        out_specs=pl.BlockSpec((MOE_BLOCK, d), lambda i, be, us: (i, 0)),
        scratch_shapes=[pltpu.VMEM((2, MOE_BLOCK, d), _f32), pltpu.SemaphoreType.DMA((2,))],
    )
    return pl.pallas_call(
        _expert_kernel,
        grid_spec=grid_spec,
        out_shape=jax.ShapeDtypeStruct((nb * MOE_BLOCK, d), _f32),
        compiler_params=pltpu.CompilerParams(dimension_semantics=("arbitrary",), vmem_limit_bytes=VMEM_LIMIT),
        name="experts",
    )(blk_e, used, tok3, tok3, x1_flat, g_moe.reshape(1, d), w_gate.astype(_bf16), w_up.astype(_bf16),
      w_down.astype(_bf16))


def _combine_kernel(dest_ref, dest_next_ref, ew_ref, x1_ref, p_ref, y_ref, gin_ref, wpg_ref, wpp_ref, gpost_ref,
                    gfin_ref, out_ref, yg_ref, sem, *, n_steps):
    tc, d = x1_ref.shape
    i = pl.program_id(0) * pl.num_programs(1) + pl.program_id(1)
    slot = i % 2

    def start(idx_ref, sl):
        for k in range(2):
            for r in range(tc):
                pltpu.make_async_copy(y_ref.at[pl.ds(idx_ref[k, r], 1)], yg_ref.at[sl, k, pl.ds(r, 1)],
                                      sem.at[sl]).start()

    @pl.when(i == 0)
    def _():
        start(dest_ref, 0)

    @pl.when(i + 1 < n_steps)
    def _():
        start(dest_next_ref, 1 - slot)

    for k in range(2):
        for r in range(tc):
            pltpu.make_async_copy(y_ref.at[pl.ds(0, 1)], yg_ref.at[slot, k, pl.ds(r, 1)], sem.at[slot]).wait()

    wcol = [jnp.broadcast_to(ew_ref[k:k + 1, :], (LANES, tc)).T for k in range(2)]
    x2 = x1_ref[...]
    for k in range(2):
        x2 = x2 + jnp.tile(wcol[k], (1, d // LANES)) * yg_ref[slot, k]
    gate = _sigmoid(_dot(_rms(x2, gin_ref[...]).astype(_bf16), wpg_ref[...]))
    ple = _rms(_dot(p_ref[...].astype(_bf16), wpp_ref[...]), gpost_ref[...])
    out_ref[...] = _rms(x2 + gate * ple, gfin_ref[...])


def _combine(x1, p0, y, dest, ew, g_ple_in, w_ple_gate, w_ple_proj, g_ple_post, g_final):
    b, s, d = x1.shape
    tc = COMBINE_TILE
    nt = s // tc
    n_steps = b * nt
    pdim = p0.shape[-1]

    def nxt(bi, ti):
        j = jnp.minimum(bi * nt + ti + 1, n_steps - 1)
        return (j // nt, 0, j % nt)

    vec = lambda: pl.BlockSpec((1, d), lambda bi, ti: (0, 0))
    return pl.pallas_call(
        functools.partial(_combine_kernel, n_steps=n_steps),
        grid=(b, nt),
        in_specs=[
            pl.BlockSpec((None, 2, tc), lambda bi, ti: (bi, 0, ti), memory_space=pltpu.SMEM),
            pl.BlockSpec((None, 2, tc), nxt, memory_space=pltpu.SMEM),
            pl.BlockSpec((None, 2, tc), lambda bi, ti: (bi, 0, ti)),
            pl.BlockSpec((None, tc, d), lambda bi, ti: (bi, ti, 0)),
            pl.BlockSpec((None, tc, pdim), lambda bi, ti: (bi, ti, 0)),
            pl.BlockSpec(memory_space=pl.ANY),
            vec(),
            pl.BlockSpec((d, d), lambda bi, ti: (0, 0)),
            pl.BlockSpec((pdim, d), lambda bi, ti: (0, 0)),
            vec(), vec(),
        ],
        out_specs=pl.BlockSpec((None, tc, d), lambda bi, ti: (bi, ti, 0)),
        out_shape=jax.ShapeDtypeStruct((b, s, d), _f32),
        scratch_shapes=[pltpu.VMEM((2, 2, tc, d), _f32), pltpu.SemaphoreType.DMA((2,))],
        compiler_params=pltpu.CompilerParams(
            dimension_semantics=("arbitrary", "arbitrary"), vmem_limit_bytes=VMEM_LIMIT),
        name="combine_ple",
    )(dest, dest, ew, x1, p0, y, g_ple_in.reshape(1, d), w_ple_gate.astype(_bf16), w_ple_proj.astype(_bf16),
      g_ple_post.reshape(1, d), g_final.reshape(1, d))


def _layer(x, p_i, g_mix, w_in, conv_w, g_ret, w_out_conv, w_out_ret, w_o, g_moe, w_rg, b_rg, w_re, b_re,
           w_exp_gate, w_exp_up, w_exp_down, g_ple_in, w_ple_gate, w_ple_proj, g_ple_post, g_out):
    b, s, d = x.shape
    x1, eid, ew, rank, cnt = _mixer(x, g_mix, w_in, conv_w, g_ret, w_out_conv, w_out_ret, w_o, g_moe,
                                    w_rg, b_rg, w_re, b_re)
    nblk = (2 * s + N_EXPERTS * (MOE_BLOCK - 1) + MOE_BLOCK - 1) // MOE_BLOCK
    n_rows = nblk * MOE_BLOCK
    counts = cnt[:, :, 0]
    padded = (counts + MOE_BLOCK - 1) // MOE_BLOCK * MOE_BLOCK
    pends = jnp.cumsum(padded, axis=-1)
    pstarts = pends - padded
    eids = jnp.arange(N_EXPERTS, dtype=jnp.int32)
    seg = jnp.sum(jnp.where(eid[..., None] == eids, pstarts[:, None, None, :], 0), axis=-1)
    dest_local = rank + seg
    dest = dest_local + (jnp.arange(b, dtype=jnp.int32) * n_rows)[:, None, None]
    blk_start = jnp.arange(nblk, dtype=jnp.int32) * MOE_BLOCK
    blk_e = jnp.minimum(jnp.sum(pends[:, None, :] <= blk_start[None, :, None], axis=-1), N_EXPERTS - 1)
    used = (blk_start[None, :] < pends[:, -1:]).astype(jnp.int32)
    row_tok = _inverse(dest_local, n_rows)
    y = _experts(x1.reshape(b * s, d), g_moe, w_exp_gate, w_exp_up, w_exp_down,
                 blk_e.reshape(-1).astype(jnp.int32), used.reshape(-1), row_tok)
    return _combine(x1, p_i, y, dest, ew, g_ple_in, w_ple_gate, w_ple_proj, g_ple_post, g_out)


def kernel(x, p, g_mix, w_in, conv_w, g_ret, w_out_conv, w_out_ret, w_o, g_moe, w_rg, b_rg, w_re, b_re, w_exp_gate, w_exp_up, w_exp_down, g_ple_in, w_ple_gate, w_ple_proj, g_ple_post, g_final):
    depth = p.shape[0]
    assert depth == 1, "the final norm is fused into the single layer's combine kernel"
    return _layer(x, p[0], g_mix[0], w_in[0], conv_w[0], g_ret[0], w_out_conv[0], w_out_ret[0], w_o[0],
                  g_moe[0], w_rg[0], b_rg[0], w_re[0], b_re[0], w_exp_gate[0], w_exp_up[0], w_exp_down[0],
                  g_ple_in[0], w_ple_gate[0], w_ple_proj[0], g_ple_post[0], g_final)
```

```python
import functools

import jax
import jax.numpy as jnp
import numpy as np
from jax import lax
from jax.experimental import pallas as pl
from jax.experimental.pallas import tpu as pltpu
from jax.experimental.pallas import tpu_sc as plsc

EPS = 1e-6
CONV_K = 3
RET_HEADS = 8
RET_DK = 64
RET_DV = 128
RET_CHUNK = 128
ROPE_BASE = 10000.0
N_GROUPS = 4
EXPERTS_PER_GROUP = 8
N_EXPERTS = N_GROUPS * EXPERTS_PER_GROUP
MOE_BLOCK = 256
LANES = 128
SUBLANES = 8
ROUTER_ROWS = 48
MIX_TILE = 256
COMBINE_TILE = 256
SC_CORES = 2
SC_SUBCORES = 16
SC_WORKERS = SC_CORES * SC_SUBCORES
SC_WINDOW = 64
VMEM_LIMIT = 56 * 1024 * 1024

_bf16 = jnp.bfloat16
_f32 = jnp.float32


def _sigmoid(v):
    return 1.0 / (1.0 + jnp.exp(-v))


def _rms(v, g):
    ms = jnp.mean(v * v, axis=-1, keepdims=True)
    return v * lax.rsqrt(ms + EPS) * g


def _dot(a, b):
    return jnp.dot(a, b, preferred_element_type=_f32)


def _const_spec(shape):
    nd = len(shape)
    return pl.BlockSpec(shape, lambda *_: (0,) * nd, pipeline_mode=pl.Buffered(1))


def _mixer_kernel(x_ref, gmix_ref, win_ref, convw_ref, cos_ref, sin_ref, dmask_ref, qd_ref, kdt_ref,
                  sdec_ref, bmask_ref, gret_ref, woc_ref, wor_ref, wo_ref, gmoe_ref, wrt_ref, rbias_ref,
                  tri_ref,
                  x1_ref, eid_ref, ew_ref, rank_ref, cnt_ref,
                  hb_ref, qr_ref, kr_ref, vb_ref, o_ref, og_ref, acc_ref, cuc_ref, state_ref, carry_ref):
    ts, d = x_ref.shape
    t = pl.program_id(1)

    @pl.when(t == 0)
    def _():
        cuc_ref[...] = jnp.zeros_like(cuc_ref)
        state_ref[...] = jnp.zeros_like(state_ref)

    @pl.when(jnp.logical_and(t == 0, pl.program_id(0) == 0))
    def _():
        carry_ref[...] = jnp.zeros_like(carry_ref)

    x = x_ref[...]
    hb_ref[...] = _rms(x, gmix_ref[...]).astype(_bf16)
    hb = hb_ref[...]

    def proj(lo, hi):
        return _dot(hb, win_ref[:, lo:hi])

    cu = proj(d, 2 * d) * proj(0, d)
    prev = cuc_ref[...]
    p1 = prev[SUBLANES - 1:SUBLANES, :]
    p2 = prev[SUBLANES - 2:SUBLANES - 1, :]
    rows = lax.broadcasted_iota(jnp.int32, (ts, d), 0)
    s1 = jnp.where(rows == 0, p1, pltpu.roll(cu, 1, 0))
    s2 = jnp.where(rows == 0, p2, jnp.where(rows == 1, p1, pltpu.roll(cu, 2, 0)))
    conv = convw_ref[0:1, :] * s2 + convw_ref[1:2, :] * s1 + convw_ref[2:3, :] * cu
    cuc_ref[...] = cu[ts - SUBLANES:ts, :]
    a = (proj(2 * d, 3 * d) * conv).astype(_bf16)
    acc_ref[...] = _sigmoid(proj(6 * d, 7 * d)) * _dot(a, woc_ref[...])

    qk0 = 3 * d
    cosv = cos_ref[...]
    sinv = sin_ref[...]
    lane = lax.broadcasted_iota(jnp.int32, (ts, LANES), 1)
    first_half = (lane % RET_DK) < (RET_DK // 2)
    for g in range(4):
        for dst, base, scale in ((qr_ref, qk0, None), (kr_ref, qk0 + 4 * LANES, RET_DK ** -0.5)):
            z = proj(base + g * LANES, base + (g + 1) * LANES)
            zs = jnp.where(first_half, pltpu.roll(z, LANES - RET_DK // 2, 1), pltpu.roll(z, RET_DK // 2, 1))
            r = z * cosv + zs * sinv
            if scale is not None:
                r = r * scale
            dst[:, g * LANES:(g + 1) * LANES] = r
    vb_ref[...] = proj(4 * d, 5 * d).astype(_bf16)

    c = RET_CHUNK
    lane_c = lax.broadcasted_iota(jnp.int32, (c, LANES), 1)
    even = lane_c < RET_DK
    for ci in range(ts // c):
        r0 = ci * c
        for j in range(RET_HEADS // 2):
            q2 = qr_ref[r0:r0 + c, j * LANES:(j + 1) * LANES]
            k2 = kr_ref[r0:r0 + c, j * LANES:(j + 1) * LANES]
            v2 = vb_ref[r0:r0 + c, 2 * j * RET_DV:(2 * j + 2) * RET_DV]
            kt = k2.T
            qq = jnp.concatenate([jnp.where(even, q2, 0.0), jnp.where(even, 0.0, q2)], axis=0).astype(_bf16)
            sc = _dot(qq, kt.astype(_bf16))
            pe = (sc[:c] * dmask_ref[2 * j]).astype(_bf16)
            po = (sc[c:] * dmask_ref[2 * j + 1]).astype(_bf16)
            inner = jnp.concatenate([_dot(pe, v2[:, :RET_DV]), _dot(po, v2[:, RET_DV:])], axis=1)
            st = state_ref[j]
            cross = _dot(q2.astype(_bf16), st.astype(_bf16)) * qd_ref[:, 2 * j * RET_DV:(2 * j + 2) * RET_DV]
            o_ref[r0:r0 + c, 2 * j * RET_DV:(2 * j + 2) * RET_DV] = inner + cross
            upd = _dot((kt * kdt_ref[j]).astype(_bf16), v2)
            state_ref[j] = st * sdec_ref[j] + upd * bmask_ref[...]

    for h in range(RET_HEADS):
        sl = slice(h * RET_DV, (h + 1) * RET_DV)
        sg = proj(5 * d + h * RET_DV, 5 * d + (h + 1) * RET_DV)
        on = _rms(o_ref[:, sl], gret_ref[:, sl])
        og_ref[:, sl] = (sg * _sigmoid(sg) * on).astype(_bf16)
    yr = _dot(og_ref[...], wor_ref[...])
    mixed = acc_ref[...] + _sigmoid(proj(7 * d, 8 * d)) * yr
    x1 = x + _dot(mixed.astype(_bf16), wo_ref[...])
    x1_ref[...] = x1

    h2 = _rms(x1, gmoe_ref[...]).astype(_bf16)
    lt = lax.dot_general(wrt_ref[...], h2, (((1,), (1,)), ((), ())), preferred_element_type=_f32)
    lt = lt + rbias_ref[...]
    g0, g1, g2, g3 = (lt[i:i + 1, :] for i in range(N_GROUPS))
    gmax = jnp.maximum(jnp.maximum(g0, g1), jnp.maximum(g2, g3))
    grp = jnp.where(g0 == gmax, 0, jnp.where(g1 == gmax, 1, jnp.where(g2 == gmax, 2, 3)))
    gsum = jnp.exp(g0 - gmax) + jnp.exp(g1 - gmax) + jnp.exp(g2 - gmax) + jnp.exp(g3 - gmax)
    g_w = 1.0 / gsum
    e_in = lt[SUBLANES:2 * SUBLANES, :]
    for g in range(1, N_GROUPS):
        e_in = jnp.where(grp == g, lt[(g + 1) * SUBLANES:(g + 2) * SUBLANES, :], e_in)
    ridx = lax.broadcasted_iota(jnp.int32, (EXPERTS_PER_GROUP, ts), 0)
    top1 = jnp.max(e_in, axis=0, keepdims=True)
    i1 = jnp.min(jnp.where(e_in == top1, ridx, EXPERTS_PER_GROUP), axis=0, keepdims=True)
    e_m = jnp.where(ridx == i1, -jnp.inf, e_in)
    top2 = jnp.max(e_m, axis=0, keepdims=True)
    i2 = jnp.min(jnp.where(e_m == top2, ridx, EXPERTS_PER_GROUP), axis=0, keepdims=True)
    ex = jnp.exp(top2 - top1)
    den = 1.0 + ex
    id0 = grp * EXPERTS_PER_GROUP + i1
    id1 = grp * EXPERTS_PER_GROUP + i2
    eid_ref[0:1, :] = id0
    eid_ref[1:2, :] = id1
    ew_ref[0:1, :] = (1.0 / den) * g_w
    ew_ref[1:2, :] = (ex / den) * g_w

    eidx = lax.broadcasted_iota(jnp.int32, (N_EXPERTS, ts), 0)
    oh0 = (eidx == id0).astype(_f32)
    oh1 = (eidx == id1).astype(_f32)
    cnt = (oh0 + oh1).astype(_bf16)
    before = carry_ref[...] + _dot(cnt, tri_ref[0])
    rank_ref[0:1, :] = jnp.sum(oh0 * before, axis=0, keepdims=True).astype(jnp.int32)
    rank_ref[1:2, :] = jnp.sum(oh1 * before, axis=0, keepdims=True).astype(jnp.int32)
    total = carry_ref[...] + _dot(cnt, tri_ref[1])
    carry_ref[...] = total
    cnt_ref[...] = total[:, :LANES].astype(jnp.int32)


def _retention_tables():
    c = RET_CHUNK
    log_gamma = np.log1p(-np.exp2(-5.0 - np.arange(RET_HEADS, dtype=np.float64)))
    pos = np.arange(c, dtype=np.float64)
    diff = pos[:, None] - pos[None, :]
    dmask = np.where((diff >= 0)[None], np.exp(log_gamma[:, None, None] * np.maximum(diff, 0.0)[None]), 0.0)
    q_decay = np.exp(log_gamma[:, None] * (pos[None, :] + 1.0))
    k_decay = np.exp(log_gamma[:, None] * (c - 1.0 - pos[None, :]))
    chunk_decay = np.exp(log_gamma * c)
    qd = np.repeat(q_decay.T, RET_DV, axis=1)
    kdt = np.repeat(k_decay.reshape(RET_HEADS // 2, 2, 1, c), RET_DK, axis=2).reshape(RET_HEADS // 2, 2 * RET_DK, c)
    sdec = np.repeat(chunk_decay.reshape(RET_HEADS // 2, 2, 1), RET_DK, axis=2).reshape(RET_HEADS // 2, 2 * RET_DK, 1)
    sdec = np.broadcast_to(sdec, (RET_HEADS // 2, 2 * RET_DK, 2 * RET_DV))
    rr = np.arange(2 * RET_DK)[:, None] // RET_DK
    cc = np.arange(2 * RET_DV)[None, :] // RET_DV
    bmask = (rr == cc).astype(np.float64)
    return tuple(jnp.asarray(np.ascontiguousarray(v), _f32) for v in (dmask, qd, kdt, sdec, bmask))


def _rope_tables(s_len):
    inv = ROPE_BASE ** (-jnp.arange(0, RET_DK, 2, dtype=_f32) / RET_DK)
    ang = jnp.arange(s_len, dtype=_f32)[:, None] * inv[None, :]
    cos, sin = jnp.cos(ang), jnp.sin(ang)
    cos_t = jnp.tile(cos, (1, LANES // (RET_DK // 2)))
    sin_t = jnp.tile(jnp.concatenate([-sin, sin], axis=1), (1, LANES // RET_DK))
    return cos_t, sin_t


def _mixer(x, g_mix, w_in, conv_w, g_ret, w_out_conv, w_out_ret, w_o, g_moe, w_rg, b_rg, w_re, b_re):
    b, s, d = x.shape
    ts = MIX_TILE
    nt = s // ts
    cos_t, sin_t = _rope_tables(s)
    dmask, qd, kdt, sdec, bmask = _retention_tables()
    wrt = jnp.zeros((ROUTER_ROWS, d), _f32)
    wrt = wrt.at[:N_GROUPS].set(w_rg.T).at[SUBLANES:SUBLANES + N_EXPERTS].set(w_re.T).astype(_bf16)
    rb = jnp.zeros((ROUTER_ROWS,), _f32).at[:N_GROUPS].set(b_rg).at[SUBLANES:SUBLANES + N_EXPERTS].set(b_re)
    rbias = jnp.broadcast_to(rb[:, None], (ROUTER_ROWS, ts))
    ii = np.arange(ts)
    tri = jnp.asarray(np.stack([(ii[:, None] < ii[None, :]), np.ones((ts, ts), bool)]), _bf16)

    tile3 = lambda w: pl.BlockSpec((None, ts, w), lambda bi, ti: (bi, ti, 0))
    route = lambda: pl.BlockSpec((None, 2, ts), lambda bi, ti: (bi, 0, ti))
    in_specs = [
        tile3(d),
        _const_spec((1, d)),
        _const_spec(w_in.shape),
        _const_spec(conv_w.shape),
        pl.BlockSpec((ts, LANES), lambda bi, ti: (ti, 0)),
        pl.BlockSpec((ts, LANES), lambda bi, ti: (ti, 0)),
        _const_spec(dmask.shape), _const_spec(qd.shape), _const_spec(kdt.shape), _const_spec(sdec.shape),
        _const_spec(bmask.shape),
        _const_spec((1, d)),
        _const_spec((d, d)), _const_spec((d, d)), _const_spec((d, d)),
        _const_spec((1, d)),
        _const_spec(wrt.shape), _const_spec(rbias.shape), _const_spec(tri.shape),
    ]
    out_shape = [
        jax.ShapeDtypeStruct((b, s, d), _f32),
        jax.ShapeDtypeStruct((b, 2, s), jnp.int32),
        jax.ShapeDtypeStruct((b, 2, s), _f32),
        jax.ShapeDtypeStruct((b, 2, s), jnp.int32),
        jax.ShapeDtypeStruct((N_EXPERTS, LANES), jnp.int32),
    ]
    out_specs = [tile3(d), route(), route(), route(),
                 pl.BlockSpec((N_EXPERTS, LANES), lambda bi, ti: (0, 0))]
    scratch = [
        pltpu.VMEM((ts, d), _bf16),
        pltpu.VMEM((ts, RET_HEADS * RET_DK), _f32),
        pltpu.VMEM((ts, RET_HEADS * RET_DK), _f32),
        pltpu.VMEM((ts, RET_HEADS * RET_DV), _bf16),
        pltpu.VMEM((ts, RET_HEADS * RET_DV), _f32),
        pltpu.VMEM((ts, RET_HEADS * RET_DV), _bf16),
        pltpu.VMEM((ts, d), _f32),
        pltpu.VMEM((SUBLANES, d), _f32),
        pltpu.VMEM((RET_HEADS // 2, 2 * RET_DK, 2 * RET_DV), _f32),
        pltpu.VMEM((N_EXPERTS, ts), _f32),
    ]
    return pl.pallas_call(
        _mixer_kernel,
        grid=(b, nt),
        in_specs=in_specs,
        out_specs=out_specs,
        out_shape=out_shape,
        scratch_shapes=scratch,
        compiler_params=pltpu.CompilerParams(
            dimension_semantics=("arbitrary", "arbitrary"), vmem_limit_bytes=VMEM_LIMIT),
        name="mixer_router",
    )(x, g_mix.reshape(1, d), w_in.astype(_bf16), conv_w, cos_t, sin_t, dmask, qd, kdt, sdec, bmask,
      g_ret.reshape(1, d), w_out_conv.astype(_bf16), w_out_ret.astype(_bf16), w_o.astype(_bf16),
      g_moe.reshape(1, d), wrt, rbias, tri)


def _sc_worker_id():
    return lax.axis_index("s") * SC_CORES + lax.axis_index("c")


def _sc_mesh():
    return plsc.VectorSubcoreMesh(core_axis_name="c", subcore_axis_name="s")


def _sc_dispatch(src, idx3, n_rows):
    t, d = src.shape
    n_win_total, _, win = idx3.shape
    n_win = n_win_total // SC_WORKERS

    @functools.partial(
        pl.kernel, mesh=_sc_mesh(),
        out_type=jax.ShapeDtypeStruct((n_rows, d), src.dtype),
        scratch_types=[pltpu.VMEM((2, win), jnp.int32), pltpu.VMEM((win, d), src.dtype)],
    )
    def dispatch(src_hbm, idx_hbm, out_hbm, idx_v, rows_v):
        wid = _sc_worker_id()

        @pl.loop(0, n_win)
        def _(i):
            w = wid * n_win + i
            off = pl.multiple_of(w * win, SUBLANES)
            pltpu.sync_copy(idx_hbm.at[w], idx_v)
            pltpu.sync_copy(src_hbm.at[pl.ds(off, win)], rows_v)
            pltpu.sync_copy(rows_v, out_hbm.at[idx_v.at[0]])
            pltpu.sync_copy(rows_v, out_hbm.at[idx_v.at[1]])

    return dispatch(src, idx3)


def _sc_gather(table, idx):
    n = idx.shape[0]
    d = table.shape[1]
    win = SC_WINDOW
    per_w = n // SC_WORKERS
    n_win = per_w // win

    @functools.partial(
        pl.kernel, mesh=_sc_mesh(),
        out_type=jax.ShapeDtypeStruct((n, d), table.dtype),
        scratch_types=[pltpu.VMEM((win,), jnp.int32), pltpu.VMEM((win, d), table.dtype),
                       pltpu.SemaphoreType.DMA],
    )
    def gather(table_hbm, idx_hbm, out_hbm, idx_v, rows_v, sem):
        base = _sc_worker_id() * per_w

        @pl.loop(0, n_win)
        def _(i):
            off = pl.multiple_of(base + i * win, SUBLANES)
            pltpu.sync_copy(idx_hbm.at[pl.ds(off, win)], idx_v)
            pltpu.async_copy(table_hbm.at[idx_v], rows_v, sem).wait()
            pltpu.sync_copy(rows_v, out_hbm.at[pl.ds(off, win)])

    return gather(table, idx)


def _expert_kernel(blk_e_ref, nvalid_ref, xs_ref, gmoe_ref, wg_ref, wu_ref, wd_ref, y_ref):
    nv = nvalid_ref[pl.program_id(0)]

    @pl.when(nv > 0)
    def _():
        rowid = lax.broadcasted_iota(jnp.int32, xs_ref.shape, 0)
        xb = jnp.where(rowid < nv, xs_ref[...], 0.0)
        hb = _rms(xb, gmoe_ref[...]).astype(_bf16)
        gate = _dot(hb, wg_ref[...])
        up = _dot(hb, wu_ref[...])
        hid = (gate * _sigmoid(gate) * up).astype(_bf16)
        y_ref[...] = _dot(hid, wd_ref[...])

    @pl.when(nv == 0)
    def _():
        y_ref[...] = jnp.zeros_like(y_ref)


def _experts(xs, g_moe, w_gate, w_up, w_down, blk_e, nvalid):
    n_rows, d = xs.shape
    nb = blk_e.shape[0]
    de = w_gate.shape[-1]
    grid_spec = pltpu.PrefetchScalarGridSpec(
        num_scalar_prefetch=2,
        grid=(nb,),
        in_specs=[
            pl.BlockSpec((MOE_BLOCK, d), lambda i, be, nv: (i, 0)),
            pl.BlockSpec((1, d), lambda i, be, nv: (0, 0)),
            pl.BlockSpec((None, d, de), lambda i, be, nv: (be[i], 0, 0)),
            pl.BlockSpec((None, d, de), lambda i, be, nv: (be[i], 0, 0)),
            pl.BlockSpec((None, de, d), lambda i, be, nv: (be[i], 0, 0)),
        ],
        out_specs=pl.BlockSpec((MOE_BLOCK, d), lambda i, be, nv: (i, 0)),
    )
    return pl.pallas_call(
        _expert_kernel,
        grid_spec=grid_spec,
        out_shape=jax.ShapeDtypeStruct((n_rows, d), _f32),
        compiler_params=pltpu.CompilerParams(dimension_semantics=("arbitrary",), vmem_limit_bytes=VMEM_LIMIT),
        name="experts",
    )(blk_e, nvalid, xs, g_moe.reshape(1, d), w_gate.astype(_bf16), w_up.astype(_bf16), w_down.astype(_bf16))


def _combine_kernel(ew_ref, x1_ref, p_ref, y2_ref, gin_ref, wpg_ref, wpp_ref, gpost_ref, gfin_ref, out_ref):
    tc, d = x1_ref.shape
    x2 = x1_ref[...]
    for k in range(2):
        wcol = jnp.broadcast_to(ew_ref[k:k + 1, :], (LANES, tc)).T
        x2 = x2 + jnp.tile(wcol, (1, d // LANES)) * y2_ref[k]
    gate = _sigmoid(_dot(_rms(x2, gin_ref[...]).astype(_bf16), wpg_ref[...]))
    ple = _rms(_dot(p_ref[...].astype(_bf16), wpp_ref[...]), gpost_ref[...])
    out_ref[...] = _rms(x2 + gate * ple, gfin_ref[...])


def _combine(x1, p0, y2, ew, g_ple_in, w_ple_gate, w_ple_proj, g_ple_post, g_final):
    b, s, d = x1.shape
    tc = COMBINE_TILE
    pdim = p0.shape[-1]
    vec = lambda: pl.BlockSpec((1, d), lambda bi, ti: (0, 0))
    return pl.pallas_call(
        _combine_kernel,
        grid=(b, s // tc),
        in_specs=[
            pl.BlockSpec((None, 2, tc), lambda bi, ti: (bi, 0, ti)),
            pl.BlockSpec((None, tc, d), lambda bi, ti: (bi, ti, 0)),
            pl.BlockSpec((None, tc, pdim), lambda bi, ti: (bi, ti, 0)),
            pl.BlockSpec((None, 2, tc, d), lambda bi, ti: (bi, 0, ti, 0)),
            vec(),
            pl.BlockSpec((d, d), lambda bi, ti: (0, 0)),
            pl.BlockSpec((pdim, d), lambda bi, ti: (0, 0)),
            vec(), vec(),
        ],
        out_specs=pl.BlockSpec((None, tc, d), lambda bi, ti: (bi, ti, 0)),
        out_shape=jax.ShapeDtypeStruct((b, s, d), _f32),
        compiler_params=pltpu.CompilerParams(
            dimension_semantics=("arbitrary", "arbitrary"), vmem_limit_bytes=VMEM_LIMIT),
        name="combine_ple",
    )(ew, x1, p0, y2, g_ple_in.reshape(1, d), w_ple_gate.astype(_bf16), w_ple_proj.astype(_bf16),
      g_ple_post.reshape(1, d), g_final.reshape(1, d))


def _layer(x, p_i, g_mix, w_in, conv_w, g_ret, w_out_conv, w_out_ret, w_o, g_moe, w_rg, b_rg, w_re, b_re,
           w_exp_gate, w_exp_up, w_exp_down, g_ple_in, w_ple_gate, w_ple_proj, g_ple_post, g_out):
    b, s, d = x.shape
    n_tok = b * s
    x1, eid, ew, rank, cnt = _mixer(x, g_mix, w_in, conv_w, g_ret, w_out_conv, w_out_ret, w_o, g_moe,
                                    w_rg, b_rg, w_re, b_re)
    nblk = (2 * n_tok + N_EXPERTS * (MOE_BLOCK - 1) + MOE_BLOCK - 1) // MOE_BLOCK
    n_rows = nblk * MOE_BLOCK
    counts = cnt[:, 0]
    padded = (counts + MOE_BLOCK - 1) // MOE_BLOCK * MOE_BLOCK
    pends = jnp.cumsum(padded)
    pstarts = pends - padded
    eids = jnp.arange(N_EXPERTS, dtype=jnp.int32)
    dest = rank + jnp.sum(jnp.where(eid[..., None] == eids, pstarts, 0), axis=-1)
    blk_start = jnp.arange(nblk, dtype=jnp.int32) * MOE_BLOCK
    blk_e = jnp.minimum(jnp.sum(pends[None, :] <= blk_start[:, None], axis=-1), N_EXPERTS - 1).astype(jnp.int32)
    nvalid = jnp.clip(pstarts[blk_e] + counts[blk_e] - blk_start, 0, MOE_BLOCK).astype(jnp.int32)

    win = SC_WINDOW
    idx3 = dest.reshape(b, 2, s // win, win).transpose(0, 2, 1, 3).reshape(n_tok // win, 2, win)
    xs = _sc_dispatch(x1.reshape(n_tok, d), idx3, n_rows)
    y = _experts(xs, g_moe, w_exp_gate, w_exp_up, w_exp_down, blk_e, nvalid)
    y2 = _sc_gather(y, dest.reshape(-1)).reshape(b, 2, s, d)
    return _combine(x1, p_i, y2, ew, g_ple_in, w_ple_gate, w_ple_proj, g_ple_post, g_out)


def kernel(x, p, g_mix, w_in, conv_w, g_ret, w_out_conv, w_out_ret, w_o, g_moe, w_rg, b_rg, w_re, b_re, w_exp_gate, w_exp_up, w_exp_down, g_ple_in, w_ple_gate, w_ple_proj, g_ple_post, g_final):
    depth = p.shape[0]
    assert depth == 1, "the final norm is fused into the single layer's combine kernel"
    return _layer(x, p[0], g_mix[0], w_in[0], conv_w[0], g_ret[0], w_out_conv[0], w_out_ret[0], w_o[0],
                  g_moe[0], w_rg[0], b_rg[0], w_re[0], b_re[0], w_exp_gate[0], w_exp_up[0], w_exp_down[0],
                  g_ple_in[0], w_ple_gate[0], w_ple_proj[0], g_ple_post[0], g_final)
```

```python
import functools

import jax
import jax.numpy as jnp
import numpy as np
from jax import lax
from jax.experimental import pallas as pl
from jax.experimental.pallas import tpu as pltpu
from jax.experimental.pallas import tpu_sc as plsc

EPS = 1e-6
CONV_K = 3
RET_HEADS = 8
RET_DK = 64
RET_DV = 128
RET_CHUNK = 128
ROPE_BASE = 10000.0
N_GROUPS = 4
EXPERTS_PER_GROUP = 8
N_EXPERTS = N_GROUPS * EXPERTS_PER_GROUP
MOE_BLOCK = 256
LANES = 128
SUBLANES = 8
ROUTER_ROWS = 48
MIX_TILE = 256
COMBINE_TILE = 256
SC_CORES = 2
SC_SUBCORES = 16
SC_WORKERS = SC_CORES * SC_SUBCORES
SC_WINDOW = 128
BF16_BITS = 16
HIGH_HALF = np.uint32(0xFFFF0000)
VMEM_LIMIT = 56 * 1024 * 1024

_bf16 = jnp.bfloat16
_f32 = jnp.float32


def _sigmoid(v):
    return 1.0 / (1.0 + jnp.exp(-v))


def _rms(v, g):
    ms = jnp.mean(v * v, axis=-1, keepdims=True)
    return v * lax.rsqrt(ms + EPS) * g


def _dot(a, b):
    return jnp.dot(a, b, preferred_element_type=_f32)


def _pack_halves(v):
    bits = lax.bitcast_convert_type(v.astype(_bf16).astype(_f32), jnp.uint32)
    c = v.shape[1] // 2
    return (bits[:, :c] >> BF16_BITS) | (bits[:, c:] & HIGH_HALF)


def _unpack_halves(w):
    lo = lax.bitcast_convert_type(w << BF16_BITS, _f32)
    hi = lax.bitcast_convert_type(w & HIGH_HALF, _f32)
    return lo, hi


def _const_spec(shape):
    nd = len(shape)
    return pl.BlockSpec(shape, lambda *_: (0,) * nd, pipeline_mode=pl.Buffered(1))


def _mixer_kernel(x_ref, gmix_ref, win_ref, convw_ref, cos_ref, sin_ref, dmask_ref, qd_ref, kdt_ref,
                  sdec_ref, bmask_ref, gret_ref, woc_ref, wor_ref, wo_ref, gmoe_ref, wrt_ref, rbias_ref,
                  tri_ref,
                  x1_ref, h2p_ref, eid_ref, ew_ref, rank_ref, cnt_ref,
                  hb_ref, qr_ref, kr_ref, vb_ref, o_ref, og_ref, acc_ref, cuc_ref, state_ref, carry_ref):
    ts, d = x_ref.shape
    t = pl.program_id(1)

    @pl.when(t == 0)
    def _():
        cuc_ref[...] = jnp.zeros_like(cuc_ref)
        state_ref[...] = jnp.zeros_like(state_ref)

    @pl.when(jnp.logical_and(t == 0, pl.program_id(0) == 0))
    def _():
        carry_ref[...] = jnp.zeros_like(carry_ref)

    x = x_ref[...]
    hb_ref[...] = _rms(x, gmix_ref[...]).astype(_bf16)
    hb = hb_ref[...]

    def proj(lo, hi):
        return _dot(hb, win_ref[:, lo:hi])

    cu = proj(d, 2 * d) * proj(0, d)
    prev = cuc_ref[...]
    p1 = prev[SUBLANES - 1:SUBLANES, :]
    p2 = prev[SUBLANES - 2:SUBLANES - 1, :]
    rows = lax.broadcasted_iota(jnp.int32, (ts, d), 0)
    s1 = jnp.where(rows == 0, p1, pltpu.roll(cu, 1, 0))
    s2 = jnp.where(rows == 0, p2, jnp.where(rows == 1, p1, pltpu.roll(cu, 2, 0)))
    conv = convw_ref[0:1, :] * s2 + convw_ref[1:2, :] * s1 + convw_ref[2:3, :] * cu
    cuc_ref[...] = cu[ts - SUBLANES:ts, :]
    a = (proj(2 * d, 3 * d) * conv).astype(_bf16)
    acc_ref[...] = _sigmoid(proj(6 * d, 7 * d)) * _dot(a, woc_ref[...])

    qk0 = 3 * d
    cosv = cos_ref[...]
    sinv = sin_ref[...]
    lane = lax.broadcasted_iota(jnp.int32, (ts, LANES), 1)
    first_half = (lane % RET_DK) < (RET_DK // 2)
    for g in range(4):
        for dst, base, scale in ((qr_ref, qk0, None), (kr_ref, qk0 + 4 * LANES, RET_DK ** -0.5)):
            z = proj(base + g * LANES, base + (g + 1) * LANES)
            zs = jnp.where(first_half, pltpu.roll(z, LANES - RET_DK // 2, 1), pltpu.roll(z, RET_DK // 2, 1))
            r = z * cosv + zs * sinv
            if scale is not None:
                r = r * scale
            dst[:, g * LANES:(g + 1) * LANES] = r
    vb_ref[...] = proj(4 * d, 5 * d).astype(_bf16)

    c = RET_CHUNK
    lane_c = lax.broadcasted_iota(jnp.int32, (c, LANES), 1)
    even = lane_c < RET_DK
    for ci in range(ts // c):
        r0 = ci * c
        for j in range(RET_HEADS // 2):
            q2 = qr_ref[r0:r0 + c, j * LANES:(j + 1) * LANES]
            k2 = kr_ref[r0:r0 + c, j * LANES:(j + 1) * LANES]
            v2 = vb_ref[r0:r0 + c, 2 * j * RET_DV:(2 * j + 2) * RET_DV]
            kt = k2.T
            qq = jnp.concatenate([jnp.where(even, q2, 0.0), jnp.where(even, 0.0, q2)], axis=0).astype(_bf16)
            sc = _dot(qq, kt.astype(_bf16))
            pe = (sc[:c] * dmask_ref[2 * j]).astype(_bf16)
            po = (sc[c:] * dmask_ref[2 * j + 1]).astype(_bf16)
            inner = jnp.concatenate([_dot(pe, v2[:, :RET_DV]), _dot(po, v2[:, RET_DV:])], axis=1)
            st = state_ref[j]
            cross = _dot(q2.astype(_bf16), st.astype(_bf16)) * qd_ref[:, 2 * j * RET_DV:(2 * j + 2) * RET_DV]
            o_ref[r0:r0 + c, 2 * j * RET_DV:(2 * j + 2) * RET_DV] = inner + cross
            upd = _dot((kt * kdt_ref[j]).astype(_bf16), v2)
            state_ref[j] = st * sdec_ref[j] + upd * bmask_ref[...]

    for h in range(RET_HEADS):
        sl = slice(h * RET_DV, (h + 1) * RET_DV)
        sg = proj(5 * d + h * RET_DV, 5 * d + (h + 1) * RET_DV)
        on = _rms(o_ref[:, sl], gret_ref[:, sl])
        og_ref[:, sl] = (sg * _sigmoid(sg) * on).astype(_bf16)
    yr = _dot(og_ref[...], wor_ref[...])
    mixed = acc_ref[...] + _sigmoid(proj(7 * d, 8 * d)) * yr
    x1 = x + _dot(mixed.astype(_bf16), wo_ref[...])
    x1_ref[...] = x1

    h2 = _rms(x1, gmoe_ref[...]).astype(_bf16)
    h2p_ref[...] = _pack_halves(h2)
    lt = lax.dot_general(wrt_ref[...], h2, (((1,), (1,)), ((), ())), preferred_element_type=_f32)
    lt = lt + rbias_ref[...]
    g0, g1, g2, g3 = (lt[i:i + 1, :] for i in range(N_GROUPS))
    gmax = jnp.maximum(jnp.maximum(g0, g1), jnp.maximum(g2, g3))
    grp = jnp.where(g0 == gmax, 0, jnp.where(g1 == gmax, 1, jnp.where(g2 == gmax, 2, 3)))
    gsum = jnp.exp(g0 - gmax) + jnp.exp(g1 - gmax) + jnp.exp(g2 - gmax) + jnp.exp(g3 - gmax)
    g_w = 1.0 / gsum
    e_in = lt[SUBLANES:2 * SUBLANES, :]
    for g in range(1, N_GROUPS):
        e_in = jnp.where(grp == g, lt[(g + 1) * SUBLANES:(g + 2) * SUBLANES, :], e_in)
    ridx = lax.broadcasted_iota(jnp.int32, (EXPERTS_PER_GROUP, ts), 0)
    top1 = jnp.max(e_in, axis=0, keepdims=True)
    i1 = jnp.min(jnp.where(e_in == top1, ridx, EXPERTS_PER_GROUP), axis=0, keepdims=True)
    e_m = jnp.where(ridx == i1, -jnp.inf, e_in)
    top2 = jnp.max(e_m, axis=0, keepdims=True)
    i2 = jnp.min(jnp.where(e_m == top2, ridx, EXPERTS_PER_GROUP), axis=0, keepdims=True)
    ex = jnp.exp(top2 - top1)
    den = 1.0 + ex
    id0 = grp * EXPERTS_PER_GROUP + i1
    id1 = grp * EXPERTS_PER_GROUP + i2
    eid_ref[0:1, :] = id0
    eid_ref[1:2, :] = id1
    ew_ref[0:1, :] = (1.0 / den) * g_w
    ew_ref[1:2, :] = (ex / den) * g_w

    eidx = lax.broadcasted_iota(jnp.int32, (N_EXPERTS, ts), 0)
    oh0 = (eidx == id0).astype(_f32)
    oh1 = (eidx == id1).astype(_f32)
    cnt = (oh0 + oh1).astype(_bf16)
    before = carry_ref[...] + _dot(cnt, tri_ref[0])
    rank_ref[0:1, :] = jnp.sum(oh0 * before, axis=0, keepdims=True).astype(jnp.int32)
    rank_ref[1:2, :] = jnp.sum(oh1 * before, axis=0, keepdims=True).astype(jnp.int32)
    total = carry_ref[...] + _dot(cnt, tri_ref[1])
    carry_ref[...] = total
    cnt_ref[...] = total[:, :LANES].astype(jnp.int32)


def _retention_tables():
    c = RET_CHUNK
    log_gamma = np.log1p(-np.exp2(-5.0 - np.arange(RET_HEADS, dtype=np.float64)))
    pos = np.arange(c, dtype=np.float64)
    diff = pos[:, None] - pos[None, :]
    dmask = np.where((diff >= 0)[None], np.exp(log_gamma[:, None, None] * np.maximum(diff, 0.0)[None]), 0.0)
    q_decay = np.exp(log_gamma[:, None] * (pos[None, :] + 1.0))
    k_decay = np.exp(log_gamma[:, None] * (c - 1.0 - pos[None, :]))
    chunk_decay = np.exp(log_gamma * c)
    qd = np.repeat(q_decay.T, RET_DV, axis=1)
    kdt = np.repeat(k_decay.reshape(RET_HEADS // 2, 2, 1, c), RET_DK, axis=2).reshape(RET_HEADS // 2, 2 * RET_DK, c)
    sdec = np.repeat(chunk_decay.reshape(RET_HEADS // 2, 2, 1), RET_DK, axis=2).reshape(RET_HEADS // 2, 2 * RET_DK, 1)
    sdec = np.broadcast_to(sdec, (RET_HEADS // 2, 2 * RET_DK, 2 * RET_DV))
    rr = np.arange(2 * RET_DK)[:, None] // RET_DK
    cc = np.arange(2 * RET_DV)[None, :] // RET_DV
    bmask = (rr == cc).astype(np.float64)
    return tuple(jnp.asarray(np.ascontiguousarray(v), _f32) for v in (dmask, qd, kdt, sdec, bmask))


def _rope_tables(s_len):
    inv = ROPE_BASE ** (-jnp.arange(0, RET_DK, 2, dtype=_f32) / RET_DK)
    ang = jnp.arange(s_len, dtype=_f32)[:, None] * inv[None, :]
    cos, sin = jnp.cos(ang), jnp.sin(ang)
    cos_t = jnp.tile(cos, (1, LANES // (RET_DK // 2)))
    sin_t = jnp.tile(jnp.concatenate([-sin, sin], axis=1), (1, LANES // RET_DK))
    return cos_t, sin_t


def _mixer(x, g_mix, w_in, conv_w, g_ret, w_out_conv, w_out_ret, w_o, g_moe, w_rg, b_rg, w_re, b_re):
    b, s, d = x.shape
    ts = MIX_TILE
    nt = s // ts
    cos_t, sin_t = _rope_tables(s)
    dmask, qd, kdt, sdec, bmask = _retention_tables()
    wrt = jnp.zeros((ROUTER_ROWS, d), _f32)
    wrt = wrt.at[:N_GROUPS].set(w_rg.T).at[SUBLANES:SUBLANES + N_EXPERTS].set(w_re.T).astype(_bf16)
    rb = jnp.zeros((ROUTER_ROWS,), _f32).at[:N_GROUPS].set(b_rg).at[SUBLANES:SUBLANES + N_EXPERTS].set(b_re)
    rbias = jnp.broadcast_to(rb[:, None], (ROUTER_ROWS, ts))
    ii = np.arange(ts)
    tri = jnp.asarray(np.stack([(ii[:, None] < ii[None, :]), np.ones((ts, ts), bool)]), _bf16)

    tile3 = lambda w: pl.BlockSpec((None, ts, w), lambda bi, ti: (bi, ti, 0))
    route = lambda: pl.BlockSpec((None, 2, ts), lambda bi, ti: (bi, 0, ti))
    in_specs = [
        tile3(d),
        _const_spec((1, d)),
        _const_spec(w_in.shape),
        _const_spec(conv_w.shape),
        pl.BlockSpec((ts, LANES), lambda bi, ti: (ti, 0)),
        pl.BlockSpec((ts, LANES), lambda bi, ti: (ti, 0)),
        _const_spec(dmask.shape), _const_spec(qd.shape), _const_spec(kdt.shape), _const_spec(sdec.shape),
        _const_spec(bmask.shape),
        _const_spec((1, d)),
        _const_spec((d, d)), _const_spec((d, d)), _const_spec((d, d)),
        _const_spec((1, d)),
        _const_spec(wrt.shape), _const_spec(rbias.shape), _const_spec(tri.shape),
    ]
    out_shape = [
        jax.ShapeDtypeStruct((b, s, d), _f32),
        jax.ShapeDtypeStruct((b, s, d // 2), jnp.uint32),
        jax.ShapeDtypeStruct((b, 2, s), jnp.int32),
        jax.ShapeDtypeStruct((b, 2, s), _f32),
        jax.ShapeDtypeStruct((b, 2, s), jnp.int32),
        jax.ShapeDtypeStruct((N_EXPERTS, LANES), jnp.int32),
    ]
    out_specs = [tile3(d), tile3(d // 2), route(), route(), route(),
                 pl.BlockSpec((N_EXPERTS, LANES), lambda bi, ti: (0, 0))]
    scratch = [
        pltpu.VMEM((ts, d), _bf16),
        pltpu.VMEM((ts, RET_HEADS * RET_DK), _f32),
        pltpu.VMEM((ts, RET_HEADS * RET_DK), _f32),
        pltpu.VMEM((ts, RET_HEADS * RET_DV), _bf16),
        pltpu.VMEM((ts, RET_HEADS * RET_DV), _f32),
        pltpu.VMEM((ts, RET_HEADS * RET_DV), _bf16),
        pltpu.VMEM((ts, d), _f32),
        pltpu.VMEM((SUBLANES, d), _f32),
        pltpu.VMEM((RET_HEADS // 2, 2 * RET_DK, 2 * RET_DV), _f32),
        pltpu.VMEM((N_EXPERTS, ts), _f32),
    ]
    return pl.pallas_call(
        _mixer_kernel,
        grid=(b, nt),
        in_specs=in_specs,
        out_specs=out_specs,
        out_shape=out_shape,
        scratch_shapes=scratch,
        compiler_params=pltpu.CompilerParams(
            dimension_semantics=("arbitrary", "arbitrary"), vmem_limit_bytes=VMEM_LIMIT),
        name="mixer_router",
    )(x, g_mix.reshape(1, d), w_in.astype(_bf16), conv_w, cos_t, sin_t, dmask, qd, kdt, sdec, bmask,
      g_ret.reshape(1, d), w_out_conv.astype(_bf16), w_out_ret.astype(_bf16), w_o.astype(_bf16),
      g_moe.reshape(1, d), wrt, rbias, tri)


def _sc_worker_id():
    return lax.axis_index("s") * SC_CORES + lax.axis_index("c")


def _sc_mesh():
    return plsc.VectorSubcoreMesh(core_axis_name="c", subcore_axis_name="s")


def _sc_dispatch(src, idx3, n_rows):
    t, d = src.shape
    n_win_total, _, win = idx3.shape
    n_win = n_win_total // SC_WORKERS

    @functools.partial(
        pl.kernel, mesh=_sc_mesh(),
        out_type=jax.ShapeDtypeStruct((n_rows, d), src.dtype),
        scratch_types=[pltpu.VMEM((2, win), jnp.int32), pltpu.VMEM((win, d), src.dtype)],
    )
    def dispatch(src_hbm, idx_hbm, out_hbm, idx_v, rows_v):
        wid = _sc_worker_id()

        @pl.loop(0, n_win)
        def _(i):
            w = wid * n_win + i
            off = pl.multiple_of(w * win, SUBLANES)
            pltpu.sync_copy(idx_hbm.at[w], idx_v)
            pltpu.sync_copy(src_hbm.at[pl.ds(off, win)], rows_v)
            pltpu.sync_copy(rows_v, out_hbm.at[idx_v.at[0]])
            pltpu.sync_copy(rows_v, out_hbm.at[idx_v.at[1]])

    return dispatch(src, idx3)


def _sc_gather(table, idx):
    n = idx.shape[0]
    d = table.shape[1]
    win = SC_WINDOW
    per_w = n // SC_WORKERS
    n_win = per_w // win

    @functools.partial(
        pl.kernel, mesh=_sc_mesh(),
        out_type=jax.ShapeDtypeStruct((n, d), table.dtype),
        scratch_types=[pltpu.VMEM((win,), jnp.int32), pltpu.VMEM((win, d), table.dtype),
                       pltpu.SemaphoreType.DMA],
    )
    def gather(table_hbm, idx_hbm, out_hbm, idx_v, rows_v, sem):
        base = _sc_worker_id() * per_w

        @pl.loop(0, n_win)
        def _(i):
            off = pl.multiple_of(base + i * win, SUBLANES)
            pltpu.sync_copy(idx_hbm.at[pl.ds(off, win)], idx_v)
            pltpu.async_copy(table_hbm.at[idx_v], rows_v, sem).wait()
            pltpu.sync_copy(rows_v, out_hbm.at[pl.ds(off, win)])

    return gather(table, idx)


def _expert_kernel(blk_e_ref, nvalid_ref, xs_ref, wg_ref, wu_ref, wd_ref, y_ref, wgb_ref, wub_ref, wdb_ref):
    i = pl.program_id(0)
    nv = nvalid_ref[i]
    half = xs_ref.shape[1]

    @pl.when(jnp.logical_or(i == 0, blk_e_ref[i] != blk_e_ref[jnp.maximum(i - 1, 0)]))
    def _():
        wgb_ref[...] = wg_ref[...].astype(_bf16)
        wub_ref[...] = wu_ref[...].astype(_bf16)
        wdb_ref[...] = wd_ref[...].astype(_bf16)

    @pl.when(nv > 0)
    def _():
        rowid = lax.broadcasted_iota(jnp.int32, xs_ref.shape, 0)
        lo, hi = _unpack_halves(jnp.where(rowid < nv, xs_ref[...], jnp.uint32(0)))
        lo = lo.astype(_bf16)
        hi = hi.astype(_bf16)
        gate = _dot(lo, wgb_ref[:half, :]) + _dot(hi, wgb_ref[half:, :])
        up = _dot(lo, wub_ref[:half, :]) + _dot(hi, wub_ref[half:, :])
        hid = (gate * _sigmoid(gate) * up).astype(_bf16)
        y_ref[...] = _pack_halves(_dot(hid, wdb_ref[...]))

    @pl.when(nv == 0)
    def _():
        y_ref[...] = jnp.zeros_like(y_ref)


def _experts(xs, w_gate, w_up, w_down, blk_e, nvalid):
    n_rows, half = xs.shape
    d = 2 * half
    nb = blk_e.shape[0]
    de = w_gate.shape[-1]
    grid_spec = pltpu.PrefetchScalarGridSpec(
        num_scalar_prefetch=2,
        grid=(nb,),
        in_specs=[
            pl.BlockSpec((MOE_BLOCK, half), lambda i, be, nv: (i, 0)),
            pl.BlockSpec((None, d, de), lambda i, be, nv: (be[i], 0, 0)),
            pl.BlockSpec((None, d, de), lambda i, be, nv: (be[i], 0, 0)),
            pl.BlockSpec((None, de, d), lambda i, be, nv: (be[i], 0, 0)),
        ],
        out_specs=pl.BlockSpec((MOE_BLOCK, half), lambda i, be, nv: (i, 0)),
        scratch_shapes=[pltpu.VMEM((d, de), _bf16), pltpu.VMEM((d, de), _bf16), pltpu.VMEM((de, d), _bf16)],
    )
    return pl.pallas_call(
        _expert_kernel,
        grid_spec=grid_spec,
        out_shape=jax.ShapeDtypeStruct((n_rows, half), jnp.uint32),
        compiler_params=pltpu.CompilerParams(dimension_semantics=("arbitrary",), vmem_limit_bytes=VMEM_LIMIT),
        name="experts",
    )(blk_e, nvalid, xs, w_gate, w_up, w_down)


def _combine_kernel(ew_ref, x1_ref, p_ref, y2_ref, gin_ref, wpg_ref, wpp_ref, gpost_ref, gfin_ref, out_ref):
    tc, d = x1_ref.shape
    x2 = x1_ref[...]
    for k in range(2):
        wcol = jnp.broadcast_to(ew_ref[k:k + 1, :], (LANES, tc)).T
        yk = jnp.concatenate(_unpack_halves(y2_ref[k]), axis=1)
        x2 = x2 + jnp.tile(wcol, (1, d // LANES)) * yk
    gate = _sigmoid(_dot(_rms(x2, gin_ref[...]).astype(_bf16), wpg_ref[...]))
    ple = _rms(_dot(p_ref[...].astype(_bf16), wpp_ref[...]), gpost_ref[...])
    out_ref[...] = _rms(x2 + gate * ple, gfin_ref[...])


def _combine(x1, p0, y2, ew, g_ple_in, w_ple_gate, w_ple_proj, g_ple_post, g_final):
    b, s, d = x1.shape
    tc = COMBINE_TILE
    pdim = p0.shape[-1]
    vec = lambda: pl.BlockSpec((1, d), lambda bi, ti: (0, 0))
    return pl.pallas_call(
        _combine_kernel,
        grid=(b, s // tc),
        in_specs=[
            pl.BlockSpec((None, 2, tc), lambda bi, ti: (bi, 0, ti)),
            pl.BlockSpec((None, tc, d), lambda bi, ti: (bi, ti, 0)),
            pl.BlockSpec((None, tc, pdim), lambda bi, ti: (bi, ti, 0)),
            pl.BlockSpec((None, 2, tc, d // 2), lambda bi, ti: (bi, 0, ti, 0)),
            vec(),
            pl.BlockSpec((d, d), lambda bi, ti: (0, 0)),
            pl.BlockSpec((pdim, d), lambda bi, ti: (0, 0)),
            vec(), vec(),
        ],
        out_specs=pl.BlockSpec((None, tc, d), lambda bi, ti: (bi, ti, 0)),
        out_shape=jax.ShapeDtypeStruct((b, s, d), _f32),
        compiler_params=pltpu.CompilerParams(
            dimension_semantics=("arbitrary", "arbitrary"), vmem_limit_bytes=VMEM_LIMIT),
        name="combine_ple",
    )(ew, x1, p0, y2, g_ple_in.reshape(1, d), w_ple_gate.astype(_bf16), w_ple_proj.astype(_bf16),
      g_ple_post.reshape(1, d), g_final.reshape(1, d))


def _layer(x, p_i, g_mix, w_in, conv_w, g_ret, w_out_conv, w_out_ret, w_o, g_moe, w_rg, b_rg, w_re, b_re,
           w_exp_gate, w_exp_up, w_exp_down, g_ple_in, w_ple_gate, w_ple_proj, g_ple_post, g_out):
    b, s, d = x.shape
    n_tok = b * s
    x1, h2p, eid, ew, rank, cnt = _mixer(x, g_mix, w_in, conv_w, g_ret, w_out_conv, w_out_ret, w_o, g_moe,
                                         w_rg, b_rg, w_re, b_re)
    nblk = (2 * n_tok + N_EXPERTS * (MOE_BLOCK - 1) + MOE_BLOCK - 1) // MOE_BLOCK
    n_rows = nblk * MOE_BLOCK
    counts = cnt[:, 0]
    padded = (counts + MOE_BLOCK - 1) // MOE_BLOCK * MOE_BLOCK
    pends = jnp.cumsum(padded)
    pstarts = pends - padded
    eids = jnp.arange(N_EXPERTS, dtype=jnp.int32)
    dest = rank + jnp.sum(jnp.where(eid[..., None] == eids, pstarts, 0), axis=-1)
    blk_start = jnp.arange(nblk, dtype=jnp.int32) * MOE_BLOCK
    blk_e = jnp.minimum(jnp.sum(pends[None, :] <= blk_start[:, None], axis=-1), N_EXPERTS - 1).astype(jnp.int32)
    nvalid = jnp.clip(pstarts[blk_e] + counts[blk_e] - blk_start, 0, MOE_BLOCK).astype(jnp.int32)

    win = SC_WINDOW
    idx3 = dest.reshape(b, 2, s // win, win).transpose(0, 2, 1, 3).reshape(n_tok // win, 2, win)
    xs = _sc_dispatch(h2p.reshape(n_tok, d // 2), idx3, n_rows)
    y = _experts(xs, w_exp_gate, w_exp_up, w_exp_down, blk_e, nvalid)
    y2 = _sc_gather(y, dest.reshape(-1)).reshape(b, 2, s, d // 2)
    return _combine(x1, p_i, y2, ew, g_ple_in, w_ple_gate, w_ple_proj, g_ple_post, g_out)


def kernel(x, p, g_mix, w_in, conv_w, g_ret, w_out_conv, w_out_ret, w_o, g_moe, w_rg, b_rg, w_re, b_re, w_exp_gate, w_exp_up, w_exp_down, g_ple_in, w_ple_gate, w_ple_proj, g_ple_post, g_final):
    depth = p.shape[0]
    assert depth == 1, "the final norm is fused into the single layer's combine kernel"
    return _layer(x, p[0], g_mix[0], w_in[0], conv_w[0], g_ret[0], w_out_conv[0], w_out_ret[0], w_o[0],
                  g_moe[0], w_rg[0], b_rg[0], w_re[0], b_re[0], w_exp_gate[0], w_exp_up[0], w_exp_down[0],
                  g_ple_in[0], w_ple_gate[0], w_ple_proj[0], g_ple_post[0], g_final)
```

```python
import functools

import jax
import jax.numpy as jnp
import numpy as np
from jax import lax
from jax.experimental import pallas as pl
from jax.experimental.pallas import tpu as pltpu
from jax.experimental.pallas import tpu_sc as plsc

EPS = 1e-6
CONV_K = 3
RET_HEADS = 8
RET_DK = 64
RET_DV = 128
RET_CHUNK = 128
ROPE_BASE = 10000.0
N_GROUPS = 4
EXPERTS_PER_GROUP = 8
N_EXPERTS = N_GROUPS * EXPERTS_PER_GROUP
MOE_BLOCK = 256
LANES = 128
SUBLANES = 8
ROUTER_ROWS = 48
MIX_TILE = 256
COMBINE_TILE = 256
TOKEN_SPLITS = 2
SC_CORES = 2
SC_SUBCORES = 16
SC_WORKERS = SC_CORES * SC_SUBCORES
SC_WINDOW = 128
BF16_BITS = 16
HIGH_HALF = np.uint32(0xFFFF0000)
VMEM_LIMIT = 56 * 1024 * 1024

_bf16 = jnp.bfloat16
_f32 = jnp.float32


def _sigmoid(v):
    return 1.0 / (1.0 + jnp.exp(-v))


def _rms(v, g):
    ms = jnp.mean(v * v, axis=-1, keepdims=True)
    return v * lax.rsqrt(ms + EPS) * g


def _dot(a, b):
    return jnp.dot(a, b, preferred_element_type=_f32)


def _pack_halves(v):
    bits = lax.bitcast_convert_type(v.astype(_bf16).astype(_f32), jnp.uint32)
    c = v.shape[1] // 2
    return (bits[:, :c] >> BF16_BITS) | (bits[:, c:] & HIGH_HALF)


def _unpack_halves(w):
    lo = lax.bitcast_convert_type(w << BF16_BITS, _f32)
    hi = lax.bitcast_convert_type(w & HIGH_HALF, _f32)
    return lo, hi


def _const_spec(shape):
    nd = len(shape)
    return pl.BlockSpec(shape, lambda *_: (0,) * nd, pipeline_mode=pl.Buffered(1))


def _mixer_kernel(x_ref, gmix_ref, win_ref, convw_ref, cos_ref, sin_ref, dmask_ref, qd_ref, kdt_ref,
                  sdec_ref, bmask_ref, gret_ref, woc_ref, wor_ref, wo_ref, gmoe_ref, wrt_ref, rbias_ref,
                  tri_ref,
                  x1_ref, h2p_ref, eid_ref, ew_ref, rank_ref, cnt_ref,
                  hb_ref, qr_ref, kr_ref, vb_ref, o_ref, og_ref, acc_ref, cuc_ref, state_ref, carry_ref):
    ts, d = x_ref.shape
    t = pl.program_id(1)

    @pl.when(t == 0)
    def _():
        cuc_ref[...] = jnp.zeros_like(cuc_ref)
        state_ref[...] = jnp.zeros_like(state_ref)

    @pl.when(jnp.logical_and(t == 0, pl.program_id(0) == 0))
    def _():
        carry_ref[...] = jnp.zeros_like(carry_ref)

    x = x_ref[...]
    hb_ref[...] = _rms(x, gmix_ref[...]).astype(_bf16)
    hb = hb_ref[...]

    def proj(lo, hi):
        return _dot(hb, win_ref[:, lo:hi])

    cu = proj(d, 2 * d) * proj(0, d)
    prev = cuc_ref[...]
    p1 = prev[SUBLANES - 1:SUBLANES, :]
    p2 = prev[SUBLANES - 2:SUBLANES - 1, :]
    rows = lax.broadcasted_iota(jnp.int32, (ts, d), 0)
    s1 = jnp.where(rows == 0, p1, pltpu.roll(cu, 1, 0))
    s2 = jnp.where(rows == 0, p2, jnp.where(rows == 1, p1, pltpu.roll(cu, 2, 0)))
    conv = convw_ref[0:1, :] * s2 + convw_ref[1:2, :] * s1 + convw_ref[2:3, :] * cu
    cuc_ref[...] = cu[ts - SUBLANES:ts, :]
    a = (proj(2 * d, 3 * d) * conv).astype(_bf16)
    acc_ref[...] = _sigmoid(proj(6 * d, 7 * d)) * _dot(a, woc_ref[...])

    qk0 = 3 * d
    cosv = cos_ref[...]
    sinv = sin_ref[...]
    lane = lax.broadcasted_iota(jnp.int32, (ts, LANES), 1)
    first_half = (lane % RET_DK) < (RET_DK // 2)
    for g in range(4):
        for dst, base, scale in ((qr_ref, qk0, None), (kr_ref, qk0 + 4 * LANES, RET_DK ** -0.5)):
            z = proj(base + g * LANES, base + (g + 1) * LANES)
            zs = jnp.where(first_half, pltpu.roll(z, LANES - RET_DK // 2, 1), pltpu.roll(z, RET_DK // 2, 1))
            r = z * cosv + zs * sinv
            if scale is not None:
                r = r * scale
            dst[:, g * LANES:(g + 1) * LANES] = r
    vb_ref[...] = proj(4 * d, 5 * d).astype(_bf16)

    c = RET_CHUNK
    lane_c = lax.broadcasted_iota(jnp.int32, (c, LANES), 1)
    even = lane_c < RET_DK
    for ci in range(ts // c):
        r0 = ci * c
        for j in range(RET_HEADS // 2):
            q2 = qr_ref[r0:r0 + c, j * LANES:(j + 1) * LANES]
            k2 = kr_ref[r0:r0 + c, j * LANES:(j + 1) * LANES]
            v2 = vb_ref[r0:r0 + c, 2 * j * RET_DV:(2 * j + 2) * RET_DV]
            kt = k2.T
            qq = jnp.concatenate([jnp.where(even, q2, 0.0), jnp.where(even, 0.0, q2)], axis=0).astype(_bf16)
            sc = _dot(qq, kt.astype(_bf16))
            pe = (sc[:c] * dmask_ref[2 * j]).astype(_bf16)
            po = (sc[c:] * dmask_ref[2 * j + 1]).astype(_bf16)
            inner = jnp.concatenate([_dot(pe, v2[:, :RET_DV]), _dot(po, v2[:, RET_DV:])], axis=1)
            st = state_ref[j]
            cross = _dot(q2.astype(_bf16), st.astype(_bf16)) * qd_ref[:, 2 * j * RET_DV:(2 * j + 2) * RET_DV]
            o_ref[r0:r0 + c, 2 * j * RET_DV:(2 * j + 2) * RET_DV] = inner + cross
            upd = _dot((kt * kdt_ref[j]).astype(_bf16), v2)
            state_ref[j] = st * sdec_ref[j] + upd * bmask_ref[...]

    for h in range(RET_HEADS):
        sl = slice(h * RET_DV, (h + 1) * RET_DV)
        sg = proj(5 * d + h * RET_DV, 5 * d + (h + 1) * RET_DV)
        on = _rms(o_ref[:, sl], gret_ref[:, sl])
        og_ref[:, sl] = (sg * _sigmoid(sg) * on).astype(_bf16)
    yr = _dot(og_ref[...], wor_ref[...])
    mixed = acc_ref[...] + _sigmoid(proj(7 * d, 8 * d)) * yr
    x1 = x + _dot(mixed.astype(_bf16), wo_ref[...])
    x1_ref[...] = x1

    h2 = _rms(x1, gmoe_ref[...]).astype(_bf16)
    h2p_ref[...] = _pack_halves(h2)
    lt = lax.dot_general(wrt_ref[...], h2, (((1,), (1,)), ((), ())), preferred_element_type=_f32)
    lt = lt + rbias_ref[...]
    g0, g1, g2, g3 = (lt[i:i + 1, :] for i in range(N_GROUPS))
    gmax = jnp.maximum(jnp.maximum(g0, g1), jnp.maximum(g2, g3))
    grp = jnp.where(g0 == gmax, 0, jnp.where(g1 == gmax, 1, jnp.where(g2 == gmax, 2, 3)))
    gsum = jnp.exp(g0 - gmax) + jnp.exp(g1 - gmax) + jnp.exp(g2 - gmax) + jnp.exp(g3 - gmax)
    g_w = 1.0 / gsum
    e_in = lt[SUBLANES:2 * SUBLANES, :]
    for g in range(1, N_GROUPS):
        e_in = jnp.where(grp == g, lt[(g + 1) * SUBLANES:(g + 2) * SUBLANES, :], e_in)
    ridx = lax.broadcasted_iota(jnp.int32, (EXPERTS_PER_GROUP, ts), 0)
    top1 = jnp.max(e_in, axis=0, keepdims=True)
    i1 = jnp.min(jnp.where(e_in == top1, ridx, EXPERTS_PER_GROUP), axis=0, keepdims=True)
    e_m = jnp.where(ridx == i1, -jnp.inf, e_in)
    top2 = jnp.max(e_m, axis=0, keepdims=True)
    i2 = jnp.min(jnp.where(e_m == top2, ridx, EXPERTS_PER_GROUP), axis=0, keepdims=True)
    ex = jnp.exp(top2 - top1)
    den = 1.0 + ex
    id0 = grp * EXPERTS_PER_GROUP + i1
    id1 = grp * EXPERTS_PER_GROUP + i2
    eid_ref[0:1, :] = id0
    eid_ref[1:2, :] = id1
    ew_ref[0:1, :] = (1.0 / den) * g_w
    ew_ref[1:2, :] = (ex / den) * g_w

    eidx = lax.broadcasted_iota(jnp.int32, (N_EXPERTS, ts), 0)
    oh0 = (eidx == id0).astype(_f32)
    oh1 = (eidx == id1).astype(_f32)
    cnt = (oh0 + oh1).astype(_bf16)
    before = carry_ref[...] + _dot(cnt, tri_ref[0])
    rank_ref[0:1, :] = jnp.sum(oh0 * before, axis=0, keepdims=True).astype(jnp.int32)
    rank_ref[1:2, :] = jnp.sum(oh1 * before, axis=0, keepdims=True).astype(jnp.int32)
    total = carry_ref[...] + _dot(cnt, tri_ref[1])
    carry_ref[...] = total
    cnt_ref[...] = total[:, :LANES].astype(jnp.int32)


def _retention_tables():
    c = RET_CHUNK
    log_gamma = np.log1p(-np.exp2(-5.0 - np.arange(RET_HEADS, dtype=np.float64)))
    pos = np.arange(c, dtype=np.float64)
    diff = pos[:, None] - pos[None, :]
    dmask = np.where((diff >= 0)[None], np.exp(log_gamma[:, None, None] * np.maximum(diff, 0.0)[None]), 0.0)
    q_decay = np.exp(log_gamma[:, None] * (pos[None, :] + 1.0))
    k_decay = np.exp(log_gamma[:, None] * (c - 1.0 - pos[None, :]))
    chunk_decay = np.exp(log_gamma * c)
    qd = np.repeat(q_decay.T, RET_DV, axis=1)
    kdt = np.repeat(k_decay.reshape(RET_HEADS // 2, 2, 1, c), RET_DK, axis=2).reshape(RET_HEADS // 2, 2 * RET_DK, c)
    sdec = np.repeat(chunk_decay.reshape(RET_HEADS // 2, 2, 1), RET_DK, axis=2).reshape(RET_HEADS // 2, 2 * RET_DK, 1)
    sdec = np.broadcast_to(sdec, (RET_HEADS // 2, 2 * RET_DK, 2 * RET_DV))
    rr = np.arange(2 * RET_DK)[:, None] // RET_DK
    cc = np.arange(2 * RET_DV)[None, :] // RET_DV
    bmask = (rr == cc).astype(np.float64)
    return tuple(jnp.asarray(np.ascontiguousarray(v), _f32) for v in (dmask, qd, kdt, sdec, bmask))


def _rope_tables(s_len):
    inv = ROPE_BASE ** (-jnp.arange(0, RET_DK, 2, dtype=_f32) / RET_DK)
    ang = jnp.arange(s_len, dtype=_f32)[:, None] * inv[None, :]
    cos, sin = jnp.cos(ang), jnp.sin(ang)
    cos_t = jnp.tile(cos, (1, LANES // (RET_DK // 2)))
    sin_t = jnp.tile(jnp.concatenate([-sin, sin], axis=1), (1, LANES // RET_DK))
    return cos_t, sin_t


def _mixer_operands(s, d, g_mix, w_in, conv_w, g_ret, w_out_conv, w_out_ret, w_o, g_moe, w_rg, b_rg, w_re, b_re):
    ts = MIX_TILE
    cos_t, sin_t = _rope_tables(s)
    dmask, qd, kdt, sdec, bmask = _retention_tables()
    wrt = jnp.zeros((ROUTER_ROWS, d), _f32)
    wrt = wrt.at[:N_GROUPS].set(w_rg.T).at[SUBLANES:SUBLANES + N_EXPERTS].set(w_re.T).astype(_bf16)
    rb = jnp.zeros((ROUTER_ROWS,), _f32).at[:N_GROUPS].set(b_rg).at[SUBLANES:SUBLANES + N_EXPERTS].set(b_re)
    rbias = jnp.broadcast_to(rb[:, None], (ROUTER_ROWS, ts))
    ii = np.arange(ts)
    tri = jnp.asarray(np.stack([(ii[:, None] < ii[None, :]), np.ones((ts, ts), bool)]), _bf16)
    return (g_mix.reshape(1, d), w_in.astype(_bf16), conv_w, cos_t, sin_t, dmask, qd, kdt, sdec, bmask,
            g_ret.reshape(1, d), w_out_conv.astype(_bf16), w_out_ret.astype(_bf16), w_o.astype(_bf16),
            g_moe.reshape(1, d), wrt, rbias, tri)


def _mixer(x, b0, b, operands):
    _, s, d = x.shape
    ts = MIX_TILE
    nt = s // ts
    tile3 = lambda w: pl.BlockSpec((None, ts, w), lambda bi, ti: (bi, ti, 0))
    route = lambda: pl.BlockSpec((None, 2, ts), lambda bi, ti: (bi, 0, ti))
    rope_spec = pl.BlockSpec((ts, LANES), lambda bi, ti: (ti, 0))
    in_specs = [pl.BlockSpec((None, ts, d), lambda bi, ti: (b0 + bi, ti, 0))]
    in_specs += [rope_spec if i in (3, 4) else _const_spec(op.shape) for i, op in enumerate(operands)]
    out_shape = [
        jax.ShapeDtypeStruct((b, s, d), _f32),
        jax.ShapeDtypeStruct((b, s, d // 2), jnp.uint32),
        jax.ShapeDtypeStruct((b, 2, s), jnp.int32),
        jax.ShapeDtypeStruct((b, 2, s), _f32),
        jax.ShapeDtypeStruct((b, 2, s), jnp.int32),
        jax.ShapeDtypeStruct((N_EXPERTS, LANES), jnp.int32),
    ]
    out_specs = [tile3(d), tile3(d // 2), route(), route(), route(),
                 pl.BlockSpec((N_EXPERTS, LANES), lambda bi, ti: (0, 0))]
    scratch = [
        pltpu.VMEM((ts, d), _bf16),
        pltpu.VMEM((ts, RET_HEADS * RET_DK), _f32),
        pltpu.VMEM((ts, RET_HEADS * RET_DK), _f32),
        pltpu.VMEM((ts, RET_HEADS * RET_DV), _bf16),
        pltpu.VMEM((ts, RET_HEADS * RET_DV), _f32),
        pltpu.VMEM((ts, RET_HEADS * RET_DV), _bf16),
        pltpu.VMEM((ts, d), _f32),
        pltpu.VMEM((SUBLANES, d), _f32),
        pltpu.VMEM((RET_HEADS // 2, 2 * RET_DK, 2 * RET_DV), _f32),
        pltpu.VMEM((N_EXPERTS, ts), _f32),
    ]
    return pl.pallas_call(
        _mixer_kernel,
        grid=(b, nt),
        in_specs=in_specs,
        out_specs=out_specs,
        out_shape=out_shape,
        scratch_shapes=scratch,
        compiler_params=pltpu.CompilerParams(
            dimension_semantics=("arbitrary", "arbitrary"), vmem_limit_bytes=VMEM_LIMIT),
        name="mixer_router",
    )(x, *operands)


def _sc_worker_id():
    return lax.axis_index("s") * SC_CORES + lax.axis_index("c")


def _sc_mesh():
    return plsc.VectorSubcoreMesh(core_axis_name="c", subcore_axis_name="s")


def _sc_dispatch(src, idx3, n_rows):
    t, d = src.shape
    n_win_total, _, win = idx3.shape
    n_win = n_win_total // SC_WORKERS

    @functools.partial(
        pl.kernel, mesh=_sc_mesh(),
        out_type=jax.ShapeDtypeStruct((n_rows, d), src.dtype),
        scratch_types=[pltpu.VMEM((2, win), jnp.int32), pltpu.VMEM((win, d), src.dtype)],
    )
    def dispatch(src_hbm, idx_hbm, out_hbm, idx_v, rows_v):
        wid = _sc_worker_id()

        @pl.loop(0, n_win)
        def _(i):
            w = wid * n_win + i
            off = pl.multiple_of(w * win, SUBLANES)
            pltpu.sync_copy(idx_hbm.at[w], idx_v)
            pltpu.sync_copy(src_hbm.at[pl.ds(off, win)], rows_v)
            pltpu.sync_copy(rows_v, out_hbm.at[idx_v.at[0]])
            pltpu.sync_copy(rows_v, out_hbm.at[idx_v.at[1]])

    return dispatch(src, idx3)


def _sc_gather(table, idx):
    n = idx.shape[0]
    d = table.shape[1]
    win = SC_WINDOW
    per_w = n // SC_WORKERS
    n_win = per_w // win

    @functools.partial(
        pl.kernel, mesh=_sc_mesh(),
        out_type=jax.ShapeDtypeStruct((n, d), table.dtype),
        scratch_types=[pltpu.VMEM((win,), jnp.int32), pltpu.VMEM((win, d), table.dtype),
                       pltpu.SemaphoreType.DMA],
    )
    def gather(table_hbm, idx_hbm, out_hbm, idx_v, rows_v, sem):
        base = _sc_worker_id() * per_w

        @pl.loop(0, n_win)
        def _(i):
            off = pl.multiple_of(base + i * win, SUBLANES)
            pltpu.sync_copy(idx_hbm.at[pl.ds(off, win)], idx_v)
            pltpu.async_copy(table_hbm.at[idx_v], rows_v, sem).wait()
            pltpu.sync_copy(rows_v, out_hbm.at[pl.ds(off, win)])

    return gather(table, idx)


def _expert_kernel(blk_e_ref, nvalid_ref, xs_ref, wg_ref, wu_ref, wd_ref, y_ref, wgb_ref, wub_ref, wdb_ref):
    i = pl.program_id(0)
    nv = nvalid_ref[i]
    half = xs_ref.shape[1]

    @pl.when(jnp.logical_or(i == 0, blk_e_ref[i] != blk_e_ref[jnp.maximum(i - 1, 0)]))
    def _():
        wgb_ref[...] = wg_ref[...].astype(_bf16)
        wub_ref[...] = wu_ref[...].astype(_bf16)
        wdb_ref[...] = wd_ref[...].astype(_bf16)

    @pl.when(nv > 0)
    def _():
        rowid = lax.broadcasted_iota(jnp.int32, xs_ref.shape, 0)
        lo, hi = _unpack_halves(jnp.where(rowid < nv, xs_ref[...], jnp.uint32(0)))
        lo = lo.astype(_bf16)
        hi = hi.astype(_bf16)
        gate = _dot(lo, wgb_ref[:half, :]) + _dot(hi, wgb_ref[half:, :])
        up = _dot(lo, wub_ref[:half, :]) + _dot(hi, wub_ref[half:, :])
        hid = (gate * _sigmoid(gate) * up).astype(_bf16)
        y_ref[...] = _pack_halves(_dot(hid, wdb_ref[...]))

    @pl.when(nv == 0)
    def _():
        y_ref[...] = jnp.zeros_like(y_ref)


def _experts(xs, w_gate, w_up, w_down, blk_e, nvalid):
    n_rows, half = xs.shape
    d = 2 * half
    nb = blk_e.shape[0]
    de = w_gate.shape[-1]
    grid_spec = pltpu.PrefetchScalarGridSpec(
        num_scalar_prefetch=2,
        grid=(nb,),
        in_specs=[
            pl.BlockSpec((MOE_BLOCK, half), lambda i, be, nv: (i, 0)),
            pl.BlockSpec((None, d, de), lambda i, be, nv: (be[i], 0, 0)),
            pl.BlockSpec((None, d, de), lambda i, be, nv: (be[i], 0, 0)),
            pl.BlockSpec((None, de, d), lambda i, be, nv: (be[i], 0, 0)),
        ],
        out_specs=pl.BlockSpec((MOE_BLOCK, half), lambda i, be, nv: (i, 0)),
        scratch_shapes=[pltpu.VMEM((d, de), _bf16), pltpu.VMEM((d, de), _bf16), pltpu.VMEM((de, d), _bf16)],
    )
    return pl.pallas_call(
        _expert_kernel,
        grid_spec=grid_spec,
        out_shape=jax.ShapeDtypeStruct((n_rows, half), jnp.uint32),
        compiler_params=pltpu.CompilerParams(dimension_semantics=("arbitrary",), vmem_limit_bytes=VMEM_LIMIT),
        name="experts",
    )(blk_e, nvalid, xs, w_gate, w_up, w_down)


def _combine_kernel(ew_ref, x1_ref, p_ref, y2_ref, gin_ref, wpg_ref, wpp_ref, gpost_ref, gfin_ref, *rest):
    out_ref = rest[-1]
    tc, d = x1_ref.shape
    x2 = x1_ref[...]
    for k in range(2):
        wcol = jnp.broadcast_to(ew_ref[k:k + 1, :], (LANES, tc)).T
        yk = jnp.concatenate(_unpack_halves(y2_ref[k]), axis=1)
        x2 = x2 + jnp.tile(wcol, (1, d // LANES)) * yk
    gate = _sigmoid(_dot(_rms(x2, gin_ref[...]).astype(_bf16), wpg_ref[...]))
    ple = _rms(_dot(p_ref[...].astype(_bf16), wpp_ref[...]), gpost_ref[...])
    out_ref[...] = _rms(x2 + gate * ple, gfin_ref[...])


def _combine(x1, p0, y2, ew, gains_and_weights, b0, b_total, prev_out):
    b, s, d = x1.shape
    tc = COMBINE_TILE
    pdim = p0.shape[-1]
    vec = lambda: pl.BlockSpec((1, d), lambda bi, ti: (0, 0))
    in_specs = [
        pl.BlockSpec((None, 2, tc), lambda bi, ti: (bi, 0, ti)),
        pl.BlockSpec((None, tc, d), lambda bi, ti: (bi, ti, 0)),
        pl.BlockSpec((None, tc, pdim), lambda bi, ti: (b0 + bi, ti, 0)),
        pl.BlockSpec((None, 2, tc, d // 2), lambda bi, ti: (bi, 0, ti, 0)),
        vec(),
        pl.BlockSpec((d, d), lambda bi, ti: (0, 0)),
        pl.BlockSpec((pdim, d), lambda bi, ti: (0, 0)),
        vec(), vec(),
    ]
    args = [ew, x1, p0, y2, *gains_and_weights]
    aliases = {}
    if prev_out is not None:
        in_specs.append(pl.BlockSpec(memory_space=pl.ANY))
        aliases = {len(args): 0}
        args.append(prev_out)
    return pl.pallas_call(
        _combine_kernel,
        grid=(b, s // tc),
        in_specs=in_specs,
        out_specs=pl.BlockSpec((None, tc, d), lambda bi, ti: (b0 + bi, ti, 0)),
        out_shape=jax.ShapeDtypeStruct((b_total, s, d), _f32),
        input_output_aliases=aliases,
        compiler_params=pltpu.CompilerParams(
            dimension_semantics=("arbitrary", "arbitrary"), vmem_limit_bytes=VMEM_LIMIT),
        name="combine_ple",
    )(*args)


def _layer(x, p_i, g_mix, w_in, conv_w, g_ret, w_out_conv, w_out_ret, w_o, g_moe, w_rg, b_rg, w_re, b_re,
           w_exp_gate, w_exp_up, w_exp_down, g_ple_in, w_ple_gate, w_ple_proj, g_ple_post, g_out):
    b_total, s, d = x.shape
    b = b_total // TOKEN_SPLITS
    n_tok = b * s
    operands = _mixer_operands(s, d, g_mix, w_in, conv_w, g_ret, w_out_conv, w_out_ret, w_o, g_moe,
                               w_rg, b_rg, w_re, b_re)
    tail = (g_ple_in.reshape(1, d), w_ple_gate.astype(_bf16), w_ple_proj.astype(_bf16),
            g_ple_post.reshape(1, d), g_out.reshape(1, d))
    nblk = (2 * n_tok + N_EXPERTS * (MOE_BLOCK - 1) + MOE_BLOCK - 1) // MOE_BLOCK
    n_rows = nblk * MOE_BLOCK
    win = SC_WINDOW

    def route(eid, rank, cnt):
        counts = cnt[:, 0]
        padded = (counts + MOE_BLOCK - 1) // MOE_BLOCK * MOE_BLOCK
        pends = jnp.cumsum(padded)
        pstarts = pends - padded
        eids = jnp.arange(N_EXPERTS, dtype=jnp.int32)
        dest = rank + jnp.sum(jnp.where(eid[..., None] == eids, pstarts, 0), axis=-1)
        blk_start = jnp.arange(nblk, dtype=jnp.int32) * MOE_BLOCK
        blk_e = jnp.minimum(jnp.sum(pends[None, :] <= blk_start[:, None], axis=-1), N_EXPERTS - 1)
        blk_e = blk_e.astype(jnp.int32)
        nvalid = jnp.clip(pstarts[blk_e] + counts[blk_e] - blk_start, 0, MOE_BLOCK).astype(jnp.int32)
        idx3 = dest.reshape(b, 2, s // win, win).transpose(0, 2, 1, 3).reshape(n_tok // win, 2, win)
        return dest, idx3, blk_e, nvalid

    st = [dict() for _ in range(TOKEN_SPLITS)]
    out = None

    def mix(h):
        x1, h2p, eid, ew, rank, cnt = _mixer(x, h * b, b, operands)
        dest, idx3, blk_e, nvalid = route(eid, rank, cnt)
        st[h].update(x1=x1, h2p=h2p, ew=ew, dest=dest, idx3=idx3, blk_e=blk_e, nvalid=nvalid)

    def dispatch(h):
        st[h]["xs"] = _sc_dispatch(st[h]["h2p"].reshape(n_tok, d // 2), st[h]["idx3"], n_rows)

    def experts(h):
        st[h]["y"] = _experts(st[h]["xs"], w_exp_gate, w_exp_up, w_exp_down, st[h]["blk_e"], st[h]["nvalid"])

    def gather(h):
        st[h]["y2"] = _sc_gather(st[h]["y"], st[h]["dest"].reshape(-1)).reshape(b, 2, s, d // 2)

    def combine(h):
        nonlocal out
        out = _combine(st[h]["x1"], p_i, st[h]["y2"], st[h]["ew"], tail, h * b, b_total, out)

    stages = (mix, dispatch, experts, gather, combine)
    for step in range(TOKEN_SPLITS + len(stages) - 1):
        for k in reversed(range(len(stages))):
            if 0 <= step - k < TOKEN_SPLITS:
                stages[k](step - k)
    return out


def kernel(x, p, g_mix, w_in, conv_w, g_ret, w_out_conv, w_out_ret, w_o, g_moe, w_rg, b_rg, w_re, b_re, w_exp_gate, w_exp_up, w_exp_down, g_ple_in, w_ple_gate, w_ple_proj, g_ple_post, g_final):
    depth = p.shape[0]
    assert depth == 1, "the final norm is fused into the single layer's combine kernel"
    return _layer(x, p[0], g_mix[0], w_in[0], conv_w[0], g_ret[0], w_out_conv[0], w_out_ret[0], w_o[0],
                  g_moe[0], w_rg[0], b_rg[0], w_re[0], b_re[0], w_exp_gate[0], w_exp_up[0], w_exp_down[0],
                  g_ple_in[0], w_ple_gate[0], w_ple_proj[0], g_ple_post[0], g_final)
```

```python
import functools

import jax
import jax.numpy as jnp
import numpy as np
from jax import lax
from jax.experimental import pallas as pl
from jax.experimental.pallas import tpu as pltpu
from jax.experimental.pallas import tpu_sc as plsc

EPS = 1e-6
CONV_K = 3
RET_HEADS = 8
RET_DK = 64
RET_DV = 128
RET_CHUNK = 128
ROPE_BASE = 10000.0
N_GROUPS = 4
EXPERTS_PER_GROUP = 8
N_EXPERTS = N_GROUPS * EXPERTS_PER_GROUP
MOE_BLOCK = 256
LANES = 128
SUBLANES = 8
ROUTER_ROWS = 48
MIX_TILE = 256
COMBINE_TILE = 256
TOKEN_SPLITS = 2
SC_CORES = 2
SC_SUBCORES = 16
SC_WORKERS = SC_CORES * SC_SUBCORES
SC_WINDOW = 128
BF16_BITS = 16
HIGH_HALF = np.uint32(0xFFFF0000)
VMEM_LIMIT = 56 * 1024 * 1024

_bf16 = jnp.bfloat16
_f32 = jnp.float32


def _sigmoid(v):
    return 1.0 / (1.0 + jnp.exp(-v))


def _rms(v, g):
    ms = jnp.mean(v * v, axis=-1, keepdims=True)
    return v * lax.rsqrt(ms + EPS) * g


def _dot(a, b):
    return jnp.dot(a, b, preferred_element_type=_f32)


def _pack_halves(v):
    bits = lax.bitcast_convert_type(v.astype(_bf16).astype(_f32), jnp.uint32)
    c = v.shape[1] // 2
    return (bits[:, :c] >> BF16_BITS) | (bits[:, c:] & HIGH_HALF)


def _unpack_halves(w):
    lo = lax.bitcast_convert_type(w << BF16_BITS, _f32)
    hi = lax.bitcast_convert_type(w & HIGH_HALF, _f32)
    return lo, hi


def _const_spec(shape):
    nd = len(shape)
    return pl.BlockSpec(shape, lambda *_: (0,) * nd, pipeline_mode=pl.Buffered(1))


def _mixer_kernel(x_ref, gmix_ref, win_ref, convw_ref, cos_ref, sin_ref, dmask_ref, qd_ref, kdt_ref,
                  sdec_ref, bmask_ref, gret_ref, woc_ref, wor_ref, wo_ref, gmoe_ref, wrt_ref, rbias_ref,
                  tri_ref,
                  x1_ref, h2p_ref, eid_ref, ew_ref, rank_ref, cnt_ref,
                  hb_ref, qr_ref, kr_ref, vb_ref, o_ref, og_ref, acc_ref, cuc_ref, state_ref, carry_ref):
    ts, d = x_ref.shape
    t = pl.program_id(1)

    @pl.when(t == 0)
    def _():
        cuc_ref[...] = jnp.zeros_like(cuc_ref)
        state_ref[...] = jnp.zeros_like(state_ref)

    @pl.when(jnp.logical_and(t == 0, pl.program_id(0) == 0))
    def _():
        carry_ref[...] = jnp.zeros_like(carry_ref)

    x = x_ref[...]
    hb_ref[...] = _rms(x, gmix_ref[...]).astype(_bf16)
    hb = hb_ref[...]

    def proj(lo, hi):
        return _dot(hb, win_ref[:, lo:hi])

    cu = proj(d, 2 * d) * proj(0, d)
    prev = cuc_ref[...]
    p1 = prev[SUBLANES - 1:SUBLANES, :]
    p2 = prev[SUBLANES - 2:SUBLANES - 1, :]
    rows = lax.broadcasted_iota(jnp.int32, (ts, d), 0)
    s1 = jnp.where(rows == 0, p1, pltpu.roll(cu, 1, 0))
    s2 = jnp.where(rows == 0, p2, jnp.where(rows == 1, p1, pltpu.roll(cu, 2, 0)))
    conv = convw_ref[0:1, :] * s2 + convw_ref[1:2, :] * s1 + convw_ref[2:3, :] * cu
    cuc_ref[...] = cu[ts - SUBLANES:ts, :]
    a = (proj(2 * d, 3 * d) * conv).astype(_bf16)
    acc_ref[...] = _sigmoid(proj(6 * d, 7 * d)) * _dot(a, woc_ref[...])

    qk0 = 3 * d
    cosv = cos_ref[...]
    sinv = sin_ref[...]
    lane = lax.broadcasted_iota(jnp.int32, (ts, LANES), 1)
    first_half = (lane % RET_DK) < (RET_DK // 2)
    for g in range(4):
        for dst, base, scale in ((qr_ref, qk0, None), (kr_ref, qk0 + 4 * LANES, RET_DK ** -0.5)):
            z = proj(base + g * LANES, base + (g + 1) * LANES)
            zs = jnp.where(first_half, pltpu.roll(z, LANES - RET_DK // 2, 1), pltpu.roll(z, RET_DK // 2, 1))
            r = z * cosv + zs * sinv
            if scale is not None:
                r = r * scale
            dst[:, g * LANES:(g + 1) * LANES] = r
    vb_ref[...] = proj(4 * d, 5 * d).astype(_bf16)

    c = RET_CHUNK
    lane_c = lax.broadcasted_iota(jnp.int32, (c, LANES), 1)
    even = lane_c < RET_DK
    for ci in range(ts // c):
        r0 = ci * c
        for j in range(RET_HEADS // 2):
            q2 = qr_ref[r0:r0 + c, j * LANES:(j + 1) * LANES]
            k2 = kr_ref[r0:r0 + c, j * LANES:(j + 1) * LANES]
            v2 = vb_ref[r0:r0 + c, 2 * j * RET_DV:(2 * j + 2) * RET_DV]
            kt = k2.T
            qq = jnp.concatenate([jnp.where(even, q2, 0.0), jnp.where(even, 0.0, q2)], axis=0).astype(_bf16)
            sc = _dot(qq, kt.astype(_bf16))
            pe = (sc[:c] * dmask_ref[2 * j]).astype(_bf16)
            po = (sc[c:] * dmask_ref[2 * j + 1]).astype(_bf16)
            inner = jnp.concatenate([_dot(pe, v2[:, :RET_DV]), _dot(po, v2[:, RET_DV:])], axis=1)
            st = state_ref[j]
            cross = _dot(q2.astype(_bf16), st.astype(_bf16)) * qd_ref[:, 2 * j * RET_DV:(2 * j + 2) * RET_DV]
            o_ref[r0:r0 + c, 2 * j * RET_DV:(2 * j + 2) * RET_DV] = inner + cross
            upd = _dot((kt * kdt_ref[j]).astype(_bf16), v2)
            state_ref[j] = st * sdec_ref[j] + upd * bmask_ref[...]

    for h in range(RET_HEADS):
        sl = slice(h * RET_DV, (h + 1) * RET_DV)
        sg = proj(5 * d + h * RET_DV, 5 * d + (h + 1) * RET_DV)
        on = _rms(o_ref[:, sl], gret_ref[:, sl])
        og_ref[:, sl] = (sg * _sigmoid(sg) * on).astype(_bf16)
    yr = _dot(og_ref[...], wor_ref[...])
    mixed = acc_ref[...] + _sigmoid(proj(7 * d, 8 * d)) * yr
    x1 = x + _dot(mixed.astype(_bf16), wo_ref[...])
    x1_ref[...] = x1

    h2 = _rms(x1, gmoe_ref[...]).astype(_bf16)
    h2p_ref[...] = _pack_halves(h2)
    lt = lax.dot_general(wrt_ref[...], h2, (((1,), (1,)), ((), ())), preferred_element_type=_f32)
    lt = lt + rbias_ref[...]
    g0, g1, g2, g3 = (lt[i:i + 1, :] for i in range(N_GROUPS))
    gmax = jnp.maximum(jnp.maximum(g0, g1), jnp.maximum(g2, g3))
    grp = jnp.where(g0 == gmax, 0, jnp.where(g1 == gmax, 1, jnp.where(g2 == gmax, 2, 3)))
    gsum = jnp.exp(g0 - gmax) + jnp.exp(g1 - gmax) + jnp.exp(g2 - gmax) + jnp.exp(g3 - gmax)
    g_w = 1.0 / gsum
    e_in = lt[SUBLANES:2 * SUBLANES, :]
    for g in range(1, N_GROUPS):
        e_in = jnp.where(grp == g, lt[(g + 1) * SUBLANES:(g + 2) * SUBLANES, :], e_in)
    ridx = lax.broadcasted_iota(jnp.int32, (EXPERTS_PER_GROUP, ts), 0)
    top1 = jnp.max(e_in, axis=0, keepdims=True)
    i1 = jnp.min(jnp.where(e_in == top1, ridx, EXPERTS_PER_GROUP), axis=0, keepdims=True)
    e_m = jnp.where(ridx == i1, -jnp.inf, e_in)
    top2 = jnp.max(e_m, axis=0, keepdims=True)
    i2 = jnp.min(jnp.where(e_m == top2, ridx, EXPERTS_PER_GROUP), axis=0, keepdims=True)
    ex = jnp.exp(top2 - top1)
    den = 1.0 + ex
    id0 = grp * EXPERTS_PER_GROUP + i1
    id1 = grp * EXPERTS_PER_GROUP + i2
    eid_ref[0:1, :] = id0
    eid_ref[1:2, :] = id1
    ew_ref[0:1, :] = (1.0 / den) * g_w
    ew_ref[1:2, :] = (ex / den) * g_w

    eidx = lax.broadcasted_iota(jnp.int32, (N_EXPERTS, ts), 0)
    oh0 = (eidx == id0).astype(_f32)
    oh1 = (eidx == id1).astype(_f32)
    cnt = (oh0 + oh1).astype(_bf16)
    before = carry_ref[...] + _dot(cnt, tri_ref[0])
    rank_ref[0:1, :] = jnp.sum(oh0 * before, axis=0, keepdims=True).astype(jnp.int32)
    rank_ref[1:2, :] = jnp.sum(oh1 * before, axis=0, keepdims=True).astype(jnp.int32)
    total = carry_ref[...] + _dot(cnt, tri_ref[1])
    carry_ref[...] = total
    cnt_ref[...] = total[:, :LANES].astype(jnp.int32)


def _retention_tables():
    c = RET_CHUNK
    log_gamma = np.log1p(-np.exp2(-5.0 - np.arange(RET_HEADS, dtype=np.float64)))
    pos = np.arange(c, dtype=np.float64)
    diff = pos[:, None] - pos[None, :]
    dmask = np.where((diff >= 0)[None], np.exp(log_gamma[:, None, None] * np.maximum(diff, 0.0)[None]), 0.0)
    q_decay = np.exp(log_gamma[:, None] * (pos[None, :] + 1.0))
    k_decay = np.exp(log_gamma[:, None] * (c - 1.0 - pos[None, :]))
    chunk_decay = np.exp(log_gamma * c)
    qd = np.repeat(q_decay.T, RET_DV, axis=1)
    kdt = np.repeat(k_decay.reshape(RET_HEADS // 2, 2, 1, c), RET_DK, axis=2).reshape(RET_HEADS // 2, 2 * RET_DK, c)
    sdec = np.repeat(chunk_decay.reshape(RET_HEADS // 2, 2, 1), RET_DK, axis=2).reshape(RET_HEADS // 2, 2 * RET_DK, 1)
    sdec = np.broadcast_to(sdec, (RET_HEADS // 2, 2 * RET_DK, 2 * RET_DV))
    rr = np.arange(2 * RET_DK)[:, None] // RET_DK
    cc = np.arange(2 * RET_DV)[None, :] // RET_DV
    bmask = (rr == cc).astype(np.float64)
    return tuple(jnp.asarray(np.ascontiguousarray(v), _f32) for v in (dmask, qd, kdt, sdec, bmask))


def _rope_tables(s_len):
    inv = ROPE_BASE ** (-jnp.arange(0, RET_DK, 2, dtype=_f32) / RET_DK)
    ang = jnp.arange(s_len, dtype=_f32)[:, None] * inv[None, :]
    cos, sin = jnp.cos(ang), jnp.sin(ang)
    cos_t = jnp.tile(cos, (1, LANES // (RET_DK // 2)))
    sin_t = jnp.tile(jnp.concatenate([-sin, sin], axis=1), (1, LANES // RET_DK))
    return cos_t, sin_t


def _mixer_operands(s, d, g_mix, w_in, conv_w, g_ret, w_out_conv, w_out_ret, w_o, g_moe, w_rg, b_rg, w_re, b_re):
    ts = MIX_TILE
    cos_t, sin_t = _rope_tables(s)
    dmask, qd, kdt, sdec, bmask = _retention_tables()
    wrt = jnp.zeros((ROUTER_ROWS, d), _f32)
    wrt = wrt.at[:N_GROUPS].set(w_rg.T).at[SUBLANES:SUBLANES + N_EXPERTS].set(w_re.T).astype(_bf16)
    rb = jnp.zeros((ROUTER_ROWS,), _f32).at[:N_GROUPS].set(b_rg).at[SUBLANES:SUBLANES + N_EXPERTS].set(b_re)
    rbias = jnp.broadcast_to(rb[:, None], (ROUTER_ROWS, ts))
    ii = np.arange(ts)
    tri = jnp.asarray(np.stack([(ii[:, None] < ii[None, :]), np.ones((ts, ts), bool)]), _bf16)
    return (g_mix.reshape(1, d), w_in.astype(_bf16), conv_w, cos_t, sin_t, dmask, qd, kdt, sdec, bmask,
            g_ret.reshape(1, d), w_out_conv.astype(_bf16), w_out_ret.astype(_bf16), w_o.astype(_bf16),
            g_moe.reshape(1, d), wrt, rbias, tri)


def _mixer(x, b0, b, operands):
    _, s, d = x.shape
    ts = MIX_TILE
    nt = s // ts
    tile3 = lambda w: pl.BlockSpec((None, ts, w), lambda bi, ti: (bi, ti, 0))
    route = lambda: pl.BlockSpec((None, 2, ts), lambda bi, ti: (bi, 0, ti))
    rope_spec = pl.BlockSpec((ts, LANES), lambda bi, ti: (ti, 0))
    in_specs = [pl.BlockSpec((None, ts, d), lambda bi, ti: (b0 + bi, ti, 0))]
    in_specs += [rope_spec if i in (3, 4) else _const_spec(op.shape) for i, op in enumerate(operands)]
    out_shape = [
        jax.ShapeDtypeStruct((b, s, d), _f32),
        jax.ShapeDtypeStruct((b, s, d // 2), jnp.uint32),
        jax.ShapeDtypeStruct((b, 2, s), jnp.int32),
        jax.ShapeDtypeStruct((b, 2, s), _f32),
        jax.ShapeDtypeStruct((b, 2, s), jnp.int32),
        jax.ShapeDtypeStruct((N_EXPERTS, LANES), jnp.int32),
    ]
    out_specs = [tile3(d), tile3(d // 2), route(), route(), route(),
                 pl.BlockSpec((N_EXPERTS, LANES), lambda bi, ti: (0, 0))]
    scratch = [
        pltpu.VMEM((ts, d), _bf16),
        pltpu.VMEM((ts, RET_HEADS * RET_DK), _f32),
        pltpu.VMEM((ts, RET_HEADS * RET_DK), _f32),
        pltpu.VMEM((ts, RET_HEADS * RET_DV), _bf16),
        pltpu.VMEM((ts, RET_HEADS * RET_DV), _f32),
        pltpu.VMEM((ts, RET_HEADS * RET_DV), _bf16),
        pltpu.VMEM((ts, d), _f32),
        pltpu.VMEM((SUBLANES, d), _f32),
        pltpu.VMEM((RET_HEADS // 2, 2 * RET_DK, 2 * RET_DV), _f32),
        pltpu.VMEM((N_EXPERTS, ts), _f32),
    ]
    return pl.pallas_call(
        _mixer_kernel,
        grid=(b, nt),
        in_specs=in_specs,
        out_specs=out_specs,
        out_shape=out_shape,
        scratch_shapes=scratch,
        compiler_params=pltpu.CompilerParams(
            dimension_semantics=("arbitrary", "arbitrary"), vmem_limit_bytes=VMEM_LIMIT),
        name="mixer_router",
    )(x, *operands)


def _sc_worker_id():
    return lax.axis_index("s") * SC_CORES + lax.axis_index("c")


def _sc_mesh():
    return plsc.VectorSubcoreMesh(core_axis_name="c", subcore_axis_name="s")


def _sc_dispatch(src, idx3, n_rows):
    t, d = src.shape
    n_win_total, _, win = idx3.shape
    n_win = n_win_total // SC_WORKERS

    @functools.partial(
        pl.kernel, mesh=_sc_mesh(),
        out_type=jax.ShapeDtypeStruct((n_rows, d), src.dtype),
        scratch_types=[pltpu.VMEM((2, win), jnp.int32), pltpu.VMEM((win, d), src.dtype)],
    )
    def dispatch(src_hbm, idx_hbm, out_hbm, idx_v, rows_v):
        wid = _sc_worker_id()

        @pl.loop(0, n_win)
        def _(i):
            w = wid * n_win + i
            off = pl.multiple_of(w * win, SUBLANES)
            pltpu.sync_copy(idx_hbm.at[w], idx_v)
            pltpu.sync_copy(src_hbm.at[pl.ds(off, win)], rows_v)
            pltpu.sync_copy(rows_v, out_hbm.at[idx_v.at[0]])
            pltpu.sync_copy(rows_v, out_hbm.at[idx_v.at[1]])

    return dispatch(src, idx3)


def _sc_gather(table, idx):
    n = idx.shape[0]
    d = table.shape[1]
    win = SC_WINDOW
    per_w = n // SC_WORKERS
    n_win = per_w // win

    @functools.partial(
        pl.kernel, mesh=_sc_mesh(),
        out_type=jax.ShapeDtypeStruct((n, d), table.dtype),
        scratch_types=[pltpu.VMEM((win,), jnp.int32), pltpu.VMEM((win, d), table.dtype),
                       pltpu.SemaphoreType.DMA],
    )
    def gather(table_hbm, idx_hbm, out_hbm, idx_v, rows_v, sem):
        base = _sc_worker_id() * per_w

        @pl.loop(0, n_win)
        def _(i):
            off = pl.multiple_of(base + i * win, SUBLANES)
            pltpu.sync_copy(idx_hbm.at[pl.ds(off, win)], idx_v)
            pltpu.async_copy(table_hbm.at[idx_v], rows_v, sem).wait()
            pltpu.sync_copy(rows_v, out_hbm.at[pl.ds(off, win)])

    return gather(table, idx)


def _expert_kernel(blk0_ref, nblk_ref, count_ref, xs_ref, wg_ref, wu_ref, wd_ref, y_ref,
                   xbuf, ybuf, wgb_ref, wub_ref, wdb_ref, sem_in, sem_out):
    e = pl.program_id(0)
    blk0 = blk0_ref[e]
    nblk = nblk_ref[e]
    count = count_ref[e]
    half = xbuf.shape[2]

    def rows_in(j, slot):
        return pltpu.make_async_copy(xs_ref.at[pl.ds((blk0 + j) * MOE_BLOCK, MOE_BLOCK)], xbuf.at[slot],
                                     sem_in.at[slot])

    def rows_out(j, slot):
        return pltpu.make_async_copy(ybuf.at[slot], y_ref.at[pl.ds((blk0 + j) * MOE_BLOCK, MOE_BLOCK)],
                                     sem_out.at[slot])

    @pl.when(nblk > 0)
    def _():
        rows_in(0, 0).start()

    wgb_ref[...] = wg_ref[...].astype(_bf16)
    wub_ref[...] = wu_ref[...].astype(_bf16)
    wdb_ref[...] = wd_ref[...].astype(_bf16)

    def block(j, carry):
        slot = j % 2
        rows_in(j, slot).wait()

        @pl.when(j + 1 < nblk)
        def _():
            rows_in(j + 1, 1 - slot).start()

        @pl.when(j >= 2)
        def _():
            rows_out(j - 2, slot).wait()

        rowid = lax.broadcasted_iota(jnp.int32, (MOE_BLOCK, half), 0)
        lo, hi = _unpack_halves(jnp.where(rowid < count - j * MOE_BLOCK, xbuf[slot], jnp.uint32(0)))
        lo = lo.astype(_bf16)
        hi = hi.astype(_bf16)
        gate = _dot(lo, wgb_ref[:half, :]) + _dot(hi, wgb_ref[half:, :])
        up = _dot(lo, wub_ref[:half, :]) + _dot(hi, wub_ref[half:, :])
        hid = (gate * _sigmoid(gate) * up).astype(_bf16)
        ybuf[slot] = _pack_halves(_dot(hid, wdb_ref[...]))
        rows_out(j, slot).start()
        return carry

    lax.fori_loop(0, nblk, block, 0)

    @pl.when(nblk >= 2)
    def _():
        rows_out(nblk - 2, nblk % 2).wait()

    @pl.when(nblk >= 1)
    def _():
        rows_out(nblk - 1, (nblk - 1) % 2).wait()


def _experts(xs, w_gate, w_up, w_down, blk0, nblk, counts):
    n_rows, half = xs.shape
    d = 2 * half
    ne, _, de = w_gate.shape
    grid_spec = pltpu.PrefetchScalarGridSpec(
        num_scalar_prefetch=3,
        grid=(ne,),
        in_specs=[
            pl.BlockSpec(memory_space=pl.ANY),
            pl.BlockSpec((None, d, de), lambda e, *_: (e, 0, 0)),
            pl.BlockSpec((None, d, de), lambda e, *_: (e, 0, 0)),
            pl.BlockSpec((None, de, d), lambda e, *_: (e, 0, 0)),
        ],
        out_specs=pl.BlockSpec(memory_space=pl.ANY),
        scratch_shapes=[
            pltpu.VMEM((2, MOE_BLOCK, half), jnp.uint32), pltpu.VMEM((2, MOE_BLOCK, half), jnp.uint32),
            pltpu.VMEM((d, de), _bf16), pltpu.VMEM((d, de), _bf16), pltpu.VMEM((de, d), _bf16),
            pltpu.SemaphoreType.DMA((2,)), pltpu.SemaphoreType.DMA((2,)),
        ],
    )
    return pl.pallas_call(
        _expert_kernel,
        grid_spec=grid_spec,
        out_shape=jax.ShapeDtypeStruct((n_rows, half), jnp.uint32),
        compiler_params=pltpu.CompilerParams(dimension_semantics=("arbitrary",), vmem_limit_bytes=VMEM_LIMIT),
        name="experts",
    )(blk0, nblk, counts, xs, w_gate, w_up, w_down)


def _combine_kernel(ew_ref, x1_ref, p_ref, y2_ref, gin_ref, wpg_ref, wpp_ref, gpost_ref, gfin_ref, *rest):
    out_ref = rest[-1]
    tc, d = x1_ref.shape
    x2 = x1_ref[...]
    for k in range(2):
        wcol = jnp.broadcast_to(ew_ref[k:k + 1, :], (LANES, tc)).T
        yk = jnp.concatenate(_unpack_halves(y2_ref[k]), axis=1)
        x2 = x2 + jnp.tile(wcol, (1, d // LANES)) * yk
    gate = _sigmoid(_dot(_rms(x2, gin_ref[...]).astype(_bf16), wpg_ref[...]))
    ple = _rms(_dot(p_ref[...].astype(_bf16), wpp_ref[...]), gpost_ref[...])
    out_ref[...] = _rms(x2 + gate * ple, gfin_ref[...])


def _combine(x1, p0, y2, ew, gains_and_weights, b0, b_total, prev_out):
    b, s, d = x1.shape
    tc = COMBINE_TILE
    pdim = p0.shape[-1]
    vec = lambda: pl.BlockSpec((1, d), lambda bi, ti: (0, 0))
    in_specs = [
        pl.BlockSpec((None, 2, tc), lambda bi, ti: (bi, 0, ti)),
        pl.BlockSpec((None, tc, d), lambda bi, ti: (bi, ti, 0)),
        pl.BlockSpec((None, tc, pdim), lambda bi, ti: (b0 + bi, ti, 0)),
        pl.BlockSpec((None, 2, tc, d // 2), lambda bi, ti: (bi, 0, ti, 0)),
        vec(),
        pl.BlockSpec((d, d), lambda bi, ti: (0, 0)),
        pl.BlockSpec((pdim, d), lambda bi, ti: (0, 0)),
        vec(), vec(),
    ]
    args = [ew, x1, p0, y2, *gains_and_weights]
    aliases = {}
    if prev_out is not None:
        in_specs.append(pl.BlockSpec(memory_space=pl.ANY))
        aliases = {len(args): 0}
        args.append(prev_out)
    return pl.pallas_call(
        _combine_kernel,
        grid=(b, s // tc),
        in_specs=in_specs,
        out_specs=pl.BlockSpec((None, tc, d), lambda bi, ti: (b0 + bi, ti, 0)),
        out_shape=jax.ShapeDtypeStruct((b_total, s, d), _f32),
        input_output_aliases=aliases,
        compiler_params=pltpu.CompilerParams(
            dimension_semantics=("arbitrary", "arbitrary"), vmem_limit_bytes=VMEM_LIMIT),
        name="combine_ple",
    )(*args)


def _layer(x, p_i, g_mix, w_in, conv_w, g_ret, w_out_conv, w_out_ret, w_o, g_moe, w_rg, b_rg, w_re, b_re,
           w_exp_gate, w_exp_up, w_exp_down, g_ple_in, w_ple_gate, w_ple_proj, g_ple_post, g_out):
    b_total, s, d = x.shape
    b = b_total // TOKEN_SPLITS
    n_tok = b * s
    operands = _mixer_operands(s, d, g_mix, w_in, conv_w, g_ret, w_out_conv, w_out_ret, w_o, g_moe,
                               w_rg, b_rg, w_re, b_re)
    tail = (g_ple_in.reshape(1, d), w_ple_gate.astype(_bf16), w_ple_proj.astype(_bf16),
            g_ple_post.reshape(1, d), g_out.reshape(1, d))
    nblk = (2 * n_tok + N_EXPERTS * (MOE_BLOCK - 1) + MOE_BLOCK - 1) // MOE_BLOCK
    n_rows = nblk * MOE_BLOCK
    win = SC_WINDOW

    def route(eid, rank, cnt):
        counts = cnt[:, 0]
        padded = (counts + MOE_BLOCK - 1) // MOE_BLOCK * MOE_BLOCK
        pends = jnp.cumsum(padded)
        pstarts = pends - padded
        eids = jnp.arange(N_EXPERTS, dtype=jnp.int32).reshape(N_EXPERTS, 1, 1, 1)
        seg = jnp.sum(jnp.where(eid[None] == eids, pstarts.reshape(N_EXPERTS, 1, 1, 1), 0), axis=0)
        dest = rank + seg
        idx3 = dest.reshape(b, 2, s // win, win).transpose(0, 2, 1, 3).reshape(n_tok // win, 2, win)
        return dest, idx3, pstarts // MOE_BLOCK, padded // MOE_BLOCK, counts

    st = [dict() for _ in range(TOKEN_SPLITS)]
    out = None

    def mix(h):
        x1, h2p, eid, ew, rank, cnt = _mixer(x, h * b, b, operands)
        dest, idx3, blk0, nblk_e, counts = route(eid, rank, cnt)
        st[h].update(x1=x1, h2p=h2p, ew=ew, dest=dest, idx3=idx3, seg=(blk0, nblk_e, counts))

    def dispatch(h):
        st[h]["xs"] = _sc_dispatch(st[h]["h2p"].reshape(n_tok, d // 2), st[h]["idx3"], n_rows)

    def experts(h):
        st[h]["y"] = _experts(st[h]["xs"], w_exp_gate, w_exp_up, w_exp_down, *st[h]["seg"])

    def gather(h):
        st[h]["y2"] = _sc_gather(st[h]["y"], st[h]["dest"].reshape(-1)).reshape(b, 2, s, d // 2)

    def combine(h):
        nonlocal out
        out = _combine(st[h]["x1"], p_i, st[h]["y2"], st[h]["ew"], tail, h * b, b_total, out)

    stages = (mix, dispatch, experts, gather, combine)
    for step in range(TOKEN_SPLITS + len(stages) - 1):
        for k in reversed(range(len(stages))):
            if 0 <= step - k < TOKEN_SPLITS:
                stages[k](step - k)
    return out


def kernel(x, p, g_mix, w_in, conv_w, g_ret, w_out_conv, w_out_ret, w_o, g_moe, w_rg, b_rg, w_re, b_re, w_exp_gate, w_exp_up, w_exp_down, g_ple_in, w_ple_gate, w_ple_proj, g_ple_post, g_final):
    depth = p.shape[0]
    assert depth == 1, "the final norm is fused into the single layer's combine kernel"
    return _layer(x, p[0], g_mix[0], w_in[0], conv_w[0], g_ret[0], w_out_conv[0], w_out_ret[0], w_o[0],
                  g_moe[0], w_rg[0], b_rg[0], w_re[0], b_re[0], w_exp_gate[0], w_exp_up[0], w_exp_down[0],
                  g_ple_in[0], w_ple_gate[0], w_ple_proj[0], g_ple_post[0], g_final)
```

```python
import functools

import jax
import jax.numpy as jnp
import numpy as np
from jax import lax
from jax.experimental import pallas as pl
from jax.experimental.pallas import tpu as pltpu
from jax.experimental.pallas import tpu_sc as plsc

EPS = 1e-6
CONV_K = 3
RET_HEADS = 8
RET_DK = 64
RET_DV = 128
RET_CHUNK = 128
ROPE_BASE = 10000.0
N_GROUPS = 4
EXPERTS_PER_GROUP = 8
N_EXPERTS = N_GROUPS * EXPERTS_PER_GROUP
MOE_BLOCK = 256
LANES = 128
SUBLANES = 8
ROUTER_ROWS = 48
MIX_TILE = 256
COMBINE_TILE = 256
EXPERT_IN_SLOTS = 4
TOKEN_SPLITS = 2
SC_CORES = 2
SC_SUBCORES = 16
SC_WORKERS = SC_CORES * SC_SUBCORES
SC_WINDOW = 128
BF16_BITS = 16
HIGH_HALF = np.uint32(0xFFFF0000)
VMEM_LIMIT = 56 * 1024 * 1024

_bf16 = jnp.bfloat16
_f32 = jnp.float32


def _sigmoid(v):
    return 1.0 / (1.0 + jnp.exp(-v))


def _rms(v, g):
    ms = jnp.mean(v * v, axis=-1, keepdims=True)
    return v * lax.rsqrt(ms + EPS) * g


def _dot(a, b):
    return jnp.dot(a, b, preferred_element_type=_f32)


def _pack_halves(v):
    bits = lax.bitcast_convert_type(v.astype(_bf16).astype(_f32), jnp.uint32)
    c = v.shape[1] // 2
    return (bits[:, :c] >> BF16_BITS) | (bits[:, c:] & HIGH_HALF)


def _unpack_halves(w):
    lo = lax.bitcast_convert_type(w << BF16_BITS, _f32)
    hi = lax.bitcast_convert_type(w & HIGH_HALF, _f32)
    return lo, hi


def _const_spec(shape):
    nd = len(shape)
    return pl.BlockSpec(shape, lambda *_: (0,) * nd, pipeline_mode=pl.Buffered(1))


def _mixer_kernel(x_ref, gmix_ref, win_ref, convw_ref, cos_ref, sin_ref, dmask_ref, qd_ref, kdt_ref,
                  sdec_ref, bmask_ref, gret_ref, woc_ref, wor_ref, wo_ref, gmoe_ref, wrt_ref, rbias_ref,
                  tri_ref,
                  x1_ref, h2p_ref, eid_ref, ew_ref, rank_ref, cnt_ref,
                  hb_ref, qr_ref, kr_ref, vb_ref, o_ref, og_ref, acc_ref, cuc_ref, state_ref, carry_ref):
    ts, d = x_ref.shape
    t = pl.program_id(1)

    @pl.when(t == 0)
    def _():
        cuc_ref[...] = jnp.zeros_like(cuc_ref)
        state_ref[...] = jnp.zeros_like(state_ref)

    @pl.when(jnp.logical_and(t == 0, pl.program_id(0) == 0))
    def _():
        carry_ref[...] = jnp.zeros_like(carry_ref)

    x = x_ref[...]
    hb_ref[...] = _rms(x, gmix_ref[...]).astype(_bf16)
    hb = hb_ref[...]

    def proj(lo, hi):
        return _dot(hb, win_ref[:, lo:hi])

    cu = proj(d, 2 * d) * proj(0, d)
    prev = cuc_ref[...]
    p1 = prev[SUBLANES - 1:SUBLANES, :]
    p2 = prev[SUBLANES - 2:SUBLANES - 1, :]
    rows = lax.broadcasted_iota(jnp.int32, (ts, d), 0)
    s1 = jnp.where(rows == 0, p1, pltpu.roll(cu, 1, 0))
    s2 = jnp.where(rows == 0, p2, jnp.where(rows == 1, p1, pltpu.roll(cu, 2, 0)))
    conv = convw_ref[0:1, :] * s2 + convw_ref[1:2, :] * s1 + convw_ref[2:3, :] * cu
    cuc_ref[...] = cu[ts - SUBLANES:ts, :]
    a = (proj(2 * d, 3 * d) * conv).astype(_bf16)
    acc_ref[...] = _sigmoid(proj(6 * d, 7 * d)) * _dot(a, woc_ref[...])

    qk0 = 3 * d
    cosv = cos_ref[...]
    sinv = sin_ref[...]
    lane = lax.broadcasted_iota(jnp.int32, (ts, LANES), 1)
    first_half = (lane % RET_DK) < (RET_DK // 2)
    for g in range(4):
        for dst, base, scale in ((qr_ref, qk0, None), (kr_ref, qk0 + 4 * LANES, RET_DK ** -0.5)):
            z = proj(base + g * LANES, base + (g + 1) * LANES)
            zs = jnp.where(first_half, pltpu.roll(z, LANES - RET_DK // 2, 1), pltpu.roll(z, RET_DK // 2, 1))
            r = z * cosv + zs * sinv
            if scale is not None:
                r = r * scale
            dst[:, g * LANES:(g + 1) * LANES] = r
    vb_ref[...] = proj(4 * d, 5 * d).astype(_bf16)

    c = RET_CHUNK
    lane_c = lax.broadcasted_iota(jnp.int32, (c, LANES), 1)
    even = lane_c < RET_DK
    for ci in range(ts // c):
        r0 = ci * c
        for j in range(RET_HEADS // 2):
            q2 = qr_ref[r0:r0 + c, j * LANES:(j + 1) * LANES]
            k2 = kr_ref[r0:r0 + c, j * LANES:(j + 1) * LANES]
            v2 = vb_ref[r0:r0 + c, 2 * j * RET_DV:(2 * j + 2) * RET_DV]
            kt = k2.T
            qq = jnp.concatenate([jnp.where(even, q2, 0.0), jnp.where(even, 0.0, q2)], axis=0).astype(_bf16)
            sc = _dot(qq, kt.astype(_bf16))
            pe = (sc[:c] * dmask_ref[2 * j]).astype(_bf16)
            po = (sc[c:] * dmask_ref[2 * j + 1]).astype(_bf16)
            inner = jnp.concatenate([_dot(pe, v2[:, :RET_DV]), _dot(po, v2[:, RET_DV:])], axis=1)
            st = state_ref[j]
            cross = _dot(q2.astype(_bf16), st.astype(_bf16)) * qd_ref[:, 2 * j * RET_DV:(2 * j + 2) * RET_DV]
            o_ref[r0:r0 + c, 2 * j * RET_DV:(2 * j + 2) * RET_DV] = inner + cross
            upd = _dot((kt * kdt_ref[j]).astype(_bf16), v2)
            state_ref[j] = st * sdec_ref[j] + upd * bmask_ref[...]

    for h in range(RET_HEADS):
        sl = slice(h * RET_DV, (h + 1) * RET_DV)
        sg = proj(5 * d + h * RET_DV, 5 * d + (h + 1) * RET_DV)
        on = _rms(o_ref[:, sl], gret_ref[:, sl])
        og_ref[:, sl] = (sg * _sigmoid(sg) * on).astype(_bf16)
    yr = _dot(og_ref[...], wor_ref[...])
    mixed = acc_ref[...] + _sigmoid(proj(7 * d, 8 * d)) * yr
    x1 = x + _dot(mixed.astype(_bf16), wo_ref[...])
    x1_ref[...] = x1

    h2 = _rms(x1, gmoe_ref[...]).astype(_bf16)
    h2p_ref[...] = _pack_halves(h2)
    lt = lax.dot_general(wrt_ref[...], h2, (((1,), (1,)), ((), ())), preferred_element_type=_f32)
    lt = lt + rbias_ref[...]
    g0, g1, g2, g3 = (lt[i:i + 1, :] for i in range(N_GROUPS))
    gmax = jnp.maximum(jnp.maximum(g0, g1), jnp.maximum(g2, g3))
    grp = jnp.where(g0 == gmax, 0, jnp.where(g1 == gmax, 1, jnp.where(g2 == gmax, 2, 3)))
    gsum = jnp.exp(g0 - gmax) + jnp.exp(g1 - gmax) + jnp.exp(g2 - gmax) + jnp.exp(g3 - gmax)
    g_w = 1.0 / gsum
    e_in = lt[SUBLANES:2 * SUBLANES, :]
    for g in range(1, N_GROUPS):
        e_in = jnp.where(grp == g, lt[(g + 1) * SUBLANES:(g + 2) * SUBLANES, :], e_in)
    ridx = lax.broadcasted_iota(jnp.int32, (EXPERTS_PER_GROUP, ts), 0)
    top1 = jnp.max(e_in, axis=0, keepdims=True)
    i1 = jnp.min(jnp.where(e_in == top1, ridx, EXPERTS_PER_GROUP), axis=0, keepdims=True)
    e_m = jnp.where(ridx == i1, -jnp.inf, e_in)
    top2 = jnp.max(e_m, axis=0, keepdims=True)
    i2 = jnp.min(jnp.where(e_m == top2, ridx, EXPERTS_PER_GROUP), axis=0, keepdims=True)
    ex = jnp.exp(top2 - top1)
    den = 1.0 + ex
    id0 = grp * EXPERTS_PER_GROUP + i1
    id1 = grp * EXPERTS_PER_GROUP + i2
    eid_ref[0:1, :] = id0
    eid_ref[1:2, :] = id1
    ew_ref[0:1, :] = (1.0 / den) * g_w
    ew_ref[1:2, :] = (ex / den) * g_w

    eidx = lax.broadcasted_iota(jnp.int32, (N_EXPERTS, ts), 0)
    oh0 = (eidx == id0).astype(_f32)
    oh1 = (eidx == id1).astype(_f32)
    cnt = (oh0 + oh1).astype(_bf16)
    before = carry_ref[...] + _dot(cnt, tri_ref[0])
    rank_ref[0:1, :] = jnp.sum(oh0 * before, axis=0, keepdims=True).astype(jnp.int32)
    rank_ref[1:2, :] = jnp.sum(oh1 * before, axis=0, keepdims=True).astype(jnp.int32)
    total = carry_ref[...] + _dot(cnt, tri_ref[1])
    carry_ref[...] = total
    cnt_ref[...] = total[:, :LANES].astype(jnp.int32)


def _retention_tables():
    c = RET_CHUNK
    log_gamma = np.log1p(-np.exp2(-5.0 - np.arange(RET_HEADS, dtype=np.float64)))
    pos = np.arange(c, dtype=np.float64)
    diff = pos[:, None] - pos[None, :]
    dmask = np.where((diff >= 0)[None], np.exp(log_gamma[:, None, None] * np.maximum(diff, 0.0)[None]), 0.0)
    q_decay = np.exp(log_gamma[:, None] * (pos[None, :] + 1.0))
    k_decay = np.exp(log_gamma[:, None] * (c - 1.0 - pos[None, :]))
    chunk_decay = np.exp(log_gamma * c)
    qd = np.repeat(q_decay.T, RET_DV, axis=1)
    kdt = np.repeat(k_decay.reshape(RET_HEADS // 2, 2, 1, c), RET_DK, axis=2).reshape(RET_HEADS // 2, 2 * RET_DK, c)
    sdec = np.repeat(chunk_decay.reshape(RET_HEADS // 2, 2, 1), RET_DK, axis=2).reshape(RET_HEADS // 2, 2 * RET_DK, 1)
    sdec = np.broadcast_to(sdec, (RET_HEADS // 2, 2 * RET_DK, 2 * RET_DV))
    rr = np.arange(2 * RET_DK)[:, None] // RET_DK
    cc = np.arange(2 * RET_DV)[None, :] // RET_DV
    bmask = (rr == cc).astype(np.float64)
    return tuple(jnp.asarray(np.ascontiguousarray(v), _f32) for v in (dmask, qd, kdt, sdec, bmask))


def _rope_tables(s_len):
    inv = ROPE_BASE ** (-jnp.arange(0, RET_DK, 2, dtype=_f32) / RET_DK)
    ang = jnp.arange(s_len, dtype=_f32)[:, None] * inv[None, :]
    cos, sin = jnp.cos(ang), jnp.sin(ang)
    cos_t = jnp.tile(cos, (1, LANES // (RET_DK // 2)))
    sin_t = jnp.tile(jnp.concatenate([-sin, sin], axis=1), (1, LANES // RET_DK))
    return cos_t, sin_t


def _mixer_operands(s, d, g_mix, w_in, conv_w, g_ret, w_out_conv, w_out_ret, w_o, g_moe, w_rg, b_rg, w_re, b_re):
    ts = MIX_TILE
    cos_t, sin_t = _rope_tables(s)
    dmask, qd, kdt, sdec, bmask = _retention_tables()
    wrt = jnp.zeros((ROUTER_ROWS, d), _f32)
    wrt = wrt.at[:N_GROUPS].set(w_rg.T).at[SUBLANES:SUBLANES + N_EXPERTS].set(w_re.T).astype(_bf16)
    rb = jnp.zeros((ROUTER_ROWS,), _f32).at[:N_GROUPS].set(b_rg).at[SUBLANES:SUBLANES + N_EXPERTS].set(b_re)
    rbias = jnp.broadcast_to(rb[:, None], (ROUTER_ROWS, ts))
    ii = np.arange(ts)
    tri = jnp.asarray(np.stack([(ii[:, None] < ii[None, :]), np.ones((ts, ts), bool)]), _bf16)
    return (g_mix.reshape(1, d), w_in.astype(_bf16), conv_w, cos_t, sin_t, dmask, qd, kdt, sdec, bmask,
            g_ret.reshape(1, d), w_out_conv.astype(_bf16), w_out_ret.astype(_bf16), w_o.astype(_bf16),
            g_moe.reshape(1, d), wrt, rbias, tri)


def _mixer(x, b0, b, operands):
    _, s, d = x.shape
    ts = MIX_TILE
    nt = s // ts
    tile3 = lambda w: pl.BlockSpec((None, ts, w), lambda bi, ti: (bi, ti, 0))
    route = lambda: pl.BlockSpec((None, 2, ts), lambda bi, ti: (bi, 0, ti))
    rope_spec = pl.BlockSpec((ts, LANES), lambda bi, ti: (ti, 0))
    in_specs = [pl.BlockSpec((None, ts, d), lambda bi, ti: (b0 + bi, ti, 0))]
    in_specs += [rope_spec if i in (3, 4) else _const_spec(op.shape) for i, op in enumerate(operands)]
    out_shape = [
        jax.ShapeDtypeStruct((b, s, d), _f32),
        jax.ShapeDtypeStruct((b, s, d // 2), jnp.uint32),
        jax.ShapeDtypeStruct((b, 2, s), jnp.int32),
        jax.ShapeDtypeStruct((b, 2, s), _f32),
        jax.ShapeDtypeStruct((b, 2, s), jnp.int32),
        jax.ShapeDtypeStruct((N_EXPERTS, LANES), jnp.int32),
    ]
    out_specs = [tile3(d), tile3(d // 2), route(), route(), route(),
                 pl.BlockSpec((N_EXPERTS, LANES), lambda bi, ti: (0, 0))]
    scratch = [
        pltpu.VMEM((ts, d), _bf16),
        pltpu.VMEM((ts, RET_HEADS * RET_DK), _f32),
        pltpu.VMEM((ts, RET_HEADS * RET_DK), _f32),
        pltpu.VMEM((ts, RET_HEADS * RET_DV), _bf16),
        pltpu.VMEM((ts, RET_HEADS * RET_DV), _f32),
        pltpu.VMEM((ts, RET_HEADS * RET_DV), _bf16),
        pltpu.VMEM((ts, d), _f32),
        pltpu.VMEM((SUBLANES, d), _f32),
        pltpu.VMEM((RET_HEADS // 2, 2 * RET_DK, 2 * RET_DV), _f32),
        pltpu.VMEM((N_EXPERTS, ts), _f32),
    ]
    return pl.pallas_call(
        _mixer_kernel,
        grid=(b, nt),
        in_specs=in_specs,
        out_specs=out_specs,
        out_shape=out_shape,
        scratch_shapes=scratch,
        compiler_params=pltpu.CompilerParams(
            dimension_semantics=("arbitrary", "arbitrary"), vmem_limit_bytes=VMEM_LIMIT),
        name="mixer_router",
    )(x, *operands)


def _sc_worker_id():
    return lax.axis_index("s") * SC_CORES + lax.axis_index("c")


def _sc_mesh():
    return plsc.VectorSubcoreMesh(core_axis_name="c", subcore_axis_name="s")


def _sc_dispatch(src, idx3, n_rows):
    t, d = src.shape
    n_win_total, _, win = idx3.shape
    n_win = n_win_total // SC_WORKERS

    @functools.partial(
        pl.kernel, mesh=_sc_mesh(),
        out_type=jax.ShapeDtypeStruct((n_rows, d), src.dtype),
        scratch_types=[pltpu.VMEM((2, win), jnp.int32), pltpu.VMEM((win, d), src.dtype)],
    )
    def dispatch(src_hbm, idx_hbm, out_hbm, idx_v, rows_v):
        wid = _sc_worker_id()

        @pl.loop(0, n_win)
        def _(i):
            w = wid * n_win + i
            off = pl.multiple_of(w * win, SUBLANES)
            pltpu.sync_copy(idx_hbm.at[w], idx_v)
            pltpu.sync_copy(src_hbm.at[pl.ds(off, win)], rows_v)
            pltpu.sync_copy(rows_v, out_hbm.at[idx_v.at[0]])
            pltpu.sync_copy(rows_v, out_hbm.at[idx_v.at[1]])

    return dispatch(src, idx3)


def _sc_gather(table, idx):
    n = idx.shape[0]
    d = table.shape[1]
    win = SC_WINDOW
    per_w = n // SC_WORKERS
    n_win = per_w // win

    @functools.partial(
        pl.kernel, mesh=_sc_mesh(),
        out_type=jax.ShapeDtypeStruct((n, d), table.dtype),
        scratch_types=[pltpu.VMEM((win,), jnp.int32), pltpu.VMEM((win, d), table.dtype),
                       pltpu.SemaphoreType.DMA],
    )
    def gather(table_hbm, idx_hbm, out_hbm, idx_v, rows_v, sem):
        base = _sc_worker_id() * per_w

        @pl.loop(0, n_win)
        def _(i):
            off = pl.multiple_of(base + i * win, SUBLANES)
            pltpu.sync_copy(idx_hbm.at[pl.ds(off, win)], idx_v)
            pltpu.async_copy(table_hbm.at[idx_v], rows_v, sem).wait()
            pltpu.sync_copy(rows_v, out_hbm.at[pl.ds(off, win)])

    return gather(table, idx)


def _expert_kernel(blk0_ref, nblk_ref, count_ref, xs_ref, wg_ref, wu_ref, wd_ref, y_ref,
                   xbuf, ybuf, wgb_ref, wub_ref, wdb_ref, sem_in, sem_out):
    e = pl.program_id(0)
    blk0 = blk0_ref[e]
    nblk = nblk_ref[e]
    count = count_ref[e]
    half = xbuf.shape[2]

    def rows_in(j, slot):
        return pltpu.make_async_copy(xs_ref.at[pl.ds((blk0 + j) * MOE_BLOCK, MOE_BLOCK)], xbuf.at[slot],
                                     sem_in.at[slot])

    def rows_out(j, slot):
        return pltpu.make_async_copy(ybuf.at[slot], y_ref.at[pl.ds((blk0 + j) * MOE_BLOCK, MOE_BLOCK)],
                                     sem_out.at[slot])

    n_in = xbuf.shape[0]
    for j0 in range(n_in - 1):
        @pl.when(j0 < nblk)
        def _():
            rows_in(j0, j0).start()

    wgb_ref[...] = wg_ref[...].astype(_bf16)
    wub_ref[...] = wu_ref[...].astype(_bf16)
    wdb_ref[...] = wd_ref[...].astype(_bf16)

    def block(j, carry):
        slot = j % 2
        islot = j % n_in
        rows_in(j, islot).wait()

        @pl.when(j + n_in - 1 < nblk)
        def _():
            rows_in(j + n_in - 1, (j + n_in - 1) % n_in).start()

        @pl.when(j >= 2)
        def _():
            rows_out(j - 2, slot).wait()

        rowid = lax.broadcasted_iota(jnp.int32, (MOE_BLOCK, half), 0)
        lo, hi = _unpack_halves(jnp.where(rowid < count - j * MOE_BLOCK, xbuf[islot], jnp.uint32(0)))
        lo = lo.astype(_bf16)
        hi = hi.astype(_bf16)
        gate = _dot(lo, wgb_ref[:half, :]) + _dot(hi, wgb_ref[half:, :])
        up = _dot(lo, wub_ref[:half, :]) + _dot(hi, wub_ref[half:, :])
        hid = (gate * _sigmoid(gate) * up).astype(_bf16)
        ybuf[slot] = _pack_halves(_dot(hid, wdb_ref[...]))
        rows_out(j, slot).start()
        return carry

    lax.fori_loop(0, nblk, block, 0)

    @pl.when(nblk >= 2)
    def _():
        rows_out(nblk - 2, nblk % 2).wait()

    @pl.when(nblk >= 1)
    def _():
        rows_out(nblk - 1, (nblk - 1) % 2).wait()


def _experts(xs, w_gate, w_up, w_down, blk0, nblk, counts):
    n_rows, half = xs.shape
    d = 2 * half
    ne, _, de = w_gate.shape
    grid_spec = pltpu.PrefetchScalarGridSpec(
        num_scalar_prefetch=3,
        grid=(ne,),
        in_specs=[
            pl.BlockSpec(memory_space=pl.ANY),
            pl.BlockSpec((None, d, de), lambda e, *_: (e, 0, 0)),
            pl.BlockSpec((None, d, de), lambda e, *_: (e, 0, 0)),
            pl.BlockSpec((None, de, d), lambda e, *_: (e, 0, 0)),
        ],
        out_specs=pl.BlockSpec(memory_space=pl.ANY),
        scratch_shapes=[
            pltpu.VMEM((EXPERT_IN_SLOTS, MOE_BLOCK, half), jnp.uint32), pltpu.VMEM((2, MOE_BLOCK, half), jnp.uint32),
            pltpu.VMEM((d, de), _bf16), pltpu.VMEM((d, de), _bf16), pltpu.VMEM((de, d), _bf16),
            pltpu.SemaphoreType.DMA((EXPERT_IN_SLOTS,)), pltpu.SemaphoreType.DMA((2,)),
        ],
    )
    return pl.pallas_call(
        _expert_kernel,
        grid_spec=grid_spec,
        out_shape=jax.ShapeDtypeStruct((n_rows, half), jnp.uint32),
        compiler_params=pltpu.CompilerParams(dimension_semantics=("arbitrary",), vmem_limit_bytes=VMEM_LIMIT),
        name="experts",
    )(blk0, nblk, counts, xs, w_gate, w_up, w_down)


def _combine_kernel(ew_ref, x1_ref, p_ref, y2_ref, gin_ref, wpg_ref, wpp_ref, gpost_ref, gfin_ref, *rest):
    out_ref = rest[-1]
    tc, d = x1_ref.shape
    x2 = x1_ref[...]
    for k in range(2):
        wcol = jnp.broadcast_to(ew_ref[k:k + 1, :], (LANES, tc)).T
        yk = jnp.concatenate(_unpack_halves(y2_ref[k]), axis=1)
        x2 = x2 + jnp.tile(wcol, (1, d // LANES)) * yk
    gate = _sigmoid(_dot(_rms(x2, gin_ref[...]).astype(_bf16), wpg_ref[...]))
    ple = _rms(_dot(p_ref[...].astype(_bf16), wpp_ref[...]), gpost_ref[...])
    out_ref[...] = _rms(x2 + gate * ple, gfin_ref[...])


def _combine(x1, p0, y2, ew, gains_and_weights, b0, b_total, prev_out):
    b, s, d = x1.shape
    tc = COMBINE_TILE
    pdim = p0.shape[-1]
    vec = lambda: pl.BlockSpec((1, d), lambda bi, ti: (0, 0))
    in_specs = [
        pl.BlockSpec((None, 2, tc), lambda bi, ti: (bi, 0, ti)),
        pl.BlockSpec((None, tc, d), lambda bi, ti: (bi, ti, 0)),
        pl.BlockSpec((None, tc, pdim), lambda bi, ti: (b0 + bi, ti, 0)),
        pl.BlockSpec((None, 2, tc, d // 2), lambda bi, ti: (bi, 0, ti, 0)),
        vec(),
        pl.BlockSpec((d, d), lambda bi, ti: (0, 0)),
        pl.BlockSpec((pdim, d), lambda bi, ti: (0, 0)),
        vec(), vec(),
    ]
    args = [ew, x1, p0, y2, *gains_and_weights]
    aliases = {}
    if prev_out is not None:
        in_specs.append(pl.BlockSpec(memory_space=pl.ANY))
        aliases = {len(args): 0}
        args.append(prev_out)
    return pl.pallas_call(
        _combine_kernel,
        grid=(b, s // tc),
        in_specs=in_specs,
        out_specs=pl.BlockSpec((None, tc, d), lambda bi, ti: (b0 + bi, ti, 0)),
        out_shape=jax.ShapeDtypeStruct((b_total, s, d), _f32),
        input_output_aliases=aliases,
        compiler_params=pltpu.CompilerParams(
            dimension_semantics=("arbitrary", "arbitrary"), vmem_limit_bytes=VMEM_LIMIT),
        name="combine_ple",
    )(*args)


def _layer(x, p_i, g_mix, w_in, conv_w, g_ret, w_out_conv, w_out_ret, w_o, g_moe, w_rg, b_rg, w_re, b_re,
           w_exp_gate, w_exp_up, w_exp_down, g_ple_in, w_ple_gate, w_ple_proj, g_ple_post, g_out):
    b_total, s, d = x.shape
    b = b_total // TOKEN_SPLITS
    n_tok = b * s
    operands = _mixer_operands(s, d, g_mix, w_in, conv_w, g_ret, w_out_conv, w_out_ret, w_o, g_moe,
                               w_rg, b_rg, w_re, b_re)
    tail = (g_ple_in.reshape(1, d), w_ple_gate.astype(_bf16), w_ple_proj.astype(_bf16),
            g_ple_post.reshape(1, d), g_out.reshape(1, d))
    nblk = (2 * n_tok + N_EXPERTS * (MOE_BLOCK - 1) + MOE_BLOCK - 1) // MOE_BLOCK
    n_rows = nblk * MOE_BLOCK
    win = SC_WINDOW

    def route(eid, rank, cnt):
        counts = cnt[:, 0]
        padded = (counts + MOE_BLOCK - 1) // MOE_BLOCK * MOE_BLOCK
        pends = jnp.cumsum(padded)
        pstarts = pends - padded
        eids = jnp.arange(N_EXPERTS, dtype=jnp.int32).reshape(N_EXPERTS, 1, 1, 1)
        seg = jnp.sum(jnp.where(eid[None] == eids, pstarts.reshape(N_EXPERTS, 1, 1, 1), 0), axis=0)
        dest = rank + seg
        idx3 = dest.reshape(b, 2, s // win, win).transpose(0, 2, 1, 3).reshape(n_tok // win, 2, win)
        return dest, idx3, pstarts // MOE_BLOCK, padded // MOE_BLOCK, counts

    st = [dict() for _ in range(TOKEN_SPLITS)]
    out = None

    def mix(h):
        x1, h2p, eid, ew, rank, cnt = _mixer(x, h * b, b, operands)
        dest, idx3, blk0, nblk_e, counts = route(eid, rank, cnt)
        st[h].update(x1=x1, h2p=h2p, ew=ew, dest=dest, idx3=idx3, seg=(blk0, nblk_e, counts))

    def dispatch(h):
        st[h]["xs"] = _sc_dispatch(st[h]["h2p"].reshape(n_tok, d // 2), st[h]["idx3"], n_rows)

    def experts(h):
        st[h]["y"] = _experts(st[h]["xs"], w_exp_gate, w_exp_up, w_exp_down, *st[h]["seg"])

    def gather(h):
        st[h]["y2"] = _sc_gather(st[h]["y"], st[h]["dest"].reshape(-1)).reshape(b, 2, s, d // 2)

    def combine(h):
        nonlocal out
        out = _combine(st[h]["x1"], p_i, st[h]["y2"], st[h]["ew"], tail, h * b, b_total, out)

    stages = (mix, dispatch, experts, gather, combine)
    for step in range(TOKEN_SPLITS + len(stages) - 1):
        for k in reversed(range(len(stages))):
            if 0 <= step - k < TOKEN_SPLITS:
                stages[k](step - k)
    return out


def kernel(x, p, g_mix, w_in, conv_w, g_ret, w_out_conv, w_out_ret, w_o, g_moe, w_rg, b_rg, w_re, b_re, w_exp_gate, w_exp_up, w_exp_down, g_ple_in, w_ple_gate, w_ple_proj, g_ple_post, g_final):
    depth = p.shape[0]
    assert depth == 1, "the final norm is fused into the single layer's combine kernel"
    return _layer(x, p[0], g_mix[0], w_in[0], conv_w[0], g_ret[0], w_out_conv[0], w_out_ret[0], w_o[0],
                  g_moe[0], w_rg[0], b_rg[0], w_re[0], b_re[0], w_exp_gate[0], w_exp_up[0], w_exp_down[0],
                  g_ple_in[0], w_ple_gate[0], w_ple_proj[0], g_ple_post[0], g_final)
```

```python
import functools

import jax
import jax.numpy as jnp
import numpy as np
from jax import lax
from jax.experimental import pallas as pl
from jax.experimental.pallas import tpu as pltpu
from jax.experimental.pallas import tpu_sc as plsc

EPS = 1e-6
CONV_K = 3
RET_HEADS = 8
RET_DK = 64
RET_DV = 128
RET_CHUNK = 128
ROPE_BASE = 10000.0
N_GROUPS = 4
EXPERTS_PER_GROUP = 8
N_EXPERTS = N_GROUPS * EXPERTS_PER_GROUP
MOE_BLOCK = 512
LANES = 128
SUBLANES = 8
ROUTER_ROWS = 48
MIX_TILE = 256
COMBINE_TILE = 256
EXPERT_IN_SLOTS = 4
TOKEN_SPLITS = 2
SC_CORES = 2
SC_SUBCORES = 16
SC_WORKERS = SC_CORES * SC_SUBCORES
SC_WINDOW = 128
BF16_BITS = 16
HIGH_HALF = np.uint32(0xFFFF0000)
VMEM_LIMIT = 56 * 1024 * 1024

_bf16 = jnp.bfloat16
_f32 = jnp.float32


def _sigmoid(v):
    return 1.0 / (1.0 + jnp.exp(-v))


def _rms(v, g):
    ms = jnp.mean(v * v, axis=-1, keepdims=True)
    return v * lax.rsqrt(ms + EPS) * g


def _dot(a, b):
    return jnp.dot(a, b, preferred_element_type=_f32)


def _pack_halves(v):
    bits = lax.bitcast_convert_type(v.astype(_bf16).astype(_f32), jnp.uint32)
    c = v.shape[1] // 2
    return (bits[:, :c] >> BF16_BITS) | (bits[:, c:] & HIGH_HALF)


def _unpack_halves(w):
    lo = lax.bitcast_convert_type(w << BF16_BITS, _f32)
    hi = lax.bitcast_convert_type(w & HIGH_HALF, _f32)
    return lo, hi


def _const_spec(shape):
    nd = len(shape)
    return pl.BlockSpec(shape, lambda *_: (0,) * nd, pipeline_mode=pl.Buffered(1))


def _mixer_kernel(x_ref, gmix_ref, win_ref, convw_ref, cos_ref, sin_ref, dmask_ref, qd_ref, kdt_ref,
                  sdec_ref, bmask_ref, gret_ref, woc_ref, wor_ref, wo_ref, gmoe_ref, wrt_ref, rbias_ref,
                  tri_ref,
                  x1_ref, h2p_ref, eid_ref, ew_ref, rank_ref, cnt_ref,
                  hb_ref, qr_ref, kr_ref, vb_ref, o_ref, og_ref, acc_ref, cuc_ref, state_ref, carry_ref):
    ts, d = x_ref.shape
    t = pl.program_id(1)

    @pl.when(t == 0)
    def _():
        cuc_ref[...] = jnp.zeros_like(cuc_ref)
        state_ref[...] = jnp.zeros_like(state_ref)

    @pl.when(jnp.logical_and(t == 0, pl.program_id(0) == 0))
    def _():
        carry_ref[...] = jnp.zeros_like(carry_ref)

    x = x_ref[...]
    hb_ref[...] = _rms(x, gmix_ref[...]).astype(_bf16)
    hb = hb_ref[...]

    def proj(lo, hi):
        return _dot(hb, win_ref[:, lo:hi])

    cu = proj(d, 2 * d) * proj(0, d)
    prev = cuc_ref[...]
    p1 = prev[SUBLANES - 1:SUBLANES, :]
    p2 = prev[SUBLANES - 2:SUBLANES - 1, :]
    rows = lax.broadcasted_iota(jnp.int32, (ts, d), 0)
    s1 = jnp.where(rows == 0, p1, pltpu.roll(cu, 1, 0))
    s2 = jnp.where(rows == 0, p2, jnp.where(rows == 1, p1, pltpu.roll(cu, 2, 0)))
    conv = convw_ref[0:1, :] * s2 + convw_ref[1:2, :] * s1 + convw_ref[2:3, :] * cu
    cuc_ref[...] = cu[ts - SUBLANES:ts, :]
    a = (proj(2 * d, 3 * d) * conv).astype(_bf16)
    acc_ref[...] = _sigmoid(proj(6 * d, 7 * d)) * _dot(a, woc_ref[...])

    qk0 = 3 * d
    cosv = cos_ref[...]
    sinv = sin_ref[...]
    lane = lax.broadcasted_iota(jnp.int32, (ts, LANES), 1)
    first_half = (lane % RET_DK) < (RET_DK // 2)
    for g in range(4):
        for dst, base, scale in ((qr_ref, qk0, None), (kr_ref, qk0 + 4 * LANES, RET_DK ** -0.5)):
            z = proj(base + g * LANES, base + (g + 1) * LANES)
            zs = jnp.where(first_half, pltpu.roll(z, LANES - RET_DK // 2, 1), pltpu.roll(z, RET_DK // 2, 1))
            r = z * cosv + zs * sinv
            if scale is not None:
                r = r * scale
            dst[:, g * LANES:(g + 1) * LANES] = r
    vb_ref[...] = proj(4 * d, 5 * d).astype(_bf16)

    c = RET_CHUNK
    lane_c = lax.broadcasted_iota(jnp.int32, (c, LANES), 1)
    even = lane_c < RET_DK
    for ci in range(ts // c):
        r0 = ci * c
        for j in range(RET_HEADS // 2):
            q2 = qr_ref[r0:r0 + c, j * LANES:(j + 1) * LANES]
            k2 = kr_ref[r0:r0 + c, j * LANES:(j + 1) * LANES]
            v2 = vb_ref[r0:r0 + c, 2 * j * RET_DV:(2 * j + 2) * RET_DV]
            kt = k2.T
            qq = jnp.concatenate([jnp.where(even, q2, 0.0), jnp.where(even, 0.0, q2)], axis=0).astype(_bf16)
            sc = _dot(qq, kt.astype(_bf16))
            pe = (sc[:c] * dmask_ref[2 * j]).astype(_bf16)
            po = (sc[c:] * dmask_ref[2 * j + 1]).astype(_bf16)
            inner = jnp.concatenate([_dot(pe, v2[:, :RET_DV]), _dot(po, v2[:, RET_DV:])], axis=1)
            st = state_ref[j]
            cross = _dot(q2.astype(_bf16), st.astype(_bf16)) * qd_ref[:, 2 * j * RET_DV:(2 * j + 2) * RET_DV]
            o_ref[r0:r0 + c, 2 * j * RET_DV:(2 * j + 2) * RET_DV] = inner + cross
            upd = _dot((kt * kdt_ref[j]).astype(_bf16), v2)
            state_ref[j] = st * sdec_ref[j] + upd * bmask_ref[...]

    for h in range(RET_HEADS):
        sl = slice(h * RET_DV, (h + 1) * RET_DV)
        sg = proj(5 * d + h * RET_DV, 5 * d + (h + 1) * RET_DV)
        on = _rms(o_ref[:, sl], gret_ref[:, sl])
        og_ref[:, sl] = (sg * _sigmoid(sg) * on).astype(_bf16)
    yr = _dot(og_ref[...], wor_ref[...])
    mixed = acc_ref[...] + _sigmoid(proj(7 * d, 8 * d)) * yr
    x1 = x + _dot(mixed.astype(_bf16), wo_ref[...])
    x1_ref[...] = x1

    h2 = _rms(x1, gmoe_ref[...]).astype(_bf16)
    h2p_ref[...] = _pack_halves(h2)
    lt = lax.dot_general(wrt_ref[...], h2, (((1,), (1,)), ((), ())), preferred_element_type=_f32)
    lt = lt + rbias_ref[...]
    g0, g1, g2, g3 = (lt[i:i + 1, :] for i in range(N_GROUPS))
    gmax = jnp.maximum(jnp.maximum(g0, g1), jnp.maximum(g2, g3))
    grp = jnp.where(g0 == gmax, 0, jnp.where(g1 == gmax, 1, jnp.where(g2 == gmax, 2, 3)))
    gsum = jnp.exp(g0 - gmax) + jnp.exp(g1 - gmax) + jnp.exp(g2 - gmax) + jnp.exp(g3 - gmax)
    g_w = 1.0 / gsum
    e_in = lt[SUBLANES:2 * SUBLANES, :]
    for g in range(1, N_GROUPS):
        e_in = jnp.where(grp == g, lt[(g + 1) * SUBLANES:(g + 2) * SUBLANES, :], e_in)
    ridx = lax.broadcasted_iota(jnp.int32, (EXPERTS_PER_GROUP, ts), 0)
    top1 = jnp.max(e_in, axis=0, keepdims=True)
    i1 = jnp.min(jnp.where(e_in == top1, ridx, EXPERTS_PER_GROUP), axis=0, keepdims=True)
    e_m = jnp.where(ridx == i1, -jnp.inf, e_in)
    top2 = jnp.max(e_m, axis=0, keepdims=True)
    i2 = jnp.min(jnp.where(e_m == top2, ridx, EXPERTS_PER_GROUP), axis=0, keepdims=True)
    ex = jnp.exp(top2 - top1)
    den = 1.0 + ex
    id0 = grp * EXPERTS_PER_GROUP + i1
    id1 = grp * EXPERTS_PER_GROUP + i2
    eid_ref[0:1, :] = id0
    eid_ref[1:2, :] = id1
    ew_ref[0:1, :] = (1.0 / den) * g_w
    ew_ref[1:2, :] = (ex / den) * g_w

    eidx = lax.broadcasted_iota(jnp.int32, (N_EXPERTS, ts), 0)
    oh0 = (eidx == id0).astype(_f32)
    oh1 = (eidx == id1).astype(_f32)
    cnt = (oh0 + oh1).astype(_bf16)
    before = carry_ref[...] + _dot(cnt, tri_ref[0])
    rank_ref[0:1, :] = jnp.sum(oh0 * before, axis=0, keepdims=True).astype(jnp.int32)
    rank_ref[1:2, :] = jnp.sum(oh1 * before, axis=0, keepdims=True).astype(jnp.int32)
    total = carry_ref[...] + _dot(cnt, tri_ref[1])
    carry_ref[...] = total
    cnt_ref[...] = total[:, :LANES].astype(jnp.int32)


def _retention_tables():
    c = RET_CHUNK
    log_gamma = np.log1p(-np.exp2(-5.0 - np.arange(RET_HEADS, dtype=np.float64)))
    pos = np.arange(c, dtype=np.float64)
    diff = pos[:, None] - pos[None, :]
    dmask = np.where((diff >= 0)[None], np.exp(log_gamma[:, None, None] * np.maximum(diff, 0.0)[None]), 0.0)
    q_decay = np.exp(log_gamma[:, None] * (pos[None, :] + 1.0))
    k_decay = np.exp(log_gamma[:, None] * (c - 1.0 - pos[None, :]))
    chunk_decay = np.exp(log_gamma * c)
    qd = np.repeat(q_decay.T, RET_DV, axis=1)
    kdt = np.repeat(k_decay.reshape(RET_HEADS // 2, 2, 1, c), RET_DK, axis=2).reshape(RET_HEADS // 2, 2 * RET_DK, c)
    sdec = np.repeat(chunk_decay.reshape(RET_HEADS // 2, 2, 1), RET_DK, axis=2).reshape(RET_HEADS // 2, 2 * RET_DK, 1)
    sdec = np.broadcast_to(sdec, (RET_HEADS // 2, 2 * RET_DK, 2 * RET_DV))
    rr = np.arange(2 * RET_DK)[:, None] // RET_DK
    cc = np.arange(2 * RET_DV)[None, :] // RET_DV
    bmask = (rr == cc).astype(np.float64)
    return tuple(jnp.asarray(np.ascontiguousarray(v), _f32) for v in (dmask, qd, kdt, sdec, bmask))


def _rope_tables(s_len):
    inv = ROPE_BASE ** (-jnp.arange(0, RET_DK, 2, dtype=_f32) / RET_DK)
    ang = jnp.arange(s_len, dtype=_f32)[:, None] * inv[None, :]
    cos, sin = jnp.cos(ang), jnp.sin(ang)
    cos_t = jnp.tile(cos, (1, LANES // (RET_DK // 2)))
    sin_t = jnp.tile(jnp.concatenate([-sin, sin], axis=1), (1, LANES // RET_DK))
    return cos_t, sin_t


def _mixer_operands(s, d, g_mix, w_in, conv_w, g_ret, w_out_conv, w_out_ret, w_o, g_moe, w_rg, b_rg, w_re, b_re):
    ts = MIX_TILE
    cos_t, sin_t = _rope_tables(s)
    dmask, qd, kdt, sdec, bmask = _retention_tables()
    wrt = jnp.zeros((ROUTER_ROWS, d), _f32)
    wrt = wrt.at[:N_GROUPS].set(w_rg.T).at[SUBLANES:SUBLANES + N_EXPERTS].set(w_re.T).astype(_bf16)
    rb = jnp.zeros((ROUTER_ROWS,), _f32).at[:N_GROUPS].set(b_rg).at[SUBLANES:SUBLANES + N_EXPERTS].set(b_re)
    rbias = jnp.broadcast_to(rb[:, None], (ROUTER_ROWS, ts))
    ii = np.arange(ts)
    tri = jnp.asarray(np.stack([(ii[:, None] < ii[None, :]), np.ones((ts, ts), bool)]), _bf16)
    return (g_mix.reshape(1, d), w_in.astype(_bf16), conv_w, cos_t, sin_t, dmask, qd, kdt, sdec, bmask,
            g_ret.reshape(1, d), w_out_conv.astype(_bf16), w_out_ret.astype(_bf16), w_o.astype(_bf16),
            g_moe.reshape(1, d), wrt, rbias, tri)


def _mixer(x, b0, b, operands):
    _, s, d = x.shape
    ts = MIX_TILE
    nt = s // ts
    tile3 = lambda w: pl.BlockSpec((None, ts, w), lambda bi, ti: (bi, ti, 0))
    route = lambda: pl.BlockSpec((None, 2, ts), lambda bi, ti: (bi, 0, ti))
    rope_spec = pl.BlockSpec((ts, LANES), lambda bi, ti: (ti, 0))
    in_specs = [pl.BlockSpec((None, ts, d), lambda bi, ti: (b0 + bi, ti, 0))]
    in_specs += [rope_spec if i in (3, 4) else _const_spec(op.shape) for i, op in enumerate(operands)]
    out_shape = [
        jax.ShapeDtypeStruct((b, s, d), _f32),
        jax.ShapeDtypeStruct((b, s, d // 2), jnp.uint32),
        jax.ShapeDtypeStruct((b, 2, s), jnp.int32),
        jax.ShapeDtypeStruct((b, 2, s), _f32),
        jax.ShapeDtypeStruct((b, 2, s), jnp.int32),
        jax.ShapeDtypeStruct((N_EXPERTS, LANES), jnp.int32),
    ]
    out_specs = [tile3(d), tile3(d // 2), route(), route(), route(),
                 pl.BlockSpec((N_EXPERTS, LANES), lambda bi, ti: (0, 0))]
    scratch = [
        pltpu.VMEM((ts, d), _bf16),
        pltpu.VMEM((ts, RET_HEADS * RET_DK), _f32),
        pltpu.VMEM((ts, RET_HEADS * RET_DK), _f32),
        pltpu.VMEM((ts, RET_HEADS * RET_DV), _bf16),
        pltpu.VMEM((ts, RET_HEADS * RET_DV), _f32),
        pltpu.VMEM((ts, RET_HEADS * RET_DV), _bf16),
        pltpu.VMEM((ts, d), _f32),
        pltpu.VMEM((SUBLANES, d), _f32),
        pltpu.VMEM((RET_HEADS // 2, 2 * RET_DK, 2 * RET_DV), _f32),
        pltpu.VMEM((N_EXPERTS, ts), _f32),
    ]
    return pl.pallas_call(
        _mixer_kernel,
        grid=(b, nt),
        in_specs=in_specs,
        out_specs=out_specs,
        out_shape=out_shape,
        scratch_shapes=scratch,
        compiler_params=pltpu.CompilerParams(
            dimension_semantics=("arbitrary", "arbitrary"), vmem_limit_bytes=VMEM_LIMIT),
        name="mixer_router",
    )(x, *operands)


def _sc_worker_id():
    return lax.axis_index("s") * SC_CORES + lax.axis_index("c")


def _sc_mesh():
    return plsc.VectorSubcoreMesh(core_axis_name="c", subcore_axis_name="s")


def _sc_dispatch(src, idx3, n_rows):
    t, d = src.shape
    n_win_total, _, win = idx3.shape
    n_win = n_win_total // SC_WORKERS

    @functools.partial(
        pl.kernel, mesh=_sc_mesh(),
        out_type=jax.ShapeDtypeStruct((n_rows, d), src.dtype),
        scratch_types=[pltpu.VMEM((2, win), jnp.int32), pltpu.VMEM((win, d), src.dtype)],
    )
    def dispatch(src_hbm, idx_hbm, out_hbm, idx_v, rows_v):
        wid = _sc_worker_id()

        @pl.loop(0, n_win)
        def _(i):
            w = wid * n_win + i
            off = pl.multiple_of(w * win, SUBLANES)
            pltpu.sync_copy(idx_hbm.at[w], idx_v)
            pltpu.sync_copy(src_hbm.at[pl.ds(off, win)], rows_v)
            pltpu.sync_copy(rows_v, out_hbm.at[idx_v.at[0]])
            pltpu.sync_copy(rows_v, out_hbm.at[idx_v.at[1]])

    return dispatch(src, idx3)


def _sc_gather(table, idx):
    n = idx.shape[0]
    d = table.shape[1]
    win = SC_WINDOW
    per_w = n // SC_WORKERS
    n_win = per_w // win

    @functools.partial(
        pl.kernel, mesh=_sc_mesh(),
        out_type=jax.ShapeDtypeStruct((n, d), table.dtype),
        scratch_types=[pltpu.VMEM((win,), jnp.int32), pltpu.VMEM((win, d), table.dtype),
                       pltpu.SemaphoreType.DMA],
    )
    def gather(table_hbm, idx_hbm, out_hbm, idx_v, rows_v, sem):
        base = _sc_worker_id() * per_w

        @pl.loop(0, n_win)
        def _(i):
            off = pl.multiple_of(base + i * win, SUBLANES)
            pltpu.sync_copy(idx_hbm.at[pl.ds(off, win)], idx_v)
            pltpu.async_copy(table_hbm.at[idx_v], rows_v, sem).wait()
            pltpu.sync_copy(rows_v, out_hbm.at[pl.ds(off, win)])

    return gather(table, idx)


def _expert_kernel(blk0_ref, nblk_ref, count_ref, xs_ref, wg_ref, wu_ref, wd_ref, y_ref,
                   xbuf, ybuf, wgb_ref, wub_ref, wdb_ref, sem_in, sem_out):
    e = pl.program_id(0)
    blk0 = blk0_ref[e]
    nblk = nblk_ref[e]
    count = count_ref[e]
    half = xbuf.shape[2]

    def rows_in(j, slot):
        return pltpu.make_async_copy(xs_ref.at[pl.ds((blk0 + j) * MOE_BLOCK, MOE_BLOCK)], xbuf.at[slot],
                                     sem_in.at[slot])

    def rows_out(j, slot):
        return pltpu.make_async_copy(ybuf.at[slot], y_ref.at[pl.ds((blk0 + j) * MOE_BLOCK, MOE_BLOCK)],
                                     sem_out.at[slot])

    n_in = xbuf.shape[0]
    for j0 in range(n_in - 1):
        @pl.when(j0 < nblk)
        def _():
            rows_in(j0, j0).start()

    wgb_ref[...] = wg_ref[...].astype(_bf16)
    wub_ref[...] = wu_ref[...].astype(_bf16)
    wdb_ref[...] = wd_ref[...].astype(_bf16)

    def block(j, carry):
        slot = j % 2
        islot = j % n_in
        rows_in(j, islot).wait()

        @pl.when(j + n_in - 1 < nblk)
        def _():
            rows_in(j + n_in - 1, (j + n_in - 1) % n_in).start()

        @pl.when(j >= 2)
        def _():
            rows_out(j - 2, slot).wait()

        rowid = lax.broadcasted_iota(jnp.int32, (MOE_BLOCK, half), 0)
        lo, hi = _unpack_halves(jnp.where(rowid < count - j * MOE_BLOCK, xbuf[islot], jnp.uint32(0)))
        lo = lo.astype(_bf16)
        hi = hi.astype(_bf16)
        gate = _dot(lo, wgb_ref[:half, :]) + _dot(hi, wgb_ref[half:, :])
        up = _dot(lo, wub_ref[:half, :]) + _dot(hi, wub_ref[half:, :])
        hid = (gate * _sigmoid(gate) * up).astype(_bf16)
        ybuf[slot] = _pack_halves(_dot(hid, wdb_ref[...]))
        rows_out(j, slot).start()
        return carry

    lax.fori_loop(0, nblk, block, 0)

    @pl.when(nblk >= 2)
    def _():
        rows_out(nblk - 2, nblk % 2).wait()

    @pl.when(nblk >= 1)
    def _():
        rows_out(nblk - 1, (nblk - 1) % 2).wait()


def _experts(xs, w_gate, w_up, w_down, blk0, nblk, counts):
    n_rows, half = xs.shape
    d = 2 * half
    ne, _, de = w_gate.shape
    grid_spec = pltpu.PrefetchScalarGridSpec(
        num_scalar_prefetch=3,
        grid=(ne,),
        in_specs=[
            pl.BlockSpec(memory_space=pl.ANY),
            pl.BlockSpec((None, d, de), lambda e, *_: (e, 0, 0)),
            pl.BlockSpec((None, d, de), lambda e, *_: (e, 0, 0)),
            pl.BlockSpec((None, de, d), lambda e, *_: (e, 0, 0)),
        ],
        out_specs=pl.BlockSpec(memory_space=pl.ANY),
        scratch_shapes=[
            pltpu.VMEM((EXPERT_IN_SLOTS, MOE_BLOCK, half), jnp.uint32), pltpu.VMEM((2, MOE_BLOCK, half), jnp.uint32),
            pltpu.VMEM((d, de), _bf16), pltpu.VMEM((d, de), _bf16), pltpu.VMEM((de, d), _bf16),
            pltpu.SemaphoreType.DMA((EXPERT_IN_SLOTS,)), pltpu.SemaphoreType.DMA((2,)),
        ],
    )
    return pl.pallas_call(
        _expert_kernel,
        grid_spec=grid_spec,
        out_shape=jax.ShapeDtypeStruct((n_rows, half), jnp.uint32),
        compiler_params=pltpu.CompilerParams(dimension_semantics=("arbitrary",), vmem_limit_bytes=VMEM_LIMIT),
        name="experts",
    )(blk0, nblk, counts, xs, w_gate, w_up, w_down)


def _combine_kernel(ew_ref, x1_ref, p_ref, y2_ref, gin_ref, wpg_ref, wpp_ref, gpost_ref, gfin_ref, *rest):
    out_ref = rest[-1]
    tc, d = x1_ref.shape
    x2 = x1_ref[...]
    for k in range(2):
        wcol = jnp.broadcast_to(ew_ref[k:k + 1, :], (LANES, tc)).T
        yk = jnp.concatenate(_unpack_halves(y2_ref[k]), axis=1)
        x2 = x2 + jnp.tile(wcol, (1, d // LANES)) * yk
    gate = _sigmoid(_dot(_rms(x2, gin_ref[...]).astype(_bf16), wpg_ref[...]))
    ple = _rms(_dot(p_ref[...].astype(_bf16), wpp_ref[...]), gpost_ref[...])
    out_ref[...] = _rms(x2 + gate * ple, gfin_ref[...])


def _combine(x1, p0, y2, ew, gains_and_weights, b0, b_total, prev_out):
    b, s, d = x1.shape
    tc = COMBINE_TILE
    pdim = p0.shape[-1]
    vec = lambda: pl.BlockSpec((1, d), lambda bi, ti: (0, 0))
    in_specs = [
        pl.BlockSpec((None, 2, tc), lambda bi, ti: (bi, 0, ti)),
        pl.BlockSpec((None, tc, d), lambda bi, ti: (bi, ti, 0)),
        pl.BlockSpec((None, tc, pdim), lambda bi, ti: (b0 + bi, ti, 0)),
        pl.BlockSpec((None, 2, tc, d // 2), lambda bi, ti: (bi, 0, ti, 0)),
        vec(),
        pl.BlockSpec((d, d), lambda bi, ti: (0, 0)),
        pl.BlockSpec((pdim, d), lambda bi, ti: (0, 0)),
        vec(), vec(),
    ]
    args = [ew, x1, p0, y2, *gains_and_weights]
    aliases = {}
    if prev_out is not None:
        in_specs.append(pl.BlockSpec(memory_space=pl.ANY))
        aliases = {len(args): 0}
        args.append(prev_out)
    return pl.pallas_call(
        _combine_kernel,
        grid=(b, s // tc),
        in_specs=in_specs,
        out_specs=pl.BlockSpec((None, tc, d), lambda bi, ti: (b0 + bi, ti, 0)),
        out_shape=jax.ShapeDtypeStruct((b_total, s, d), _f32),
        input_output_aliases=aliases,
        compiler_params=pltpu.CompilerParams(
            dimension_semantics=("arbitrary", "arbitrary"), vmem_limit_bytes=VMEM_LIMIT),
        name="combine_ple",
    )(*args)


def _layer(x, p_i, g_mix, w_in, conv_w, g_ret, w_out_conv, w_out_ret, w_o, g_moe, w_rg, b_rg, w_re, b_re,
           w_exp_gate, w_exp_up, w_exp_down, g_ple_in, w_ple_gate, w_ple_proj, g_ple_post, g_out):
    b_total, s, d = x.shape
    b = b_total // TOKEN_SPLITS
    n_tok = b * s
    operands = _mixer_operands(s, d, g_mix, w_in, conv_w, g_ret, w_out_conv, w_out_ret, w_o, g_moe,
                               w_rg, b_rg, w_re, b_re)
    tail = (g_ple_in.reshape(1, d), w_ple_gate.astype(_bf16), w_ple_proj.astype(_bf16),
            g_ple_post.reshape(1, d), g_out.reshape(1, d))
    nblk = (2 * n_tok + N_EXPERTS * (MOE_BLOCK - 1) + MOE_BLOCK - 1) // MOE_BLOCK
    n_rows = nblk * MOE_BLOCK
    win = SC_WINDOW

    def route(eid, rank, cnt):
        counts = cnt[:, 0]
        padded = (counts + MOE_BLOCK - 1) // MOE_BLOCK * MOE_BLOCK
        pends = jnp.cumsum(padded)
        pstarts = pends - padded
        eids = jnp.arange(N_EXPERTS, dtype=jnp.int32).reshape(N_EXPERTS, 1, 1, 1)
        seg = jnp.sum(jnp.where(eid[None] == eids, pstarts.reshape(N_EXPERTS, 1, 1, 1), 0), axis=0)
        dest = rank + seg
        idx3 = dest.reshape(b, 2, s // win, win).transpose(0, 2, 1, 3).reshape(n_tok // win, 2, win)
        return dest, idx3, pstarts // MOE_BLOCK, padded // MOE_BLOCK, counts

    st = [dict() for _ in range(TOKEN_SPLITS)]
    out = None

    def mix(h):
        x1, h2p, eid, ew, rank, cnt = _mixer(x, h * b, b, operands)
        dest, idx3, blk0, nblk_e, counts = route(eid, rank, cnt)
        st[h].update(x1=x1, h2p=h2p, ew=ew, dest=dest, idx3=idx3, seg=(blk0, nblk_e, counts))

    def dispatch(h):
        st[h]["xs"] = _sc_dispatch(st[h]["h2p"].reshape(n_tok, d // 2), st[h]["idx3"], n_rows)

    def experts(h):
        st[h]["y"] = _experts(st[h]["xs"], w_exp_gate, w_exp_up, w_exp_down, *st[h]["seg"])

    def gather(h):
        st[h]["y2"] = _sc_gather(st[h]["y"], st[h]["dest"].reshape(-1)).reshape(b, 2, s, d // 2)

    def combine(h):
        nonlocal out
        out = _combine(st[h]["x1"], p_i, st[h]["y2"], st[h]["ew"], tail, h * b, b_total, out)

    stages = (mix, dispatch, experts, gather, combine)
    for step in range(TOKEN_SPLITS + len(stages) - 1):
        for k in reversed(range(len(stages))):
            if 0 <= step - k < TOKEN_SPLITS:
                stages[k](step - k)
    return out


def kernel(x, p, g_mix, w_in, conv_w, g_ret, w_out_conv, w_out_ret, w_o, g_moe, w_rg, b_rg, w_re, b_re, w_exp_gate, w_exp_up, w_exp_down, g_ple_in, w_ple_gate, w_ple_proj, g_ple_post, g_final):
    depth = p.shape[0]
    assert depth == 1, "the final norm is fused into the single layer's combine kernel"
    return _layer(x, p[0], g_mix[0], w_in[0], conv_w[0], g_ret[0], w_out_conv[0], w_out_ret[0], w_o[0],
                  g_moe[0], w_rg[0], b_rg[0], w_re[0], b_re[0], w_exp_gate[0], w_exp_up[0], w_exp_down[0],
                  g_ple_in[0], w_ple_gate[0], w_ple_proj[0], g_ple_post[0], g_final)
```

```python
import functools

import jax
import jax.numpy as jnp
import numpy as np
from jax import lax
from jax.experimental import pallas as pl
from jax.experimental.pallas import tpu as pltpu
from jax.experimental.pallas import tpu_sc as plsc

EPS = 1e-6
CONV_K = 3
RET_HEADS = 8
RET_DK = 64
RET_DV = 128
RET_CHUNK = 128
ROPE_BASE = 10000.0
N_GROUPS = 4
EXPERTS_PER_GROUP = 8
N_EXPERTS = N_GROUPS * EXPERTS_PER_GROUP
MOE_BLOCK = 256
LANES = 128
SUBLANES = 8
ROUTER_ROWS = 48
MIX_TILE = 256
COMBINE_TILE = 256
EXPERT_SPAN = 8
TOKEN_SPLITS = 2
SC_CORES = 2
SC_SUBCORES = 16
SC_WORKERS = SC_CORES * SC_SUBCORES
SC_WINDOW = 128
BF16_BITS = 16
HIGH_HALF = np.uint32(0xFFFF0000)
VMEM_LIMIT = 56 * 1024 * 1024

_bf16 = jnp.bfloat16
_f32 = jnp.float32


def _sigmoid(v):
    return 1.0 / (1.0 + jnp.exp(-v))


def _rms(v, g):
    ms = jnp.mean(v * v, axis=-1, keepdims=True)
    return v * lax.rsqrt(ms + EPS) * g


def _dot(a, b):
    return jnp.dot(a, b, preferred_element_type=_f32)


def _pack_halves(v):
    bits = lax.bitcast_convert_type(v.astype(_bf16).astype(_f32), jnp.uint32)
    c = v.shape[1] // 2
    return (bits[:, :c] >> BF16_BITS) | (bits[:, c:] & HIGH_HALF)


def _unpack_halves(w):
    lo = lax.bitcast_convert_type(w << BF16_BITS, _f32)
    hi = lax.bitcast_convert_type(w & HIGH_HALF, _f32)
    return lo, hi


def _const_spec(shape):
    nd = len(shape)
    return pl.BlockSpec(shape, lambda *_: (0,) * nd, pipeline_mode=pl.Buffered(1))


def _mixer_kernel(x_ref, gmix_ref, win_ref, convw_ref, cos_ref, sin_ref, dmask_ref, qd_ref, kdt_ref,
                  sdec_ref, bmask_ref, gret_ref, woc_ref, wor_ref, wo_ref, gmoe_ref, wrt_ref, rbias_ref,
                  tri_ref,
                  x1_ref, h2p_ref, eid_ref, ew_ref, rank_ref, cnt_ref,
                  hb_ref, qr_ref, kr_ref, vb_ref, o_ref, og_ref, acc_ref, cuc_ref, state_ref, carry_ref):
    ts, d = x_ref.shape
    t = pl.program_id(1)

    @pl.when(t == 0)
    def _():
        cuc_ref[...] = jnp.zeros_like(cuc_ref)
        state_ref[...] = jnp.zeros_like(state_ref)

    @pl.when(jnp.logical_and(t == 0, pl.program_id(0) == 0))
    def _():
        carry_ref[...] = jnp.zeros_like(carry_ref)

    x = x_ref[...]
    hb_ref[...] = _rms(x, gmix_ref[...]).astype(_bf16)
    hb = hb_ref[...]

    def proj(lo, hi):
        return _dot(hb, win_ref[:, lo:hi])

    cu = proj(d, 2 * d) * proj(0, d)
    prev = cuc_ref[...]
    p1 = prev[SUBLANES - 1:SUBLANES, :]
    p2 = prev[SUBLANES - 2:SUBLANES - 1, :]
    rows = lax.broadcasted_iota(jnp.int32, (ts, d), 0)
    s1 = jnp.where(rows == 0, p1, pltpu.roll(cu, 1, 0))
    s2 = jnp.where(rows == 0, p2, jnp.where(rows == 1, p1, pltpu.roll(cu, 2, 0)))
    conv = convw_ref[0:1, :] * s2 + convw_ref[1:2, :] * s1 + convw_ref[2:3, :] * cu
    cuc_ref[...] = cu[ts - SUBLANES:ts, :]
    a = (proj(2 * d, 3 * d) * conv).astype(_bf16)
    acc_ref[...] = _sigmoid(proj(6 * d, 7 * d)) * _dot(a, woc_ref[...])

    qk0 = 3 * d
    cosv = cos_ref[...]
    sinv = sin_ref[...]
    lane = lax.broadcasted_iota(jnp.int32, (ts, LANES), 1)
    first_half = (lane % RET_DK) < (RET_DK // 2)
    for g in range(4):
        for dst, base, scale in ((qr_ref, qk0, None), (kr_ref, qk0 + 4 * LANES, RET_DK ** -0.5)):
            z = proj(base + g * LANES, base + (g + 1) * LANES)
            zs = jnp.where(first_half, pltpu.roll(z, LANES - RET_DK // 2, 1), pltpu.roll(z, RET_DK // 2, 1))
            r = z * cosv + zs * sinv
            if scale is not None:
                r = r * scale
            dst[:, g * LANES:(g + 1) * LANES] = r
    vb_ref[...] = proj(4 * d, 5 * d).astype(_bf16)

    c = RET_CHUNK
    lane_c = lax.broadcasted_iota(jnp.int32, (c, LANES), 1)
    even = lane_c < RET_DK
    for ci in range(ts // c):
        r0 = ci * c
        for j in range(RET_HEADS // 2):
            q2 = qr_ref[r0:r0 + c, j * LANES:(j + 1) * LANES]
            k2 = kr_ref[r0:r0 + c, j * LANES:(j + 1) * LANES]
            v2 = vb_ref[r0:r0 + c, 2 * j * RET_DV:(2 * j + 2) * RET_DV]
            kt = k2.T
            qq = jnp.concatenate([jnp.where(even, q2, 0.0), jnp.where(even, 0.0, q2)], axis=0).astype(_bf16)
            sc = _dot(qq, kt.astype(_bf16))
            pe = (sc[:c] * dmask_ref[2 * j]).astype(_bf16)
            po = (sc[c:] * dmask_ref[2 * j + 1]).astype(_bf16)
            inner = jnp.concatenate([_dot(pe, v2[:, :RET_DV]), _dot(po, v2[:, RET_DV:])], axis=1)
            st = state_ref[j]
            cross = _dot(q2.astype(_bf16), st.astype(_bf16)) * qd_ref[:, 2 * j * RET_DV:(2 * j + 2) * RET_DV]
            o_ref[r0:r0 + c, 2 * j * RET_DV:(2 * j + 2) * RET_DV] = inner + cross
            upd = _dot((kt * kdt_ref[j]).astype(_bf16), v2)
            state_ref[j] = st * sdec_ref[j] + upd * bmask_ref[...]

    for h in range(RET_HEADS):
        sl = slice(h * RET_DV, (h + 1) * RET_DV)
        sg = proj(5 * d + h * RET_DV, 5 * d + (h + 1) * RET_DV)
        on = _rms(o_ref[:, sl], gret_ref[:, sl])
        og_ref[:, sl] = (sg * _sigmoid(sg) * on).astype(_bf16)
    yr = _dot(og_ref[...], wor_ref[...])
    mixed = acc_ref[...] + _sigmoid(proj(7 * d, 8 * d)) * yr
    x1 = x + _dot(mixed.astype(_bf16), wo_ref[...])
    x1_ref[...] = x1

    h2 = _rms(x1, gmoe_ref[...]).astype(_bf16)
    h2p_ref[...] = _pack_halves(h2)
    lt = lax.dot_general(wrt_ref[...], h2, (((1,), (1,)), ((), ())), preferred_element_type=_f32)
    lt = lt + rbias_ref[...]
    g0, g1, g2, g3 = (lt[i:i + 1, :] for i in range(N_GROUPS))
    gmax = jnp.maximum(jnp.maximum(g0, g1), jnp.maximum(g2, g3))
    grp = jnp.where(g0 == gmax, 0, jnp.where(g1 == gmax, 1, jnp.where(g2 == gmax, 2, 3)))
    gsum = jnp.exp(g0 - gmax) + jnp.exp(g1 - gmax) + jnp.exp(g2 - gmax) + jnp.exp(g3 - gmax)
    g_w = 1.0 / gsum
    e_in = lt[SUBLANES:2 * SUBLANES, :]
    for g in range(1, N_GROUPS):
        e_in = jnp.where(grp == g, lt[(g + 1) * SUBLANES:(g + 2) * SUBLANES, :], e_in)
    ridx = lax.broadcasted_iota(jnp.int32, (EXPERTS_PER_GROUP, ts), 0)
    top1 = jnp.max(e_in, axis=0, keepdims=True)
    i1 = jnp.min(jnp.where(e_in == top1, ridx, EXPERTS_PER_GROUP), axis=0, keepdims=True)
    e_m = jnp.where(ridx == i1, -jnp.inf, e_in)
    top2 = jnp.max(e_m, axis=0, keepdims=True)
    i2 = jnp.min(jnp.where(e_m == top2, ridx, EXPERTS_PER_GROUP), axis=0, keepdims=True)
    ex = jnp.exp(top2 - top1)
    den = 1.0 + ex
    id0 = grp * EXPERTS_PER_GROUP + i1
    id1 = grp * EXPERTS_PER_GROUP + i2
    eid_ref[0:1, :] = id0
    eid_ref[1:2, :] = id1
    ew_ref[0:1, :] = (1.0 / den) * g_w
    ew_ref[1:2, :] = (ex / den) * g_w

    eidx = lax.broadcasted_iota(jnp.int32, (N_EXPERTS, ts), 0)
    oh0 = (eidx == id0).astype(_f32)
    oh1 = (eidx == id1).astype(_f32)
    cnt = (oh0 + oh1).astype(_bf16)
    before = carry_ref[...] + _dot(cnt, tri_ref[0])
    rank_ref[0:1, :] = jnp.sum(oh0 * before, axis=0, keepdims=True).astype(jnp.int32)
    rank_ref[1:2, :] = jnp.sum(oh1 * before, axis=0, keepdims=True).astype(jnp.int32)
    total = carry_ref[...] + _dot(cnt, tri_ref[1])
    carry_ref[...] = total
    cnt_ref[...] = total[:, :LANES].astype(jnp.int32)


def _retention_tables():
    c = RET_CHUNK
    log_gamma = np.log1p(-np.exp2(-5.0 - np.arange(RET_HEADS, dtype=np.float64)))
    pos = np.arange(c, dtype=np.float64)
    diff = pos[:, None] - pos[None, :]
    dmask = np.where((diff >= 0)[None], np.exp(log_gamma[:, None, None] * np.maximum(diff, 0.0)[None]), 0.0)
    q_decay = np.exp(log_gamma[:, None] * (pos[None, :] + 1.0))
    k_decay = np.exp(log_gamma[:, None] * (c - 1.0 - pos[None, :]))
    chunk_decay = np.exp(log_gamma * c)
    qd = np.repeat(q_decay.T, RET_DV, axis=1)
    kdt = np.repeat(k_decay.reshape(RET_HEADS // 2, 2, 1, c), RET_DK, axis=2).reshape(RET_HEADS // 2, 2 * RET_DK, c)
    sdec = np.repeat(chunk_decay.reshape(RET_HEADS // 2, 2, 1), RET_DK, axis=2).reshape(RET_HEADS // 2, 2 * RET_DK, 1)
    sdec = np.broadcast_to(sdec, (RET_HEADS // 2, 2 * RET_DK, 2 * RET_DV))
    rr = np.arange(2 * RET_DK)[:, None] // RET_DK
    cc = np.arange(2 * RET_DV)[None, :] // RET_DV
    bmask = (rr == cc).astype(np.float64)
    return tuple(jnp.asarray(np.ascontiguousarray(v), _f32) for v in (dmask, qd, kdt, sdec, bmask))


def _rope_tables(s_len):
    inv = ROPE_BASE ** (-jnp.arange(0, RET_DK, 2, dtype=_f32) / RET_DK)
    ang = jnp.arange(s_len, dtype=_f32)[:, None] * inv[None, :]
    cos, sin = jnp.cos(ang), jnp.sin(ang)
    cos_t = jnp.tile(cos, (1, LANES // (RET_DK // 2)))
    sin_t = jnp.tile(jnp.concatenate([-sin, sin], axis=1), (1, LANES // RET_DK))
    return cos_t, sin_t


def _mixer_operands(s, d, g_mix, w_in, conv_w, g_ret, w_out_conv, w_out_ret, w_o, g_moe, w_rg, b_rg, w_re, b_re):
    ts = MIX_TILE
    cos_t, sin_t = _rope_tables(s)
    dmask, qd, kdt, sdec, bmask = _retention_tables()
    wrt = jnp.zeros((ROUTER_ROWS, d), _f32)
    wrt = wrt.at[:N_GROUPS].set(w_rg.T).at[SUBLANES:SUBLANES + N_EXPERTS].set(w_re.T).astype(_bf16)
    rb = jnp.zeros((ROUTER_ROWS,), _f32).at[:N_GROUPS].set(b_rg).at[SUBLANES:SUBLANES + N_EXPERTS].set(b_re)
    rbias = jnp.broadcast_to(rb[:, None], (ROUTER_ROWS, ts))
    ii = np.arange(ts)
    tri = jnp.asarray(np.stack([(ii[:, None] < ii[None, :]), np.ones((ts, ts), bool)]), _bf16)
    return (g_mix.reshape(1, d), w_in.astype(_bf16), conv_w, cos_t, sin_t, dmask, qd, kdt, sdec, bmask,
            g_ret.reshape(1, d), w_out_conv.astype(_bf16), w_out_ret.astype(_bf16), w_o.astype(_bf16),
            g_moe.reshape(1, d), wrt, rbias, tri)


def _mixer(x, b0, b, operands):
    _, s, d = x.shape
    ts = MIX_TILE
    nt = s // ts
    tile3 = lambda w: pl.BlockSpec((None, ts, w), lambda bi, ti: (bi, ti, 0))
    route = lambda: pl.BlockSpec((None, 2, ts), lambda bi, ti: (bi, 0, ti))
    rope_spec = pl.BlockSpec((ts, LANES), lambda bi, ti: (ti, 0))
    in_specs = [pl.BlockSpec((None, ts, d), lambda bi, ti: (b0 + bi, ti, 0))]
    in_specs += [rope_spec if i in (3, 4) else _const_spec(op.shape) for i, op in enumerate(operands)]
    out_shape = [
        jax.ShapeDtypeStruct((b, s, d), _f32),
        jax.ShapeDtypeStruct((b, s, d // 2), jnp.uint32),
        jax.ShapeDtypeStruct((b, 2, s), jnp.int32),
        jax.ShapeDtypeStruct((b, 2, s), _f32),
        jax.ShapeDtypeStruct((b, 2, s), jnp.int32),
        jax.ShapeDtypeStruct((N_EXPERTS, LANES), jnp.int32),
    ]
    out_specs = [tile3(d), tile3(d // 2), route(), route(), route(),
                 pl.BlockSpec((N_EXPERTS, LANES), lambda bi, ti: (0, 0))]
    scratch = [
        pltpu.VMEM((ts, d), _bf16),
        pltpu.VMEM((ts, RET_HEADS * RET_DK), _f32),
        pltpu.VMEM((ts, RET_HEADS * RET_DK), _f32),
        pltpu.VMEM((ts, RET_HEADS * RET_DV), _bf16),
        pltpu.VMEM((ts, RET_HEADS * RET_DV), _f32),
        pltpu.VMEM((ts, RET_HEADS * RET_DV), _bf16),
        pltpu.VMEM((ts, d), _f32),
        pltpu.VMEM((SUBLANES, d), _f32),
        pltpu.VMEM((RET_HEADS // 2, 2 * RET_DK, 2 * RET_DV), _f32),
        pltpu.VMEM((N_EXPERTS, ts), _f32),
    ]
    return pl.pallas_call(
        _mixer_kernel,
        grid=(b, nt),
        in_specs=in_specs,
        out_specs=out_specs,
        out_shape=out_shape,
        scratch_shapes=scratch,
        compiler_params=pltpu.CompilerParams(
            dimension_semantics=("arbitrary", "arbitrary"), vmem_limit_bytes=VMEM_LIMIT),
        name="mixer_router",
    )(x, *operands)


def _sc_worker_id():
    return lax.axis_index("s") * SC_CORES + lax.axis_index("c")


def _sc_mesh():
    return plsc.VectorSubcoreMesh(core_axis_name="c", subcore_axis_name="s")


def _sc_dispatch(src, idx3, n_rows):
    t, d = src.shape
    n_win_total, _, win = idx3.shape
    n_win = n_win_total // SC_WORKERS

    @functools.partial(
        pl.kernel, mesh=_sc_mesh(),
        out_type=jax.ShapeDtypeStruct((n_rows, d), src.dtype),
        scratch_types=[pltpu.VMEM((2, win), jnp.int32), pltpu.VMEM((win, d), src.dtype)],
    )
    def dispatch(src_hbm, idx_hbm, out_hbm, idx_v, rows_v):
        wid = _sc_worker_id()

        @pl.loop(0, n_win)
        def _(i):
            w = wid * n_win + i
            off = pl.multiple_of(w * win, SUBLANES)
            pltpu.sync_copy(idx_hbm.at[w], idx_v)
            pltpu.sync_copy(src_hbm.at[pl.ds(off, win)], rows_v)
            pltpu.sync_copy(rows_v, out_hbm.at[idx_v.at[0]])
            pltpu.sync_copy(rows_v, out_hbm.at[idx_v.at[1]])

    return dispatch(src, idx3)


def _sc_gather(table, idx):
    n = idx.shape[0]
    d = table.shape[1]
    win = SC_WINDOW
    per_w = n // SC_WORKERS
    n_win = per_w // win

    @functools.partial(
        pl.kernel, mesh=_sc_mesh(),
        out_type=jax.ShapeDtypeStruct((n, d), table.dtype),
        scratch_types=[pltpu.VMEM((win,), jnp.int32), pltpu.VMEM((win, d), table.dtype),
                       pltpu.SemaphoreType.DMA],
    )
    def gather(table_hbm, idx_hbm, out_hbm, idx_v, rows_v, sem):
        base = _sc_worker_id() * per_w

        @pl.loop(0, n_win)
        def _(i):
            off = pl.multiple_of(base + i * win, SUBLANES)
            pltpu.sync_copy(idx_hbm.at[pl.ds(off, win)], idx_v)
            pltpu.async_copy(table_hbm.at[idx_v], rows_v, sem).wait()
            pltpu.sync_copy(rows_v, out_hbm.at[pl.ds(off, win)])

    return gather(table, idx)


def _expert_kernel(blk0_ref, nblk_ref, count_ref, xs_ref, wg_ref, wu_ref, wd_ref, y_ref,
                   xbuf, ybuf, wgb_ref, wub_ref, wdb_ref, sem_in, sem_out):
    e = pl.program_id(0)
    last = pl.num_programs(0) - 1
    blk0 = blk0_ref[e]
    nblk = nblk_ref[e]
    count = count_ref[e]
    total = blk0_ref[last] + nblk_ref[last]
    n_span = (total + EXPERT_SPAN - 1) // EXPERT_SPAN
    span_rows = EXPERT_SPAN * MOE_BLOCK
    half = xbuf.shape[2]

    def span_in(s):
        return pltpu.make_async_copy(xs_ref.at[pl.ds(s * span_rows, span_rows)], xbuf.at[s % 2], sem_in.at[s % 2])

    def span_out(s):
        return pltpu.make_async_copy(ybuf.at[s % 2], y_ref.at[pl.ds(s * span_rows, span_rows)], sem_out.at[s % 2])

    @pl.when(e == 0)
    def _():
        ybuf[...] = jnp.zeros_like(ybuf)

    @pl.when(jnp.logical_and(e == 0, n_span > 0))
    def _():
        span_in(0).start()

    wgb_ref[...] = wg_ref[...].astype(_bf16)
    wub_ref[...] = wu_ref[...].astype(_bf16)
    wdb_ref[...] = wd_ref[...].astype(_bf16)

    def block(j, carry):
        g = blk0 + j
        s = g // EXPERT_SPAN
        k = g % EXPERT_SPAN
        slot = s % 2
        row0 = pl.multiple_of(k * MOE_BLOCK, MOE_BLOCK)

        @pl.when(k == 0)
        def _():
            span_in(s).wait()

            @pl.when(s + 1 < n_span)
            def _():
                span_in(s + 1).start()

            @pl.when(s >= 2)
            def _():
                span_out(s - 2).wait()

        rowid = lax.broadcasted_iota(jnp.int32, (MOE_BLOCK, half), 0)
        xb = xbuf[slot, pl.ds(row0, MOE_BLOCK), :]
        lo, hi = _unpack_halves(jnp.where(rowid < count - j * MOE_BLOCK, xb, jnp.uint32(0)))
        lo = lo.astype(_bf16)
        hi = hi.astype(_bf16)
        gate = _dot(lo, wgb_ref[:half, :]) + _dot(hi, wgb_ref[half:, :])
        up = _dot(lo, wub_ref[:half, :]) + _dot(hi, wub_ref[half:, :])
        hid = (gate * _sigmoid(gate) * up).astype(_bf16)
        ybuf[slot, pl.ds(row0, MOE_BLOCK), :] = _pack_halves(_dot(hid, wdb_ref[...]))

        @pl.when(jnp.logical_or(k == EXPERT_SPAN - 1, g == total - 1))
        def _():
            span_out(s).start()
        return carry

    lax.fori_loop(0, nblk, block, 0)

    @pl.when(jnp.logical_and(e == last, n_span >= 2))
    def _():
        span_out(n_span - 2).wait()

    @pl.when(jnp.logical_and(e == last, n_span >= 1))
    def _():
        span_out(n_span - 1).wait()


def _experts(xs, w_gate, w_up, w_down, blk0, nblk, counts):
    n_rows, half = xs.shape
    d = 2 * half
    ne, _, de = w_gate.shape
    grid_spec = pltpu.PrefetchScalarGridSpec(
        num_scalar_prefetch=3,
        grid=(ne,),
        in_specs=[
            pl.BlockSpec(memory_space=pl.ANY),
            pl.BlockSpec((None, d, de), lambda e, *_: (e, 0, 0)),
            pl.BlockSpec((None, d, de), lambda e, *_: (e, 0, 0)),
            pl.BlockSpec((None, de, d), lambda e, *_: (e, 0, 0)),
        ],
        out_specs=pl.BlockSpec(memory_space=pl.ANY),
        scratch_shapes=[
            pltpu.VMEM((2, EXPERT_SPAN * MOE_BLOCK, half), jnp.uint32),
            pltpu.VMEM((2, EXPERT_SPAN * MOE_BLOCK, half), jnp.uint32),
            pltpu.VMEM((d, de), _bf16), pltpu.VMEM((d, de), _bf16), pltpu.VMEM((de, d), _bf16),
            pltpu.SemaphoreType.DMA((2,)), pltpu.SemaphoreType.DMA((2,)),
        ],
    )
    return pl.pallas_call(
        _expert_kernel,
        grid_spec=grid_spec,
        out_shape=jax.ShapeDtypeStruct((n_rows, half), jnp.uint32),
        compiler_params=pltpu.CompilerParams(dimension_semantics=("arbitrary",), vmem_limit_bytes=VMEM_LIMIT),
        name="experts",
    )(blk0, nblk, counts, xs, w_gate, w_up, w_down)


def _combine_kernel(ew_ref, x1_ref, p_ref, y2_ref, gin_ref, wpg_ref, wpp_ref, gpost_ref, gfin_ref, *rest):
    out_ref = rest[-1]
    tc, d = x1_ref.shape
    x2 = x1_ref[...]
    for k in range(2):
        wcol = jnp.broadcast_to(ew_ref[k:k + 1, :], (LANES, tc)).T
        yk = jnp.concatenate(_unpack_halves(y2_ref[k]), axis=1)
        x2 = x2 + jnp.tile(wcol, (1, d // LANES)) * yk
    gate = _sigmoid(_dot(_rms(x2, gin_ref[...]).astype(_bf16), wpg_ref[...]))
    ple = _rms(_dot(p_ref[...].astype(_bf16), wpp_ref[...]), gpost_ref[...])
    out_ref[...] = _rms(x2 + gate * ple, gfin_ref[...])


def _combine(x1, p0, y2, ew, gains_and_weights, b0, b_total, prev_out):
    b, s, d = x1.shape
    tc = COMBINE_TILE
    pdim = p0.shape[-1]
    vec = lambda: pl.BlockSpec((1, d), lambda bi, ti: (0, 0))
    in_specs = [
        pl.BlockSpec((None, 2, tc), lambda bi, ti: (bi, 0, ti)),
        pl.BlockSpec((None, tc, d), lambda bi, ti: (bi, ti, 0)),
        pl.BlockSpec((None, tc, pdim), lambda bi, ti: (b0 + bi, ti, 0)),
        pl.BlockSpec((None, 2, tc, d // 2), lambda bi, ti: (bi, 0, ti, 0)),
        vec(),
        pl.BlockSpec((d, d), lambda bi, ti: (0, 0)),
        pl.BlockSpec((pdim, d), lambda bi, ti: (0, 0)),
        vec(), vec(),
    ]
    args = [ew, x1, p0, y2, *gains_and_weights]
    aliases = {}
    if prev_out is not None:
        in_specs.append(pl.BlockSpec(memory_space=pl.ANY))
        aliases = {len(args): 0}
        args.append(prev_out)
    return pl.pallas_call(
        _combine_kernel,
        grid=(b, s // tc),
        in_specs=in_specs,
        out_specs=pl.BlockSpec((None, tc, d), lambda bi, ti: (b0 + bi, ti, 0)),
        out_shape=jax.ShapeDtypeStruct((b_total, s, d), _f32),
        input_output_aliases=aliases,
        compiler_params=pltpu.CompilerParams(
            dimension_semantics=("arbitrary", "arbitrary"), vmem_limit_bytes=VMEM_LIMIT),
        name="combine_ple",
    )(*args)


def _layer(x, p_i, g_mix, w_in, conv_w, g_ret, w_out_conv, w_out_ret, w_o, g_moe, w_rg, b_rg, w_re, b_re,
           w_exp_gate, w_exp_up, w_exp_down, g_ple_in, w_ple_gate, w_ple_proj, g_ple_post, g_out):
    b_total, s, d = x.shape
    b = b_total // TOKEN_SPLITS
    n_tok = b * s
    operands = _mixer_operands(s, d, g_mix, w_in, conv_w, g_ret, w_out_conv, w_out_ret, w_o, g_moe,
                               w_rg, b_rg, w_re, b_re)
    tail = (g_ple_in.reshape(1, d), w_ple_gate.astype(_bf16), w_ple_proj.astype(_bf16),
            g_ple_post.reshape(1, d), g_out.reshape(1, d))
    nblk = (2 * n_tok + N_EXPERTS * (MOE_BLOCK - 1) + MOE_BLOCK - 1) // MOE_BLOCK
    nblk = (nblk + EXPERT_SPAN - 1) // EXPERT_SPAN * EXPERT_SPAN
    n_rows = nblk * MOE_BLOCK
    win = SC_WINDOW

    def route(eid, rank, cnt):
        counts = cnt[:, 0]
        padded = (counts + MOE_BLOCK - 1) // MOE_BLOCK * MOE_BLOCK
        pends = jnp.cumsum(padded)
        pstarts = pends - padded
        eids = jnp.arange(N_EXPERTS, dtype=jnp.int32).reshape(N_EXPERTS, 1, 1, 1)
        seg = jnp.sum(jnp.where(eid[None] == eids, pstarts.reshape(N_EXPERTS, 1, 1, 1), 0), axis=0)
        dest = rank + seg
        idx3 = dest.reshape(b, 2, s // win, win).transpose(0, 2, 1, 3).reshape(n_tok // win, 2, win)
        return dest, idx3, pstarts // MOE_BLOCK, padded // MOE_BLOCK, counts

    st = [dict() for _ in range(TOKEN_SPLITS)]
    out = None

    def mix(h):
        x1, h2p, eid, ew, rank, cnt = _mixer(x, h * b, b, operands)
        dest, idx3, blk0, nblk_e, counts = route(eid, rank, cnt)
        st[h].update(x1=x1, h2p=h2p, ew=ew, dest=dest, idx3=idx3, seg=(blk0, nblk_e, counts))

    def dispatch(h):
        st[h]["xs"] = _sc_dispatch(st[h]["h2p"].reshape(n_tok, d // 2), st[h]["idx3"], n_rows)

    def experts(h):
        st[h]["y"] = _experts(st[h]["xs"], w_exp_gate, w_exp_up, w_exp_down, *st[h]["seg"])

    def gather(h):
        st[h]["y2"] = _sc_gather(st[h]["y"], st[h]["dest"].reshape(-1)).reshape(b, 2, s, d // 2)

    def combine(h):
        nonlocal out
        out = _combine(st[h]["x1"], p_i, st[h]["y2"], st[h]["ew"], tail, h * b, b_total, out)

    stages = (mix, dispatch, experts, gather, combine)
    for step in range(TOKEN_SPLITS + len(stages) - 1):
        for k in reversed(range(len(stages))):
            if 0 <= step - k < TOKEN_SPLITS:
                stages[k](step - k)
    return out


def kernel(x, p, g_mix, w_in, conv_w, g_ret, w_out_conv, w_out_ret, w_o, g_moe, w_rg, b_rg, w_re, b_re, w_exp_gate, w_exp_up, w_exp_down, g_ple_in, w_ple_gate, w_ple_proj, g_ple_post, g_final):
    depth = p.shape[0]
    assert depth == 1, "the final norm is fused into the single layer's combine kernel"
    return _layer(x, p[0], g_mix[0], w_in[0], conv_w[0], g_ret[0], w_out_conv[0], w_out_ret[0], w_o[0],
                  g_moe[0], w_rg[0], b_rg[0], w_re[0], b_re[0], w_exp_gate[0], w_exp_up[0], w_exp_down[0],
                  g_ple_in[0], w_ple_gate[0], w_ple_proj[0], g_ple_post[0], g_final)
```

```python
import functools

import jax
import jax.numpy as jnp
import numpy as np
from jax import lax
from jax.experimental import pallas as pl
from jax.experimental.pallas import tpu as pltpu
from jax.experimental.pallas import tpu_sc as plsc

EPS = 1e-6
CONV_K = 3
RET_HEADS = 8
RET_DK = 64
RET_DV = 128
RET_CHUNK = 128
ROPE_BASE = 10000.0
N_GROUPS = 4
EXPERTS_PER_GROUP = 8
N_EXPERTS = N_GROUPS * EXPERTS_PER_GROUP
MOE_BLOCK = 256
LANES = 128
SUBLANES = 8
ROUTER_ROWS = 48
MIX_TILE = 512
EXPERT_SPAN = 8
TOKEN_SPLITS = 2
SC_CORES = 2
SC_SUBCORES = 16
SC_WORKERS = SC_CORES * SC_SUBCORES
SC_WINDOW = 128
BF16_BITS = 16
HIGH_HALF = np.uint32(0xFFFF0000)
VMEM_LIMIT = 56 * 1024 * 1024

_bf16 = jnp.bfloat16
_f32 = jnp.float32


def _sigmoid(v):
    return 1.0 / (1.0 + jnp.exp(-v))


def _rms(v, g):
    ms = jnp.mean(v * v, axis=-1, keepdims=True)
    return v * lax.rsqrt(ms + EPS) * g


def _dot(a, b):
    return jnp.dot(a, b, preferred_element_type=_f32)


def _pack_halves(v):
    bits = lax.bitcast_convert_type(v.astype(_bf16).astype(_f32), jnp.uint32)
    c = v.shape[1] // 2
    return (bits[:, :c] >> BF16_BITS) | (bits[:, c:] & HIGH_HALF)


def _unpack_halves(w):
    lo = lax.bitcast_convert_type(w << BF16_BITS, _f32)
    hi = lax.bitcast_convert_type(w & HIGH_HALF, _f32)
    return lo, hi


def _const_spec(shape):
    nd = len(shape)
    return pl.BlockSpec(shape, lambda *_: (0,) * nd, pipeline_mode=pl.Buffered(1))


def _mixer_kernel(x_ref, gmix_ref, win_ref, convw_ref, cos_ref, sin_ref, dmask_ref, qd_ref, kdt_ref,
                  sdec_ref, bmask_ref, gret_ref, woc_ref, wor_ref, wo_ref, gmoe_ref, wrt_ref, rbias_ref,
                  tri_ref,
                  x1_ref, h2p_ref, eid_ref, ew_ref, rank_ref, cnt_ref,
                  hb_ref, qr_ref, kr_ref, vb_ref, o_ref, og_ref, acc_ref, cuc_ref, state_ref, carry_ref):
    ts, d = x_ref.shape
    t = pl.program_id(1)
    n = pl.program_id(0) * pl.num_programs(1) + t
    router_refs = (wrt_ref, rbias_ref, tri_ref, carry_ref, eid_ref, ew_ref, rank_ref, cnt_ref)

    @pl.when(t == 0)
    def _():
        cuc_ref[...] = jnp.zeros_like(cuc_ref)
        state_ref[...] = jnp.zeros_like(state_ref)

    @pl.when(n == 0)
    def _():
        carry_ref[...] = jnp.zeros_like(carry_ref)

    x = x_ref[...]
    hb_ref[...] = _rms(x, gmix_ref[...]).astype(_bf16)
    hb = hb_ref[...]

    def proj(lo, hi):
        return _dot(hb, win_ref[:, lo:hi])

    cu = proj(d, 2 * d) * proj(0, d)
    prev = cuc_ref[...]
    p1 = prev[SUBLANES - 1:SUBLANES, :]
    p2 = prev[SUBLANES - 2:SUBLANES - 1, :]
    rows = lax.broadcasted_iota(jnp.int32, (ts, d), 0)
    s1 = jnp.where(rows == 0, p1, pltpu.roll(cu, 1, 0))
    s2 = jnp.where(rows == 0, p2, jnp.where(rows == 1, p1, pltpu.roll(cu, 2, 0)))
    conv = convw_ref[0:1, :] * s2 + convw_ref[1:2, :] * s1 + convw_ref[2:3, :] * cu
    cuc_ref[...] = cu[ts - SUBLANES:ts, :]
    a = (proj(2 * d, 3 * d) * conv).astype(_bf16)
    acc_ref[...] = _sigmoid(proj(6 * d, 7 * d)) * _dot(a, woc_ref[...])

    qk0 = 3 * d
    cosv = cos_ref[...]
    sinv = sin_ref[...]
    lane = lax.broadcasted_iota(jnp.int32, (ts, LANES), 1)
    first_half = (lane % RET_DK) < (RET_DK // 2)
    for g in range(4):
        for dst, base, scale in ((qr_ref, qk0, None), (kr_ref, qk0 + 4 * LANES, RET_DK ** -0.5)):
            z = proj(base + g * LANES, base + (g + 1) * LANES)
            zs = jnp.where(first_half, pltpu.roll(z, LANES - RET_DK // 2, 1), pltpu.roll(z, RET_DK // 2, 1))
            r = z * cosv + zs * sinv
            if scale is not None:
                r = r * scale
            dst[:, g * LANES:(g + 1) * LANES] = r
    vb_ref[...] = proj(4 * d, 5 * d).astype(_bf16)

    c = RET_CHUNK
    lane_c = lax.broadcasted_iota(jnp.int32, (c, LANES), 1)
    even = lane_c < RET_DK
    for ci in range(ts // c):
        r0 = ci * c
        for j in range(RET_HEADS // 2):
            q2 = qr_ref[r0:r0 + c, j * LANES:(j + 1) * LANES]
            k2 = kr_ref[r0:r0 + c, j * LANES:(j + 1) * LANES]
            v2 = vb_ref[r0:r0 + c, 2 * j * RET_DV:(2 * j + 2) * RET_DV]
            kt = k2.T
            qq = jnp.concatenate([jnp.where(even, q2, 0.0), jnp.where(even, 0.0, q2)], axis=0).astype(_bf16)
            sc = _dot(qq, kt.astype(_bf16))
            pe = (sc[:c] * dmask_ref[2 * j]).astype(_bf16)
            po = (sc[c:] * dmask_ref[2 * j + 1]).astype(_bf16)
            inner = jnp.concatenate([_dot(pe, v2[:, :RET_DV]), _dot(po, v2[:, RET_DV:])], axis=1)
            st = state_ref[j]
            cross = _dot(q2.astype(_bf16), st.astype(_bf16)) * qd_ref[:, 2 * j * RET_DV:(2 * j + 2) * RET_DV]
            o_ref[r0:r0 + c, 2 * j * RET_DV:(2 * j + 2) * RET_DV] = inner + cross
            upd = _dot((kt * kdt_ref[j]).astype(_bf16), v2)
            state_ref[j] = st * sdec_ref[j] + upd * bmask_ref[...]

    for h in range(RET_HEADS):
        sl = slice(h * RET_DV, (h + 1) * RET_DV)
        sg = proj(5 * d + h * RET_DV, 5 * d + (h + 1) * RET_DV)
        on = _rms(o_ref[:, sl], gret_ref[:, sl])
        og_ref[:, sl] = (sg * _sigmoid(sg) * on).astype(_bf16)
    yr = _dot(og_ref[...], wor_ref[...])
    mixed = acc_ref[...] + _sigmoid(proj(7 * d, 8 * d)) * yr
    x1 = x + _dot(mixed.astype(_bf16), wo_ref[...])
    x1_ref[...] = x1

    h2 = _rms(x1, gmoe_ref[...]).astype(_bf16)
    h2p_ref[...] = _pack_halves(h2)
    _router(h2, n, *router_refs)


def _router(h2, tile, wrt_ref, rbias_ref, tri_ref, carry_ref, eid_ref, ew_ref, rank_ref, cnt_ref):
    ts = h2.shape[0]
    lt = lax.dot_general(wrt_ref[...], h2, (((1,), (1,)), ((), ())), preferred_element_type=_f32)
    lt = lt + rbias_ref[...]
    g0, g1, g2, g3 = (lt[i:i + 1, :] for i in range(N_GROUPS))
    gmax = jnp.maximum(jnp.maximum(g0, g1), jnp.maximum(g2, g3))
    grp = jnp.where(g0 == gmax, 0, jnp.where(g1 == gmax, 1, jnp.where(g2 == gmax, 2, 3)))
    gsum = jnp.exp(g0 - gmax) + jnp.exp(g1 - gmax) + jnp.exp(g2 - gmax) + jnp.exp(g3 - gmax)
    g_w = 1.0 / gsum
    e_in = lt[SUBLANES:2 * SUBLANES, :]
    for g in range(1, N_GROUPS):
        e_in = jnp.where(grp == g, lt[(g + 1) * SUBLANES:(g + 2) * SUBLANES, :], e_in)
    ridx = lax.broadcasted_iota(jnp.int32, (EXPERTS_PER_GROUP, ts), 0)
    top1 = jnp.max(e_in, axis=0, keepdims=True)
    i1 = jnp.min(jnp.where(e_in == top1, ridx, EXPERTS_PER_GROUP), axis=0, keepdims=True)
    e_m = jnp.where(ridx == i1, -jnp.inf, e_in)
    top2 = jnp.max(e_m, axis=0, keepdims=True)
    i2 = jnp.min(jnp.where(e_m == top2, ridx, EXPERTS_PER_GROUP), axis=0, keepdims=True)
    ex = jnp.exp(top2 - top1)
    den = 1.0 + ex
    id0 = grp * EXPERTS_PER_GROUP + i1
    id1 = grp * EXPERTS_PER_GROUP + i2
    eid_ref[tile, 0:1, :] = id0
    eid_ref[tile, 1:2, :] = id1
    ew_ref[tile, 0:1, :] = (1.0 / den) * g_w
    ew_ref[tile, 1:2, :] = (ex / den) * g_w

    eidx = lax.broadcasted_iota(jnp.int32, (N_EXPERTS, ts), 0)
    oh0 = (eidx == id0).astype(_f32)
    oh1 = (eidx == id1).astype(_f32)
    cnt = (oh0 + oh1).astype(_bf16)
    before = carry_ref[...] + _dot(cnt, tri_ref[0])
    rank_ref[tile, 0:1, :] = jnp.sum(oh0 * before, axis=0, keepdims=True).astype(jnp.int32)
    rank_ref[tile, 1:2, :] = jnp.sum(oh1 * before, axis=0, keepdims=True).astype(jnp.int32)
    total = carry_ref[...] + _dot(cnt, tri_ref[1])
    carry_ref[...] = total
    cnt_ref[...] = total[:, :LANES].astype(jnp.int32)


def _retention_tables():
    c = RET_CHUNK
    log_gamma = np.log1p(-np.exp2(-5.0 - np.arange(RET_HEADS, dtype=np.float64)))
    pos = np.arange(c, dtype=np.float64)
    diff = pos[:, None] - pos[None, :]
    dmask = np.where((diff >= 0)[None], np.exp(log_gamma[:, None, None] * np.maximum(diff, 0.0)[None]), 0.0)
    q_decay = np.exp(log_gamma[:, None] * (pos[None, :] + 1.0))
    k_decay = np.exp(log_gamma[:, None] * (c - 1.0 - pos[None, :]))
    chunk_decay = np.exp(log_gamma * c)
    qd = np.repeat(q_decay.T, RET_DV, axis=1)
    kdt = np.repeat(k_decay.reshape(RET_HEADS // 2, 2, 1, c), RET_DK, axis=2).reshape(RET_HEADS // 2, 2 * RET_DK, c)
    sdec = np.repeat(chunk_decay.reshape(RET_HEADS // 2, 2, 1), RET_DK, axis=2).reshape(RET_HEADS // 2, 2 * RET_DK, 1)
    sdec = np.broadcast_to(sdec, (RET_HEADS // 2, 2 * RET_DK, 2 * RET_DV))
    rr = np.arange(2 * RET_DK)[:, None] // RET_DK
    cc = np.arange(2 * RET_DV)[None, :] // RET_DV
    bmask = (rr == cc).astype(np.float64)
    return tuple(jnp.asarray(np.ascontiguousarray(v), _f32) for v in (dmask, qd, kdt, sdec, bmask))


def _rope_tables(s_len):
    inv = ROPE_BASE ** (-jnp.arange(0, RET_DK, 2, dtype=_f32) / RET_DK)
    ang = jnp.arange(s_len, dtype=_f32)[:, None] * inv[None, :]
    cos, sin = jnp.cos(ang), jnp.sin(ang)
    cos_t = jnp.tile(cos, (1, LANES // (RET_DK // 2)))
    sin_t = jnp.tile(jnp.concatenate([-sin, sin], axis=1), (1, LANES // RET_DK))
    return cos_t, sin_t


def _mixer_operands(s, d, g_mix, w_in, conv_w, g_ret, w_out_conv, w_out_ret, w_o, g_moe, w_rg, b_rg, w_re, b_re):
    ts = MIX_TILE
    cos_t, sin_t = _rope_tables(s)
    dmask, qd, kdt, sdec, bmask = _retention_tables()
    wrt = jnp.zeros((ROUTER_ROWS, d), _f32)
    wrt = wrt.at[:N_GROUPS].set(w_rg.T).at[SUBLANES:SUBLANES + N_EXPERTS].set(w_re.T).astype(_bf16)
    rb = jnp.zeros((ROUTER_ROWS,), _f32).at[:N_GROUPS].set(b_rg).at[SUBLANES:SUBLANES + N_EXPERTS].set(b_re)
    rbias = jnp.broadcast_to(rb[:, None], (ROUTER_ROWS, ts))
    ii = np.arange(ts)
    tri = jnp.asarray(np.stack([(ii[:, None] < ii[None, :]), np.ones((ts, ts), bool)]), _bf16)
    return (g_mix.reshape(1, d), w_in.astype(_bf16), conv_w, cos_t, sin_t, dmask, qd, kdt, sdec, bmask,
            g_ret.reshape(1, d), w_out_conv.astype(_bf16), w_out_ret.astype(_bf16), w_o.astype(_bf16),
            g_moe.reshape(1, d), wrt, rbias, tri)


def _mixer(x, b0, b, operands):
    _, s, d = x.shape
    ts = MIX_TILE
    nt = s // ts
    tile3 = lambda w: pl.BlockSpec((None, ts, w), lambda bi, ti: (bi, ti, 0))
    route = lambda: pl.BlockSpec((b * nt, 2, ts), lambda bi, ti: (0, 0, 0))
    rope_spec = pl.BlockSpec((ts, LANES), lambda bi, ti: (ti, 0))
    in_specs = [pl.BlockSpec((None, ts, d), lambda bi, ti: (b0 + bi, ti, 0))]
    in_specs += [rope_spec if i in (3, 4) else _const_spec(op.shape) for i, op in enumerate(operands)]
    out_shape = [
        jax.ShapeDtypeStruct((b, s, d), _f32),
        jax.ShapeDtypeStruct((b, s, d // 2), jnp.uint32),
        jax.ShapeDtypeStruct((b * nt, 2, ts), jnp.int32),
        jax.ShapeDtypeStruct((b * nt, 2, ts), _f32),
        jax.ShapeDtypeStruct((b * nt, 2, ts), jnp.int32),
        jax.ShapeDtypeStruct((N_EXPERTS, LANES), jnp.int32),
    ]
    out_specs = [tile3(d), tile3(d // 2), route(), route(), route(),
                 pl.BlockSpec((N_EXPERTS, LANES), lambda bi, ti: (0, 0))]
    scratch = [
        pltpu.VMEM((ts, d), _bf16),
        pltpu.VMEM((ts, RET_HEADS * RET_DK), _f32),
        pltpu.VMEM((ts, RET_HEADS * RET_DK), _f32),
        pltpu.VMEM((ts, RET_HEADS * RET_DV), _bf16),
        pltpu.VMEM((ts, RET_HEADS * RET_DV), _f32),
        pltpu.VMEM((ts, RET_HEADS * RET_DV), _bf16),
        pltpu.VMEM((ts, d), _f32),
        pltpu.VMEM((SUBLANES, d), _f32),
        pltpu.VMEM((RET_HEADS // 2, 2 * RET_DK, 2 * RET_DV), _f32),
        pltpu.VMEM((N_EXPERTS, ts), _f32),
    ]
    return pl.pallas_call(
        _mixer_kernel,
        grid=(b, nt),
        in_specs=in_specs,
        out_specs=out_specs,
        out_shape=out_shape,
        scratch_shapes=scratch,
        compiler_params=pltpu.CompilerParams(
            dimension_semantics=("arbitrary", "arbitrary"), vmem_limit_bytes=VMEM_LIMIT),
        name="mixer_router",
    )(x, *operands)


def _sc_worker_id():
    return lax.axis_index("s") * SC_CORES + lax.axis_index("c")


def _sc_mesh():
    return plsc.VectorSubcoreMesh(core_axis_name="c", subcore_axis_name="s")


def _sc_dispatch(src, idx3, n_rows):
    t, d = src.shape
    n_win_total, _, win = idx3.shape
    n_win = n_win_total // SC_WORKERS

    @functools.partial(
        pl.kernel, mesh=_sc_mesh(),
        out_type=jax.ShapeDtypeStruct((n_rows, d), src.dtype),
        scratch_types=[pltpu.VMEM((2, win), jnp.int32), pltpu.VMEM((win, d), src.dtype)],
    )
    def dispatch(src_hbm, idx_hbm, out_hbm, idx_v, rows_v):
        wid = _sc_worker_id()

        @pl.loop(0, n_win)
        def _(i):
            w = wid * n_win + i
            off = pl.multiple_of(w * win, SUBLANES)
            pltpu.sync_copy(idx_hbm.at[w], idx_v)
            pltpu.sync_copy(src_hbm.at[pl.ds(off, win)], rows_v)
            pltpu.sync_copy(rows_v, out_hbm.at[idx_v.at[0]])
            pltpu.sync_copy(rows_v, out_hbm.at[idx_v.at[1]])

    return dispatch(src, idx3)


def _sc_gather(table, idx):
    n = idx.shape[0]
    d = table.shape[1]
    win = SC_WINDOW
    per_w = n // SC_WORKERS
    n_win = per_w // win

    @functools.partial(
        pl.kernel, mesh=_sc_mesh(),
        out_type=jax.ShapeDtypeStruct((n, d), table.dtype),
        scratch_types=[pltpu.VMEM((win,), jnp.int32), pltpu.VMEM((win, d), table.dtype),
                       pltpu.SemaphoreType.DMA],
    )
    def gather(table_hbm, idx_hbm, out_hbm, idx_v, rows_v, sem):
        base = _sc_worker_id() * per_w

        @pl.loop(0, n_win)
        def _(i):
            off = pl.multiple_of(base + i * win, SUBLANES)
            pltpu.sync_copy(idx_hbm.at[pl.ds(off, win)], idx_v)
            pltpu.async_copy(table_hbm.at[idx_v], rows_v, sem).wait()
            pltpu.sync_copy(rows_v, out_hbm.at[pl.ds(off, win)])

    return gather(table, idx)


def _expert_kernel(blk0_ref, nblk_ref, count_ref, xs_ref, wg_ref, wu_ref, wd_ref, y_ref,
                   xbuf, ybuf, wgb_ref, wub_ref, wdb_ref, sem_in, sem_out):
    e = pl.program_id(0)
    last = pl.num_programs(0) - 1
    blk0 = blk0_ref[e]
    nblk = nblk_ref[e]
    count = count_ref[e]
    total = blk0_ref[last] + nblk_ref[last]
    n_span = (total + EXPERT_SPAN - 1) // EXPERT_SPAN
    span_rows = EXPERT_SPAN * MOE_BLOCK
    half = xbuf.shape[2]

    def span_in(s):
        return pltpu.make_async_copy(xs_ref.at[pl.ds(s * span_rows, span_rows)], xbuf.at[s % 2], sem_in.at[s % 2])

    def span_out(s):
        return pltpu.make_async_copy(ybuf.at[s % 2], y_ref.at[pl.ds(s * span_rows, span_rows)], sem_out.at[s % 2])

    @pl.when(e == 0)
    def _():
        ybuf[...] = jnp.zeros_like(ybuf)

    @pl.when(jnp.logical_and(e == 0, n_span > 0))
    def _():
        span_in(0).start()

    wgb_ref[...] = wg_ref[...].astype(_bf16)
    wub_ref[...] = wu_ref[...].astype(_bf16)
    wdb_ref[...] = wd_ref[...].astype(_bf16)

    def block(j, carry):
        g = blk0 + j
        s = g // EXPERT_SPAN
        k = g % EXPERT_SPAN
        slot = s % 2
        row0 = pl.multiple_of(k * MOE_BLOCK, MOE_BLOCK)

        @pl.when(k == 0)
        def _():
            span_in(s).wait()

            @pl.when(s + 1 < n_span)
            def _():
                span_in(s + 1).start()

            @pl.when(s >= 2)
            def _():
                span_out(s - 2).wait()

        rowid = lax.broadcasted_iota(jnp.int32, (MOE_BLOCK, half), 0)
        xb = xbuf[slot, pl.ds(row0, MOE_BLOCK), :]
        lo, hi = _unpack_halves(jnp.where(rowid < count - j * MOE_BLOCK, xb, jnp.uint32(0)))
        lo = lo.astype(_bf16)
        hi = hi.astype(_bf16)
        gate = _dot(lo, wgb_ref[:half, :]) + _dot(hi, wgb_ref[half:, :])
        up = _dot(lo, wub_ref[:half, :]) + _dot(hi, wub_ref[half:, :])
        hid = (gate * _sigmoid(gate) * up).astype(_bf16)
        ybuf[slot, pl.ds(row0, MOE_BLOCK), :] = _pack_halves(_dot(hid, wdb_ref[...]))

        @pl.when(jnp.logical_or(k == EXPERT_SPAN - 1, g == total - 1))
        def _():
            span_out(s).start()
        return carry

    lax.fori_loop(0, nblk, block, 0)

    @pl.when(jnp.logical_and(e == last, n_span >= 2))
    def _():
        span_out(n_span - 2).wait()

    @pl.when(jnp.logical_and(e == last, n_span >= 1))
    def _():
        span_out(n_span - 1).wait()


def _experts(xs, w_gate, w_up, w_down, blk0, nblk, counts):
    n_rows, half = xs.shape
    d = 2 * half
    ne, _, de = w_gate.shape
    grid_spec = pltpu.PrefetchScalarGridSpec(
        num_scalar_prefetch=3,
        grid=(ne,),
        in_specs=[
            pl.BlockSpec(memory_space=pl.ANY),
            pl.BlockSpec((None, d, de), lambda e, *_: (e, 0, 0)),
            pl.BlockSpec((None, d, de), lambda e, *_: (e, 0, 0)),
            pl.BlockSpec((None, de, d), lambda e, *_: (e, 0, 0)),
        ],
        out_specs=pl.BlockSpec(memory_space=pl.ANY),
        scratch_shapes=[
            pltpu.VMEM((2, EXPERT_SPAN * MOE_BLOCK, half), jnp.uint32),
            pltpu.VMEM((2, EXPERT_SPAN * MOE_BLOCK, half), jnp.uint32),
            pltpu.VMEM((d, de), _bf16), pltpu.VMEM((d, de), _bf16), pltpu.VMEM((de, d), _bf16),
            pltpu.SemaphoreType.DMA((2,)), pltpu.SemaphoreType.DMA((2,)),
        ],
    )
    return pl.pallas_call(
        _expert_kernel,
        grid_spec=grid_spec,
        out_shape=jax.ShapeDtypeStruct((n_rows, half), jnp.uint32),
        compiler_params=pltpu.CompilerParams(dimension_semantics=("arbitrary",), vmem_limit_bytes=VMEM_LIMIT),
        name="experts",
    )(blk0, nblk, counts, xs, w_gate, w_up, w_down)


def _combine_kernel(ew_ref, x1_ref, p_ref, y2_ref, gin_ref, wpg_ref, wpp_ref, gpost_ref, gfin_ref, *rest):
    out_ref = rest[-1]
    tc, d = x1_ref.shape
    x2 = x1_ref[...]
    for k in range(2):
        wcol = jnp.broadcast_to(ew_ref[k:k + 1, :], (LANES, tc)).T
        yk = jnp.concatenate(_unpack_halves(y2_ref[k]), axis=1)
        x2 = x2 + jnp.tile(wcol, (1, d // LANES)) * yk
    gate = _sigmoid(_dot(_rms(x2, gin_ref[...]).astype(_bf16), wpg_ref[...]))
    ple = _rms(_dot(p_ref[...].astype(_bf16), wpp_ref[...]), gpost_ref[...])
    out_ref[...] = _rms(x2 + gate * ple, gfin_ref[...])


def _combine(x1, p0, y2, ew, gains_and_weights, b0, b_total, prev_out):
    b, s, d = x1.shape
    tc = MIX_TILE
    nt = s // tc
    pdim = p0.shape[-1]
    vec = lambda: pl.BlockSpec((1, d), lambda bi, ti: (0, 0))
    in_specs = [
        pl.BlockSpec((None, 2, tc), lambda bi, ti: (bi * nt + ti, 0, 0)),
        pl.BlockSpec((None, tc, d), lambda bi, ti: (bi, ti, 0)),
        pl.BlockSpec((None, tc, pdim), lambda bi, ti: (b0 + bi, ti, 0)),
        pl.BlockSpec((None, 2, tc, d // 2), lambda bi, ti: (bi * nt + ti, 0, 0, 0)),
        vec(),
        pl.BlockSpec((d, d), lambda bi, ti: (0, 0)),
        pl.BlockSpec((pdim, d), lambda bi, ti: (0, 0)),
        vec(), vec(),
    ]
    args = [ew, x1, p0, y2, *gains_and_weights]
    aliases = {}
    if prev_out is not None:
        in_specs.append(pl.BlockSpec(memory_space=pl.ANY))
        aliases = {len(args): 0}
        args.append(prev_out)
    return pl.pallas_call(
        _combine_kernel,
        grid=(b, s // tc),
        in_specs=in_specs,
        out_specs=pl.BlockSpec((None, tc, d), lambda bi, ti: (b0 + bi, ti, 0)),
        out_shape=jax.ShapeDtypeStruct((b_total, s, d), _f32),
        input_output_aliases=aliases,
        compiler_params=pltpu.CompilerParams(
            dimension_semantics=("arbitrary", "arbitrary"), vmem_limit_bytes=VMEM_LIMIT),
        name="combine_ple",
    )(*args)


def _layer(x, p_i, g_mix, w_in, conv_w, g_ret, w_out_conv, w_out_ret, w_o, g_moe, w_rg, b_rg, w_re, b_re,
           w_exp_gate, w_exp_up, w_exp_down, g_ple_in, w_ple_gate, w_ple_proj, g_ple_post, g_out):
    b_total, s, d = x.shape
    b = b_total // TOKEN_SPLITS
    n_tok = b * s
    operands = _mixer_operands(s, d, g_mix, w_in, conv_w, g_ret, w_out_conv, w_out_ret, w_o, g_moe,
                               w_rg, b_rg, w_re, b_re)
    tail = (g_ple_in.reshape(1, d), w_ple_gate.astype(_bf16), w_ple_proj.astype(_bf16),
            g_ple_post.reshape(1, d), g_out.reshape(1, d))
    nblk = (2 * n_tok + N_EXPERTS * (MOE_BLOCK - 1) + MOE_BLOCK - 1) // MOE_BLOCK
    nblk = (nblk + EXPERT_SPAN - 1) // EXPERT_SPAN * EXPERT_SPAN
    n_rows = nblk * MOE_BLOCK
    win = SC_WINDOW

    def route(eid, rank, cnt):
        counts = cnt[:, 0]
        padded = (counts + MOE_BLOCK - 1) // MOE_BLOCK * MOE_BLOCK
        pends = jnp.cumsum(padded)
        pstarts = pends - padded
        eids = jnp.arange(N_EXPERTS, dtype=jnp.int32).reshape(N_EXPERTS, 1, 1, 1)
        seg = jnp.sum(jnp.where(eid[None] == eids, pstarts.reshape(N_EXPERTS, 1, 1, 1), 0), axis=0)
        dest = rank + seg
        tiles, _, ts = dest.shape
        idx3 = dest.reshape(tiles, 2, ts // win, win).transpose(0, 2, 1, 3).reshape(n_tok // win, 2, win)
        return dest, idx3, pstarts // MOE_BLOCK, padded // MOE_BLOCK, counts

    st = [dict() for _ in range(TOKEN_SPLITS)]
    out = None

    def mix(h):
        x1, h2p, eid, ew, rank, cnt = _mixer(x, h * b, b, operands)
        dest, idx3, blk0, nblk_e, counts = route(eid, rank, cnt)
        st[h].update(x1=x1, h2p=h2p, ew=ew, dest=dest, idx3=idx3, seg=(blk0, nblk_e, counts))

    def dispatch(h):
        st[h]["xs"] = _sc_dispatch(st[h]["h2p"].reshape(n_tok, d // 2), st[h]["idx3"], n_rows)

    def experts(h):
        st[h]["y"] = _experts(st[h]["xs"], w_exp_gate, w_exp_up, w_exp_down, *st[h]["seg"])

    def gather(h):
        y2 = _sc_gather(st[h]["y"], st[h]["dest"].reshape(-1))
        st[h]["y2"] = y2.reshape(n_tok // MIX_TILE, 2, MIX_TILE, d // 2)

    def combine(h):
        nonlocal out
        out = _combine(st[h]["x1"], p_i, st[h]["y2"], st[h]["ew"], tail, h * b, b_total, out)

    stages = (mix, dispatch, experts, gather, combine)
    for step in range(TOKEN_SPLITS + len(stages) - 1):
        for k in reversed(range(len(stages))):
            if 0 <= step - k < TOKEN_SPLITS:
                stages[k](step - k)
    return out


def kernel(x, p, g_mix, w_in, conv_w, g_ret, w_out_conv, w_out_ret, w_o, g_moe, w_rg, b_rg, w_re, b_re, w_exp_gate, w_exp_up, w_exp_down, g_ple_in, w_ple_gate, w_ple_proj, g_ple_post, g_final):
    depth = p.shape[0]
    assert depth == 1, "the final norm is fused into the single layer's combine kernel"
    return _layer(x, p[0], g_mix[0], w_in[0], conv_w[0], g_ret[0], w_out_conv[0], w_out_ret[0], w_o[0],
                  g_moe[0], w_rg[0], b_rg[0], w_re[0], b_re[0], w_exp_gate[0], w_exp_up[0], w_exp_down[0],
                  g_ple_in[0], w_ple_gate[0], w_ple_proj[0], g_ple_post[0], g_final)
```

```python
import functools

import jax
import jax.numpy as jnp
import numpy as np
from jax import lax
from jax.experimental import pallas as pl
from jax.experimental.pallas import tpu as pltpu
from jax.experimental.pallas import tpu_sc as plsc

EPS = 1e-6
CONV_K = 3
RET_HEADS = 8
RET_DK = 64
RET_DV = 128
RET_CHUNK = 128
ROPE_BASE = 10000.0
N_GROUPS = 4
EXPERTS_PER_GROUP = 8
N_EXPERTS = N_GROUPS * EXPERTS_PER_GROUP
MOE_BLOCK = 256
LANES = 128
SUBLANES = 8
ROUTER_ROWS = 48
MIX_TILE = 512
EXPERT_SPAN = 8
TOKEN_SPLITS = 2
SC_CORES = 2
SC_SUBCORES = 16
SC_WORKERS = SC_CORES * SC_SUBCORES
SC_WINDOW = 128
BF16_BITS = 16
HIGH_HALF = np.uint32(0xFFFF0000)
VMEM_LIMIT = 56 * 1024 * 1024

_bf16 = jnp.bfloat16
_f32 = jnp.float32


def _sigmoid(v):
    return 1.0 / (1.0 + jnp.exp(-v))


def _rms(v, g):
    ms = jnp.mean(v * v, axis=-1, keepdims=True)
    return v * lax.rsqrt(ms + EPS) * g


def _dot(a, b):
    return jnp.dot(a, b, preferred_element_type=_f32)


def _pack_halves(v):
    bits = lax.bitcast_convert_type(v.astype(_bf16).astype(_f32), jnp.uint32)
    c = v.shape[1] // 2
    return (bits[:, :c] >> BF16_BITS) | (bits[:, c:] & HIGH_HALF)


def _unpack_halves(w):
    lo = lax.bitcast_convert_type(w << BF16_BITS, _f32)
    hi = lax.bitcast_convert_type(w & HIGH_HALF, _f32)
    return lo, hi


def _const_spec(shape):
    nd = len(shape)
    return pl.BlockSpec(shape, lambda *_: (0,) * nd, pipeline_mode=pl.Buffered(1))


def _mixer_kernel(x_ref, gmix_ref, win_ref, convw_ref, cos_ref, sin_ref, dmask_ref, qd_ref, kdt_ref,
                  sdec_ref, bmask_ref, gret_ref, woc_ref, wor_ref, wo_ref, gmoe_ref, wrt_ref, rbias_ref,
                  tri_ref,
                  x1_ref, h2p_ref, eid_ref, ew_ref, rank_ref, cnt_ref,
                  hb_ref, qr_ref, kr_ref, vb_ref, o_ref, og_ref, acc_ref, cuc_ref, state_ref, carry_ref):
    ts, d = x_ref.shape
    t = pl.program_id(1)
    n = pl.program_id(0) * pl.num_programs(1) + t
    router_refs = (wrt_ref, rbias_ref, tri_ref, carry_ref, eid_ref, ew_ref, rank_ref, cnt_ref)

    @pl.when(t == 0)
    def _():
        cuc_ref[...] = jnp.zeros_like(cuc_ref)
        state_ref[...] = jnp.zeros_like(state_ref)

    @pl.when(n == 0)
    def _():
        carry_ref[...] = jnp.zeros_like(carry_ref)

    x = x_ref[...]
    hb_ref[...] = _rms(x, gmix_ref[...]).astype(_bf16)
    hb = hb_ref[...]

    def proj(lo, hi):
        return _dot(hb, win_ref[:, lo:hi])

    cu = proj(d, 2 * d) * proj(0, d)
    prev = cuc_ref[...]
    p1 = prev[SUBLANES - 1:SUBLANES, :]
    p2 = prev[SUBLANES - 2:SUBLANES - 1, :]
    rows = lax.broadcasted_iota(jnp.int32, (ts, d), 0)
    s1 = jnp.where(rows == 0, p1, pltpu.roll(cu, 1, 0))
    s2 = jnp.where(rows == 0, p2, jnp.where(rows == 1, p1, pltpu.roll(cu, 2, 0)))
    conv = convw_ref[0:1, :] * s2 + convw_ref[1:2, :] * s1 + convw_ref[2:3, :] * cu
    cuc_ref[...] = cu[ts - SUBLANES:ts, :]
    a = (proj(2 * d, 3 * d) * conv).astype(_bf16)
    acc_ref[...] = _sigmoid(proj(6 * d, 7 * d)) * _dot(a, woc_ref[...])

    qk0 = 3 * d
    cosv = cos_ref[...]
    sinv = sin_ref[...]
    lane = lax.broadcasted_iota(jnp.int32, (ts, LANES), 1)
    first_half = (lane % RET_DK) < (RET_DK // 2)
    qk_w = RET_HEADS * RET_DK
    for dst, base, scale in ((qr_ref, qk0, None), (kr_ref, qk0 + qk_w, RET_DK ** -0.5)):
        zw = proj(base, base + qk_w)
        for g in range(qk_w // LANES):
            z = zw[:, g * LANES:(g + 1) * LANES]
            zs = jnp.where(first_half, pltpu.roll(z, LANES - RET_DK // 2, 1), pltpu.roll(z, RET_DK // 2, 1))
            r = z * cosv + zs * sinv
            if scale is not None:
                r = r * scale
            dst[:, g * LANES:(g + 1) * LANES] = r
    vb_ref[...] = proj(4 * d, 5 * d).astype(_bf16)

    c = RET_CHUNK
    lane_c = lax.broadcasted_iota(jnp.int32, (c, LANES), 1)
    even = lane_c < RET_DK
    for ci in range(ts // c):
        r0 = ci * c
        for j in range(RET_HEADS // 2):
            q2 = qr_ref[r0:r0 + c, j * LANES:(j + 1) * LANES]
            k2 = kr_ref[r0:r0 + c, j * LANES:(j + 1) * LANES]
            v2 = vb_ref[r0:r0 + c, 2 * j * RET_DV:(2 * j + 2) * RET_DV]
            kt = k2.T
            qq = jnp.concatenate([jnp.where(even, q2, 0.0), jnp.where(even, 0.0, q2)], axis=0).astype(_bf16)
            sc = _dot(qq, kt.astype(_bf16))
            pe = (sc[:c] * dmask_ref[2 * j]).astype(_bf16)
            po = (sc[c:] * dmask_ref[2 * j + 1]).astype(_bf16)
            inner = jnp.concatenate([_dot(pe, v2[:, :RET_DV]), _dot(po, v2[:, RET_DV:])], axis=1)
            st = state_ref[j]
            cross = _dot(q2.astype(_bf16), st.astype(_bf16)) * qd_ref[:, 2 * j * RET_DV:(2 * j + 2) * RET_DV]
            o_ref[r0:r0 + c, 2 * j * RET_DV:(2 * j + 2) * RET_DV] = inner + cross
            upd = _dot((kt * kdt_ref[j]).astype(_bf16), v2)
            state_ref[j] = st * sdec_ref[j] + upd * bmask_ref[...]

    for hp in range(RET_HEADS // 2):
        sg2 = proj(5 * d + 2 * hp * RET_DV, 5 * d + (2 * hp + 2) * RET_DV)
        for h in (2 * hp, 2 * hp + 1):
            sl = slice(h * RET_DV, (h + 1) * RET_DV)
            sg = sg2[:, (h - 2 * hp) * RET_DV:(h - 2 * hp + 1) * RET_DV]
            on = _rms(o_ref[:, sl], gret_ref[:, sl])
            og_ref[:, sl] = (sg * _sigmoid(sg) * on).astype(_bf16)
    yr = _dot(og_ref[...], wor_ref[...])
    mixed = acc_ref[...] + _sigmoid(proj(7 * d, 8 * d)) * yr
    x1 = x + _dot(mixed.astype(_bf16), wo_ref[...])
    x1_ref[...] = x1

    h2 = _rms(x1, gmoe_ref[...]).astype(_bf16)
    h2p_ref[...] = _pack_halves(h2)
    _router(h2, n, *router_refs)


def _router(h2, tile, wrt_ref, rbias_ref, tri_ref, carry_ref, eid_ref, ew_ref, rank_ref, cnt_ref):
    ts = h2.shape[0]
    lt = lax.dot_general(wrt_ref[...], h2, (((1,), (1,)), ((), ())), preferred_element_type=_f32)
    lt = lt + rbias_ref[...]
    g0, g1, g2, g3 = (lt[i:i + 1, :] for i in range(N_GROUPS))
    gmax = jnp.maximum(jnp.maximum(g0, g1), jnp.maximum(g2, g3))
    grp = jnp.where(g0 == gmax, 0, jnp.where(g1 == gmax, 1, jnp.where(g2 == gmax, 2, 3)))
    gsum = jnp.exp(g0 - gmax) + jnp.exp(g1 - gmax) + jnp.exp(g2 - gmax) + jnp.exp(g3 - gmax)
    g_w = 1.0 / gsum
    e_in = lt[SUBLANES:2 * SUBLANES, :]
    for g in range(1, N_GROUPS):
        e_in = jnp.where(grp == g, lt[(g + 1) * SUBLANES:(g + 2) * SUBLANES, :], e_in)
    ridx = lax.broadcasted_iota(jnp.int32, (EXPERTS_PER_GROUP, ts), 0)
    top1 = jnp.max(e_in, axis=0, keepdims=True)
    i1 = jnp.min(jnp.where(e_in == top1, ridx, EXPERTS_PER_GROUP), axis=0, keepdims=True)
    e_m = jnp.where(ridx == i1, -jnp.inf, e_in)
    top2 = jnp.max(e_m, axis=0, keepdims=True)
    i2 = jnp.min(jnp.where(e_m == top2, ridx, EXPERTS_PER_GROUP), axis=0, keepdims=True)
    ex = jnp.exp(top2 - top1)
    den = 1.0 + ex
    id0 = grp * EXPERTS_PER_GROUP + i1
    id1 = grp * EXPERTS_PER_GROUP + i2
    eid_ref[tile, 0:1, :] = id0
    eid_ref[tile, 1:2, :] = id1
    ew_ref[tile, 0:1, :] = (1.0 / den) * g_w
    ew_ref[tile, 1:2, :] = (ex / den) * g_w

    eidx = lax.broadcasted_iota(jnp.int32, (N_EXPERTS, ts), 0)
    oh0 = (eidx == id0).astype(_f32)
    oh1 = (eidx == id1).astype(_f32)
    cnt = (oh0 + oh1).astype(_bf16)
    before = carry_ref[...] + _dot(cnt, tri_ref[0])
    rank_ref[tile, 0:1, :] = jnp.sum(oh0 * before, axis=0, keepdims=True).astype(jnp.int32)
    rank_ref[tile, 1:2, :] = jnp.sum(oh1 * before, axis=0, keepdims=True).astype(jnp.int32)
    total = carry_ref[...] + _dot(cnt, tri_ref[1])
    carry_ref[...] = total
    cnt_ref[...] = total[:, :LANES].astype(jnp.int32)


def _retention_tables():
    c = RET_CHUNK
    log_gamma = np.log1p(-np.exp2(-5.0 - np.arange(RET_HEADS, dtype=np.float64)))
    pos = np.arange(c, dtype=np.float64)
    diff = pos[:, None] - pos[None, :]
    dmask = np.where((diff >= 0)[None], np.exp(log_gamma[:, None, None] * np.maximum(diff, 0.0)[None]), 0.0)
    q_decay = np.exp(log_gamma[:, None] * (pos[None, :] + 1.0))
    k_decay = np.exp(log_gamma[:, None] * (c - 1.0 - pos[None, :]))
    chunk_decay = np.exp(log_gamma * c)
    qd = np.repeat(q_decay.T, RET_DV, axis=1)
    kdt = np.repeat(k_decay.reshape(RET_HEADS // 2, 2, 1, c), RET_DK, axis=2).reshape(RET_HEADS // 2, 2 * RET_DK, c)
    sdec = np.repeat(chunk_decay.reshape(RET_HEADS // 2, 2, 1), RET_DK, axis=2).reshape(RET_HEADS // 2, 2 * RET_DK, 1)
    sdec = np.broadcast_to(sdec, (RET_HEADS // 2, 2 * RET_DK, 2 * RET_DV))
    rr = np.arange(2 * RET_DK)[:, None] // RET_DK
    cc = np.arange(2 * RET_DV)[None, :] // RET_DV
    bmask = (rr == cc).astype(np.float64)
    return tuple(jnp.asarray(np.ascontiguousarray(v), _f32) for v in (dmask, qd, kdt, sdec, bmask))


def _rope_tables(s_len):
    inv = ROPE_BASE ** (-jnp.arange(0, RET_DK, 2, dtype=_f32) / RET_DK)
    ang = jnp.arange(s_len, dtype=_f32)[:, None] * inv[None, :]
    cos, sin = jnp.cos(ang), jnp.sin(ang)
    cos_t = jnp.tile(cos, (1, LANES // (RET_DK // 2)))
    sin_t = jnp.tile(jnp.concatenate([-sin, sin], axis=1), (1, LANES // RET_DK))
    return cos_t, sin_t


def _mixer_operands(s, d, g_mix, w_in, conv_w, g_ret, w_out_conv, w_out_ret, w_o, g_moe, w_rg, b_rg, w_re, b_re):
    ts = MIX_TILE
    cos_t, sin_t = _rope_tables(s)
    dmask, qd, kdt, sdec, bmask = _retention_tables()
    wrt = jnp.zeros((ROUTER_ROWS, d), _f32)
    wrt = wrt.at[:N_GROUPS].set(w_rg.T).at[SUBLANES:SUBLANES + N_EXPERTS].set(w_re.T).astype(_bf16)
    rb = jnp.zeros((ROUTER_ROWS,), _f32).at[:N_GROUPS].set(b_rg).at[SUBLANES:SUBLANES + N_EXPERTS].set(b_re)
    rbias = jnp.broadcast_to(rb[:, None], (ROUTER_ROWS, ts))
    ii = np.arange(ts)
    tri = jnp.asarray(np.stack([(ii[:, None] < ii[None, :]), np.ones((ts, ts), bool)]), _bf16)
    return (g_mix.reshape(1, d), w_in.astype(_bf16), conv_w, cos_t, sin_t, dmask, qd, kdt, sdec, bmask,
            g_ret.reshape(1, d), w_out_conv.astype(_bf16), w_out_ret.astype(_bf16), w_o.astype(_bf16),
            g_moe.reshape(1, d), wrt, rbias, tri)


def _mixer(x, b0, b, operands):
    _, s, d = x.shape
    ts = MIX_TILE
    nt = s // ts
    tile3 = lambda w: pl.BlockSpec((None, ts, w), lambda bi, ti: (bi, ti, 0))
    route = lambda: pl.BlockSpec((b * nt, 2, ts), lambda bi, ti: (0, 0, 0))
    rope_spec = pl.BlockSpec((ts, LANES), lambda bi, ti: (ti, 0))
    in_specs = [pl.BlockSpec((None, ts, d), lambda bi, ti: (b0 + bi, ti, 0))]
    in_specs += [rope_spec if i in (3, 4) else _const_spec(op.shape) for i, op in enumerate(operands)]
    out_shape = [
        jax.ShapeDtypeStruct((b, s, d), _f32),
        jax.ShapeDtypeStruct((b, s, d // 2), jnp.uint32),
        jax.ShapeDtypeStruct((b * nt, 2, ts), jnp.int32),
        jax.ShapeDtypeStruct((b * nt, 2, ts), _f32),
        jax.ShapeDtypeStruct((b * nt, 2, ts), jnp.int32),
        jax.ShapeDtypeStruct((N_EXPERTS, LANES), jnp.int32),
    ]
    out_specs = [tile3(d), tile3(d // 2), route(), route(), route(),
                 pl.BlockSpec((N_EXPERTS, LANES), lambda bi, ti: (0, 0))]
    scratch = [
        pltpu.VMEM((ts, d), _bf16),
        pltpu.VMEM((ts, RET_HEADS * RET_DK), _f32),
        pltpu.VMEM((ts, RET_HEADS * RET_DK), _f32),
        pltpu.VMEM((ts, RET_HEADS * RET_DV), _bf16),
        pltpu.VMEM((ts, RET_HEADS * RET_DV), _f32),
        pltpu.VMEM((ts, RET_HEADS * RET_DV), _bf16),
        pltpu.VMEM((ts, d), _f32),
        pltpu.VMEM((SUBLANES, d), _f32),
        pltpu.VMEM((RET_HEADS // 2, 2 * RET_DK, 2 * RET_DV), _f32),
        pltpu.VMEM((N_EXPERTS, ts), _f32),
    ]
    return pl.pallas_call(
        _mixer_kernel,
        grid=(b, nt),
        in_specs=in_specs,
        out_specs=out_specs,
        out_shape=out_shape,
        scratch_shapes=scratch,
        compiler_params=pltpu.CompilerParams(
            dimension_semantics=("arbitrary", "arbitrary"), vmem_limit_bytes=VMEM_LIMIT),
        name="mixer_router",
    )(x, *operands)


def _sc_worker_id():
    return lax.axis_index("s") * SC_CORES + lax.axis_index("c")


def _sc_mesh():
    return plsc.VectorSubcoreMesh(core_axis_name="c", subcore_axis_name="s")


def _sc_dispatch(src, idx3, n_rows):
    t, d = src.shape
    n_win_total, _, win = idx3.shape
    n_win = n_win_total // SC_WORKERS

    @functools.partial(
        pl.kernel, mesh=_sc_mesh(),
        out_type=jax.ShapeDtypeStruct((n_rows, d), src.dtype),
        scratch_types=[pltpu.VMEM((2, win), jnp.int32), pltpu.VMEM((win, d), src.dtype)],
    )
    def dispatch(src_hbm, idx_hbm, out_hbm, idx_v, rows_v):
        wid = _sc_worker_id()

        @pl.loop(0, n_win)
        def _(i):
            w = wid * n_win + i
            off = pl.multiple_of(w * win, SUBLANES)
            pltpu.sync_copy(idx_hbm.at[w], idx_v)
            pltpu.sync_copy(src_hbm.at[pl.ds(off, win)], rows_v)
            pltpu.sync_copy(rows_v, out_hbm.at[idx_v.at[0]])
            pltpu.sync_copy(rows_v, out_hbm.at[idx_v.at[1]])

    return dispatch(src, idx3)


def _sc_gather(table, idx):
    n = idx.shape[0]
    d = table.shape[1]
    win = SC_WINDOW
    per_w = n // SC_WORKERS
    n_win = per_w // win

    @functools.partial(
        pl.kernel, mesh=_sc_mesh(),
        out_type=jax.ShapeDtypeStruct((n, d), table.dtype),
        scratch_types=[pltpu.VMEM((win,), jnp.int32), pltpu.VMEM((win, d), table.dtype),
                       pltpu.SemaphoreType.DMA],
    )
    def gather(table_hbm, idx_hbm, out_hbm, idx_v, rows_v, sem):
        base = _sc_worker_id() * per_w

        @pl.loop(0, n_win)
        def _(i):
            off = pl.multiple_of(base + i * win, SUBLANES)
            pltpu.sync_copy(idx_hbm.at[pl.ds(off, win)], idx_v)
            pltpu.async_copy(table_hbm.at[idx_v], rows_v, sem).wait()
            pltpu.sync_copy(rows_v, out_hbm.at[pl.ds(off, win)])

    return gather(table, idx)


def _expert_kernel(blk0_ref, nblk_ref, count_ref, xs_ref, wg_ref, wu_ref, wd_ref, y_ref,
                   xbuf, ybuf, wgb_ref, wub_ref, wdb_ref, sem_in, sem_out):
    e = pl.program_id(0)
    last = pl.num_programs(0) - 1
    blk0 = blk0_ref[e]
    nblk = nblk_ref[e]
    count = count_ref[e]
    total = blk0_ref[last] + nblk_ref[last]
    n_span = (total + EXPERT_SPAN - 1) // EXPERT_SPAN
    span_rows = EXPERT_SPAN * MOE_BLOCK
    half = xbuf.shape[2]

    def span_in(s):
        return pltpu.make_async_copy(xs_ref.at[pl.ds(s * span_rows, span_rows)], xbuf.at[s % 2], sem_in.at[s % 2])

    def span_out(s):
        return pltpu.make_async_copy(ybuf.at[s % 2], y_ref.at[pl.ds(s * span_rows, span_rows)], sem_out.at[s % 2])

    @pl.when(e == 0)
    def _():
        ybuf[...] = jnp.zeros_like(ybuf)

    @pl.when(jnp.logical_and(e == 0, n_span > 0))
    def _():
        span_in(0).start()

    wgb_ref[...] = wg_ref[...].astype(_bf16)
    wub_ref[...] = wu_ref[...].astype(_bf16)
    wdb_ref[...] = wd_ref[...].astype(_bf16)

    def block(j, carry):
        g = blk0 + j
        s = g // EXPERT_SPAN
        k = g % EXPERT_SPAN
        slot = s % 2
        row0 = pl.multiple_of(k * MOE_BLOCK, MOE_BLOCK)

        @pl.when(k == 0)
        def _():
            span_in(s).wait()

            @pl.when(s + 1 < n_span)
            def _():
                span_in(s + 1).start()

            @pl.when(s >= 2)
            def _():
                span_out(s - 2).wait()

        rowid = lax.broadcasted_iota(jnp.int32, (MOE_BLOCK, half), 0)
        xb = xbuf[slot, pl.ds(row0, MOE_BLOCK), :]
        lo, hi = _unpack_halves(jnp.where(rowid < count - j * MOE_BLOCK, xb, jnp.uint32(0)))
        lo = lo.astype(_bf16)
        hi = hi.astype(_bf16)
        gate = _dot(lo, wgb_ref[:half, :]) + _dot(hi, wgb_ref[half:, :])
        up = _dot(lo, wub_ref[:half, :]) + _dot(hi, wub_ref[half:, :])
        hid = (gate * _sigmoid(gate) * up).astype(_bf16)
        ybuf[slot, pl.ds(row0, MOE_BLOCK), :] = _pack_halves(_dot(hid, wdb_ref[...]))

        @pl.when(jnp.logical_or(k == EXPERT_SPAN - 1, g == total - 1))
        def _():
            span_out(s).start()
        return carry

    lax.fori_loop(0, nblk, block, 0)

    @pl.when(jnp.logical_and(e == last, n_span >= 2))
    def _():
        span_out(n_span - 2).wait()

    @pl.when(jnp.logical_and(e == last, n_span >= 1))
    def _():
        span_out(n_span - 1).wait()


def _experts(xs, w_gate, w_up, w_down, blk0, nblk, counts):
    n_rows, half = xs.shape
    d = 2 * half
    ne, _, de = w_gate.shape
    grid_spec = pltpu.PrefetchScalarGridSpec(
        num_scalar_prefetch=3,
        grid=(ne,),
        in_specs=[
            pl.BlockSpec(memory_space=pl.ANY),
            pl.BlockSpec((None, d, de), lambda e, *_: (e, 0, 0)),
            pl.BlockSpec((None, d, de), lambda e, *_: (e, 0, 0)),
            pl.BlockSpec((None, de, d), lambda e, *_: (e, 0, 0)),
        ],
        out_specs=pl.BlockSpec(memory_space=pl.ANY),
        scratch_shapes=[
            pltpu.VMEM((2, EXPERT_SPAN * MOE_BLOCK, half), jnp.uint32),
            pltpu.VMEM((2, EXPERT_SPAN * MOE_BLOCK, half), jnp.uint32),
            pltpu.VMEM((d, de), _bf16), pltpu.VMEM((d, de), _bf16), pltpu.VMEM((de, d), _bf16),
            pltpu.SemaphoreType.DMA((2,)), pltpu.SemaphoreType.DMA((2,)),
        ],
    )
    return pl.pallas_call(
        _expert_kernel,
        grid_spec=grid_spec,
        out_shape=jax.ShapeDtypeStruct((n_rows, half), jnp.uint32),
        compiler_params=pltpu.CompilerParams(dimension_semantics=("arbitrary",), vmem_limit_bytes=VMEM_LIMIT),
        name="experts",
    )(blk0, nblk, counts, xs, w_gate, w_up, w_down)


def _combine_kernel(ew_ref, x1_ref, p_ref, y2_ref, gin_ref, wpg_ref, wpp_ref, gpost_ref, gfin_ref, *rest):
    out_ref = rest[-1]
    tc, d = x1_ref.shape
    x2 = x1_ref[...]
    for k in range(2):
        wcol = jnp.broadcast_to(ew_ref[k:k + 1, :], (LANES, tc)).T
        yk = jnp.concatenate(_unpack_halves(y2_ref[k]), axis=1)
        x2 = x2 + jnp.tile(wcol, (1, d // LANES)) * yk
    gate = _sigmoid(_dot(_rms(x2, gin_ref[...]).astype(_bf16), wpg_ref[...]))
    ple = _rms(_dot(p_ref[...].astype(_bf16), wpp_ref[...]), gpost_ref[...])
    out_ref[...] = _rms(x2 + gate * ple, gfin_ref[...])


def _combine(x1, p0, y2, ew, gains_and_weights, b0, b_total, prev_out):
    b, s, d = x1.shape
    tc = MIX_TILE
    nt = s // tc
    pdim = p0.shape[-1]
    vec = lambda: pl.BlockSpec((1, d), lambda bi, ti: (0, 0))
    in_specs = [
        pl.BlockSpec((None, 2, tc), lambda bi, ti: (bi * nt + ti, 0, 0)),
        pl.BlockSpec((None, tc, d), lambda bi, ti: (bi, ti, 0)),
        pl.BlockSpec((None, tc, pdim), lambda bi, ti: (b0 + bi, ti, 0)),
        pl.BlockSpec((None, 2, tc, d // 2), lambda bi, ti: (bi * nt + ti, 0, 0, 0)),
        vec(),
        pl.BlockSpec((d, d), lambda bi, ti: (0, 0)),
        pl.BlockSpec((pdim, d), lambda bi, ti: (0, 0)),
        vec(), vec(),
    ]
    args = [ew, x1, p0, y2, *gains_and_weights]
    aliases = {}
    if prev_out is not None:
        in_specs.append(pl.BlockSpec(memory_space=pl.ANY))
        aliases = {len(args): 0}
        args.append(prev_out)
    return pl.pallas_call(
        _combine_kernel,
        grid=(b, s // tc),
        in_specs=in_specs,
        out_specs=pl.BlockSpec((None, tc, d), lambda bi, ti: (b0 + bi, ti, 0)),
        out_shape=jax.ShapeDtypeStruct((b_total, s, d), _f32),
        input_output_aliases=aliases,
        compiler_params=pltpu.CompilerParams(
            dimension_semantics=("arbitrary", "arbitrary"), vmem_limit_bytes=VMEM_LIMIT),
        name="combine_ple",
    )(*args)


def _layer(x, p_i, g_mix, w_in, conv_w, g_ret, w_out_conv, w_out_ret, w_o, g_moe, w_rg, b_rg, w_re, b_re,
           w_exp_gate, w_exp_up, w_exp_down, g_ple_in, w_ple_gate, w_ple_proj, g_ple_post, g_out):
    b_total, s, d = x.shape
    b = b_total // TOKEN_SPLITS
    n_tok = b * s
    operands = _mixer_operands(s, d, g_mix, w_in, conv_w, g_ret, w_out_conv, w_out_ret, w_o, g_moe,
                               w_rg, b_rg, w_re, b_re)
    tail = (g_ple_in.reshape(1, d), w_ple_gate.astype(_bf16), w_ple_proj.astype(_bf16),
            g_ple_post.reshape(1, d), g_out.reshape(1, d))
    nblk = (2 * n_tok + N_EXPERTS * (MOE_BLOCK - 1) + MOE_BLOCK - 1) // MOE_BLOCK
    nblk = (nblk + EXPERT_SPAN - 1) // EXPERT_SPAN * EXPERT_SPAN
    n_rows = nblk * MOE_BLOCK
    win = SC_WINDOW

    def route(eid, rank, cnt):
        counts = cnt[:, 0]
        padded = (counts + MOE_BLOCK - 1) // MOE_BLOCK * MOE_BLOCK
        pends = jnp.cumsum(padded)
        pstarts = pends - padded
        eids = jnp.arange(N_EXPERTS, dtype=jnp.int32).reshape(N_EXPERTS, 1, 1, 1)
        seg = jnp.sum(jnp.where(eid[None] == eids, pstarts.reshape(N_EXPERTS, 1, 1, 1), 0), axis=0)
        dest = rank + seg
        tiles, _, ts = dest.shape
        idx3 = dest.reshape(tiles, 2, ts // win, win).transpose(0, 2, 1, 3).reshape(n_tok // win, 2, win)
        return dest, idx3, pstarts // MOE_BLOCK, padded // MOE_BLOCK, counts

    st = [dict() for _ in range(TOKEN_SPLITS)]
    out = None

    def mix(h):
        x1, h2p, eid, ew, rank, cnt = _mixer(x, h * b, b, operands)
        dest, idx3, blk0, nblk_e, counts = route(eid, rank, cnt)
        st[h].update(x1=x1, h2p=h2p, ew=ew, dest=dest, idx3=idx3, seg=(blk0, nblk_e, counts))

    def dispatch(h):
        st[h]["xs"] = _sc_dispatch(st[h]["h2p"].reshape(n_tok, d // 2), st[h]["idx3"], n_rows)

    def experts(h):
        st[h]["y"] = _experts(st[h]["xs"], w_exp_gate, w_exp_up, w_exp_down, *st[h]["seg"])

    def gather(h):
        y2 = _sc_gather(st[h]["y"], st[h]["dest"].reshape(-1))
        st[h]["y2"] = y2.reshape(n_tok // MIX_TILE, 2, MIX_TILE, d // 2)

    def combine(h):
        nonlocal out
        out = _combine(st[h]["x1"], p_i, st[h]["y2"], st[h]["ew"], tail, h * b, b_total, out)

    stages = (mix, dispatch, experts, gather, combine)
    for step in range(TOKEN_SPLITS + len(stages) - 1):
        for k in reversed(range(len(stages))):
            if 0 <= step - k < TOKEN_SPLITS:
                stages[k](step - k)
    return out


def kernel(x, p, g_mix, w_in, conv_w, g_ret, w_out_conv, w_out_ret, w_o, g_moe, w_rg, b_rg, w_re, b_re, w_exp_gate, w_exp_up, w_exp_down, g_ple_in, w_ple_gate, w_ple_proj, g_ple_post, g_final):
    depth = p.shape[0]
    assert depth == 1, "the final norm is fused into the single layer's combine kernel"
    return _layer(x, p[0], g_mix[0], w_in[0], conv_w[0], g_ret[0], w_out_conv[0], w_out_ret[0], w_o[0],
                  g_moe[0], w_rg[0], b_rg[0], w_re[0], b_re[0], w_exp_gate[0], w_exp_up[0], w_exp_down[0],
                  g_ple_in[0], w_ple_gate[0], w_ple_proj[0], g_ple_post[0], g_final)
```

```python
import functools

import jax
import jax.numpy as jnp
import numpy as np
from jax import lax
from jax.experimental import pallas as pl
from jax.experimental.pallas import tpu as pltpu
from jax.experimental.pallas import tpu_sc as plsc

EPS = 1e-6
CONV_K = 3
RET_HEADS = 8
RET_DK = 64
RET_DV = 128
RET_CHUNK = 128
ROPE_BASE = 10000.0
N_GROUPS = 4
EXPERTS_PER_GROUP = 8
N_EXPERTS = N_GROUPS * EXPERTS_PER_GROUP
MOE_BLOCK = 256
LANES = 128
SUBLANES = 8
ROUTER_ROWS = 48
MIX_TILE = 512
EXPERT_SPAN = 8
TOKEN_SPLITS = 2
SC_CORES = 2
SC_SUBCORES = 16
SC_WORKERS = SC_CORES * SC_SUBCORES
SC_WINDOW = 128
BF16_BITS = 16
HIGH_HALF = np.uint32(0xFFFF0000)
VMEM_LIMIT = 56 * 1024 * 1024

_bf16 = jnp.bfloat16
_f32 = jnp.float32


def _sigmoid(v):
    return 0.5 * jnp.tanh(0.5 * v) + 0.5


def _rms(v, g):
    ms = jnp.mean(v * v, axis=-1, keepdims=True)
    return v * lax.rsqrt(ms + EPS) * g


def _dot(a, b):
    return jnp.dot(a, b, preferred_element_type=_f32)


def _pack_halves(v):
    bits = lax.bitcast_convert_type(v.astype(_bf16).astype(_f32), jnp.uint32)
    c = v.shape[1] // 2
    return (bits[:, :c] >> BF16_BITS) | (bits[:, c:] & HIGH_HALF)


def _unpack_halves(w):
    lo = lax.bitcast_convert_type(w << BF16_BITS, _f32)
    hi = lax.bitcast_convert_type(w & HIGH_HALF, _f32)
    return lo, hi


def _const_spec(shape):
    nd = len(shape)
    return pl.BlockSpec(shape, lambda *_: (0,) * nd, pipeline_mode=pl.Buffered(1))


def _mixer_kernel(x_ref, gmix_ref, win_ref, convw_ref, cos_ref, sin_ref, dmask_ref, qd_ref, kdt_ref,
                  sdec_ref, bmask_ref, gret_ref, woc_ref, wor_ref, wo_ref, gmoe_ref, wrt_ref, rbias_ref,
                  tri_ref,
                  x1_ref, h2p_ref, eid_ref, ew_ref, rank_ref, cnt_ref,
                  hb_ref, qr_ref, kr_ref, vb_ref, o_ref, og_ref, acc_ref, cuc_ref, state_ref, carry_ref):
    ts, d = x_ref.shape
    t = pl.program_id(1)
    n = pl.program_id(0) * pl.num_programs(1) + t
    router_refs = (wrt_ref, rbias_ref, tri_ref, carry_ref, eid_ref, ew_ref, rank_ref, cnt_ref)

    @pl.when(t == 0)
    def _():
        cuc_ref[...] = jnp.zeros_like(cuc_ref)
        state_ref[...] = jnp.zeros_like(state_ref)

    @pl.when(n == 0)
    def _():
        carry_ref[...] = jnp.zeros_like(carry_ref)

    x = x_ref[...]
    hb_ref[...] = _rms(x, gmix_ref[...]).astype(_bf16)
    hb = hb_ref[...]

    def proj(lo, hi):
        return _dot(hb, win_ref[:, lo:hi])

    cu = proj(d, 2 * d) * proj(0, d)
    prev = cuc_ref[...]
    p1 = prev[SUBLANES - 1:SUBLANES, :]
    p2 = prev[SUBLANES - 2:SUBLANES - 1, :]
    rows = lax.broadcasted_iota(jnp.int32, (ts, d), 0)
    s1 = jnp.where(rows == 0, p1, pltpu.roll(cu, 1, 0))
    s2 = jnp.where(rows == 0, p2, jnp.where(rows == 1, p1, pltpu.roll(cu, 2, 0)))
    conv = convw_ref[0:1, :] * s2 + convw_ref[1:2, :] * s1 + convw_ref[2:3, :] * cu
    cuc_ref[...] = cu[ts - SUBLANES:ts, :]
    a = (proj(2 * d, 3 * d) * conv).astype(_bf16)
    acc_ref[...] = _sigmoid(proj(6 * d, 7 * d)) * _dot(a, woc_ref[...])

    qk0 = 3 * d
    cosv = cos_ref[...]
    sinv = sin_ref[...]
    lane = lax.broadcasted_iota(jnp.int32, (ts, LANES), 1)
    first_half = (lane % RET_DK) < (RET_DK // 2)
    qk_w = RET_HEADS * RET_DK
    for dst, base, scale in ((qr_ref, qk0, None), (kr_ref, qk0 + qk_w, RET_DK ** -0.5)):
        zw = proj(base, base + qk_w)
        for g in range(qk_w // LANES):
            z = zw[:, g * LANES:(g + 1) * LANES]
            zs = jnp.where(first_half, pltpu.roll(z, LANES - RET_DK // 2, 1), pltpu.roll(z, RET_DK // 2, 1))
            r = z * cosv + zs * sinv
            if scale is not None:
                r = r * scale
            dst[:, g * LANES:(g + 1) * LANES] = r
    vb_ref[...] = proj(4 * d, 5 * d).astype(_bf16)

    c = RET_CHUNK
    lane_c = lax.broadcasted_iota(jnp.int32, (c, LANES), 1)
    even = lane_c < RET_DK
    for ci in range(ts // c):
        r0 = ci * c
        for j in range(RET_HEADS // 2):
            q2 = qr_ref[r0:r0 + c, j * LANES:(j + 1) * LANES]
            k2 = kr_ref[r0:r0 + c, j * LANES:(j + 1) * LANES]
            v2 = vb_ref[r0:r0 + c, 2 * j * RET_DV:(2 * j + 2) * RET_DV]
            kt = k2.T
            qq = jnp.concatenate([jnp.where(even, q2, 0.0), jnp.where(even, 0.0, q2)], axis=0).astype(_bf16)
            sc = _dot(qq, kt.astype(_bf16))
            pe = (sc[:c] * dmask_ref[2 * j]).astype(_bf16)
            po = (sc[c:] * dmask_ref[2 * j + 1]).astype(_bf16)
            inner = jnp.concatenate([_dot(pe, v2[:, :RET_DV]), _dot(po, v2[:, RET_DV:])], axis=1)
            st = state_ref[j]
            cross = _dot(q2.astype(_bf16), st.astype(_bf16)) * qd_ref[:, 2 * j * RET_DV:(2 * j + 2) * RET_DV]
            o_ref[r0:r0 + c, 2 * j * RET_DV:(2 * j + 2) * RET_DV] = inner + cross
            upd = _dot((kt * kdt_ref[j]).astype(_bf16), v2)
            state_ref[j] = st * sdec_ref[j] + upd * bmask_ref[...]

    for hp in range(RET_HEADS // 2):
        sg2 = proj(5 * d + 2 * hp * RET_DV, 5 * d + (2 * hp + 2) * RET_DV)
        for h in (2 * hp, 2 * hp + 1):
            sl = slice(h * RET_DV, (h + 1) * RET_DV)
            sg = sg2[:, (h - 2 * hp) * RET_DV:(h - 2 * hp + 1) * RET_DV]
            on = _rms(o_ref[:, sl], gret_ref[:, sl])
            og_ref[:, sl] = (sg * _sigmoid(sg) * on).astype(_bf16)
    yr = _dot(og_ref[...], wor_ref[...])
    mixed = acc_ref[...] + _sigmoid(proj(7 * d, 8 * d)) * yr
    x1 = x + _dot(mixed.astype(_bf16), wo_ref[...])
    x1_ref[...] = x1

    h2 = _rms(x1, gmoe_ref[...]).astype(_bf16)
    h2p_ref[...] = _pack_halves(h2)
    _router(h2, n, *router_refs)


def _router(h2, tile, wrt_ref, rbias_ref, tri_ref, carry_ref, eid_ref, ew_ref, rank_ref, cnt_ref):
    ts = h2.shape[0]
    lt = lax.dot_general(wrt_ref[...], h2, (((1,), (1,)), ((), ())), preferred_element_type=_f32)
    lt = lt + rbias_ref[...]
    g0, g1, g2, g3 = (lt[i:i + 1, :] for i in range(N_GROUPS))
    gmax = jnp.maximum(jnp.maximum(g0, g1), jnp.maximum(g2, g3))
    grp = jnp.where(g0 == gmax, 0, jnp.where(g1 == gmax, 1, jnp.where(g2 == gmax, 2, 3)))
    gsum = jnp.exp(g0 - gmax) + jnp.exp(g1 - gmax) + jnp.exp(g2 - gmax) + jnp.exp(g3 - gmax)
    g_w = 1.0 / gsum
    e_in = lt[SUBLANES:2 * SUBLANES, :]
    for g in range(1, N_GROUPS):
        e_in = jnp.where(grp == g, lt[(g + 1) * SUBLANES:(g + 2) * SUBLANES, :], e_in)
    ridx = lax.broadcasted_iota(jnp.int32, (EXPERTS_PER_GROUP, ts), 0)
    top1 = jnp.max(e_in, axis=0, keepdims=True)
    i1 = jnp.min(jnp.where(e_in == top1, ridx, EXPERTS_PER_GROUP), axis=0, keepdims=True)
    e_m = jnp.where(ridx == i1, -jnp.inf, e_in)
    top2 = jnp.max(e_m, axis=0, keepdims=True)
    i2 = jnp.min(jnp.where(e_m == top2, ridx, EXPERTS_PER_GROUP), axis=0, keepdims=True)
    ex = jnp.exp(top2 - top1)
    den = 1.0 + ex
    id0 = grp * EXPERTS_PER_GROUP + i1
    id1 = grp * EXPERTS_PER_GROUP + i2
    eid_ref[tile, 0:1, :] = id0
    eid_ref[tile, 1:2, :] = id1
    ew_ref[tile, 0:1, :] = (1.0 / den) * g_w
    ew_ref[tile, 1:2, :] = (ex / den) * g_w

    eidx = lax.broadcasted_iota(jnp.int32, (N_EXPERTS, ts), 0)
    oh0 = (eidx == id0).astype(_f32)
    oh1 = (eidx == id1).astype(_f32)
    cnt = (oh0 + oh1).astype(_bf16)
    before = carry_ref[...] + _dot(cnt, tri_ref[0])
    rank_ref[tile, 0:1, :] = jnp.sum(oh0 * before, axis=0, keepdims=True).astype(jnp.int32)
    rank_ref[tile, 1:2, :] = jnp.sum(oh1 * before, axis=0, keepdims=True).astype(jnp.int32)
    total = carry_ref[...] + _dot(cnt, tri_ref[1])
    carry_ref[...] = total
    cnt_ref[...] = total[:, :LANES].astype(jnp.int32)


def _retention_tables():
    c = RET_CHUNK
    log_gamma = np.log1p(-np.exp2(-5.0 - np.arange(RET_HEADS, dtype=np.float64)))
    pos = np.arange(c, dtype=np.float64)
    diff = pos[:, None] - pos[None, :]
    dmask = np.where((diff >= 0)[None], np.exp(log_gamma[:, None, None] * np.maximum(diff, 0.0)[None]), 0.0)
    q_decay = np.exp(log_gamma[:, None] * (pos[None, :] + 1.0))
    k_decay = np.exp(log_gamma[:, None] * (c - 1.0 - pos[None, :]))
    chunk_decay = np.exp(log_gamma * c)
    qd = np.repeat(q_decay.T, RET_DV, axis=1)
    kdt = np.repeat(k_decay.reshape(RET_HEADS // 2, 2, 1, c), RET_DK, axis=2).reshape(RET_HEADS // 2, 2 * RET_DK, c)
    sdec = np.repeat(chunk_decay.reshape(RET_HEADS // 2, 2, 1), RET_DK, axis=2).reshape(RET_HEADS // 2, 2 * RET_DK, 1)
    sdec = np.broadcast_to(sdec, (RET_HEADS // 2, 2 * RET_DK, 2 * RET_DV))
    rr = np.arange(2 * RET_DK)[:, None] // RET_DK
    cc = np.arange(2 * RET_DV)[None, :] // RET_DV
    bmask = (rr == cc).astype(np.float64)
    return tuple(jnp.asarray(np.ascontiguousarray(v), _f32) for v in (dmask, qd, kdt, sdec, bmask))


def _rope_tables(s_len):
    inv = ROPE_BASE ** (-jnp.arange(0, RET_DK, 2, dtype=_f32) / RET_DK)
    ang = jnp.arange(s_len, dtype=_f32)[:, None] * inv[None, :]
    cos, sin = jnp.cos(ang), jnp.sin(ang)
    cos_t = jnp.tile(cos, (1, LANES // (RET_DK // 2)))
    sin_t = jnp.tile(jnp.concatenate([-sin, sin], axis=1), (1, LANES // RET_DK))
    return cos_t, sin_t


def _mixer_operands(s, d, g_mix, w_in, conv_w, g_ret, w_out_conv, w_out_ret, w_o, g_moe, w_rg, b_rg, w_re, b_re):
    ts = MIX_TILE
    cos_t, sin_t = _rope_tables(s)
    dmask, qd, kdt, sdec, bmask = _retention_tables()
    wrt = jnp.zeros((ROUTER_ROWS, d), _f32)
    wrt = wrt.at[:N_GROUPS].set(w_rg.T).at[SUBLANES:SUBLANES + N_EXPERTS].set(w_re.T).astype(_bf16)
    rb = jnp.zeros((ROUTER_ROWS,), _f32).at[:N_GROUPS].set(b_rg).at[SUBLANES:SUBLANES + N_EXPERTS].set(b_re)
    rbias = jnp.broadcast_to(rb[:, None], (ROUTER_ROWS, ts))
    ii = np.arange(ts)
    tri = jnp.asarray(np.stack([(ii[:, None] < ii[None, :]), np.ones((ts, ts), bool)]), _bf16)
    return (g_mix.reshape(1, d), w_in.astype(_bf16), conv_w, cos_t, sin_t, dmask, qd, kdt, sdec, bmask,
            g_ret.reshape(1, d), w_out_conv.astype(_bf16), w_out_ret.astype(_bf16), w_o.astype(_bf16),
            g_moe.reshape(1, d), wrt, rbias, tri)


def _mixer(x, b0, b, operands):
    _, s, d = x.shape
    ts = MIX_TILE
    nt = s // ts
    tile3 = lambda w: pl.BlockSpec((None, ts, w), lambda bi, ti: (bi, ti, 0))
    route = lambda: pl.BlockSpec((b * nt, 2, ts), lambda bi, ti: (0, 0, 0))
    rope_spec = pl.BlockSpec((ts, LANES), lambda bi, ti: (ti, 0))
    in_specs = [pl.BlockSpec((None, ts, d), lambda bi, ti: (b0 + bi, ti, 0))]
    in_specs += [rope_spec if i in (3, 4) else _const_spec(op.shape) for i, op in enumerate(operands)]
    out_shape = [
        jax.ShapeDtypeStruct((b, s, d), _f32),
        jax.ShapeDtypeStruct((b, s, d // 2), jnp.uint32),
        jax.ShapeDtypeStruct((b * nt, 2, ts), jnp.int32),
        jax.ShapeDtypeStruct((b * nt, 2, ts), _f32),
        jax.ShapeDtypeStruct((b * nt, 2, ts), jnp.int32),
        jax.ShapeDtypeStruct((N_EXPERTS, LANES), jnp.int32),
    ]
    out_specs = [tile3(d), tile3(d // 2), route(), route(), route(),
                 pl.BlockSpec((N_EXPERTS, LANES), lambda bi, ti: (0, 0))]
    scratch = [
        pltpu.VMEM((ts, d), _bf16),
        pltpu.VMEM((ts, RET_HEADS * RET_DK), _f32),
        pltpu.VMEM((ts, RET_HEADS * RET_DK), _f32),
        pltpu.VMEM((ts, RET_HEADS * RET_DV), _bf16),
        pltpu.VMEM((ts, RET_HEADS * RET_DV), _f32),
        pltpu.VMEM((ts, RET_HEADS * RET_DV), _bf16),
        pltpu.VMEM((ts, d), _f32),
        pltpu.VMEM((SUBLANES, d), _f32),
        pltpu.VMEM((RET_HEADS // 2, 2 * RET_DK, 2 * RET_DV), _f32),
        pltpu.VMEM((N_EXPERTS, ts), _f32),
    ]
    return pl.pallas_call(
        _mixer_kernel,
        grid=(b, nt),
        in_specs=in_specs,
        out_specs=out_specs,
        out_shape=out_shape,
        scratch_shapes=scratch,
        compiler_params=pltpu.CompilerParams(
            dimension_semantics=("arbitrary", "arbitrary"), vmem_limit_bytes=VMEM_LIMIT),
        name="mixer_router",
    )(x, *operands)


def _sc_worker_id():
    return lax.axis_index("s") * SC_CORES + lax.axis_index("c")


def _sc_mesh():
    return plsc.VectorSubcoreMesh(core_axis_name="c", subcore_axis_name="s")


def _sc_dispatch(src, idx3, n_rows):
    t, d = src.shape
    n_win_total, _, win = idx3.shape
    n_win = n_win_total // SC_WORKERS

    @functools.partial(
        pl.kernel, mesh=_sc_mesh(),
        out_type=jax.ShapeDtypeStruct((n_rows, d), src.dtype),
        scratch_types=[pltpu.VMEM((2, win), jnp.int32), pltpu.VMEM((win, d), src.dtype)],
    )
    def dispatch(src_hbm, idx_hbm, out_hbm, idx_v, rows_v):
        wid = _sc_worker_id()

        @pl.loop(0, n_win)
        def _(i):
            w = wid * n_win + i
            off = pl.multiple_of(w * win, SUBLANES)
            pltpu.sync_copy(idx_hbm.at[w], idx_v)
            pltpu.sync_copy(src_hbm.at[pl.ds(off, win)], rows_v)
            pltpu.sync_copy(rows_v, out_hbm.at[idx_v.at[0]])
            pltpu.sync_copy(rows_v, out_hbm.at[idx_v.at[1]])

    return dispatch(src, idx3)


def _sc_gather(table, idx):
    n = idx.shape[0]
    d = table.shape[1]
    win = SC_WINDOW
    per_w = n // SC_WORKERS
    n_win = per_w // win

    @functools.partial(
        pl.kernel, mesh=_sc_mesh(),
        out_type=jax.ShapeDtypeStruct((n, d), table.dtype),
        scratch_types=[pltpu.VMEM((win,), jnp.int32), pltpu.VMEM((win, d), table.dtype),
                       pltpu.SemaphoreType.DMA],
    )
    def gather(table_hbm, idx_hbm, out_hbm, idx_v, rows_v, sem):
        base = _sc_worker_id() * per_w

        @pl.loop(0, n_win)
        def _(i):
            off = pl.multiple_of(base + i * win, SUBLANES)
            pltpu.sync_copy(idx_hbm.at[pl.ds(off, win)], idx_v)
            pltpu.async_copy(table_hbm.at[idx_v], rows_v, sem).wait()
            pltpu.sync_copy(rows_v, out_hbm.at[pl.ds(off, win)])

    return gather(table, idx)


def _expert_kernel(blk0_ref, nblk_ref, count_ref, xs_ref, wg_ref, wu_ref, wd_ref, y_ref,
                   xbuf, ybuf, wgb_ref, wub_ref, wdb_ref, sem_in, sem_out):
    e = pl.program_id(0)
    last = pl.num_programs(0) - 1
    blk0 = blk0_ref[e]
    nblk = nblk_ref[e]
    count = count_ref[e]
    total = blk0_ref[last] + nblk_ref[last]
    n_span = (total + EXPERT_SPAN - 1) // EXPERT_SPAN
    span_rows = EXPERT_SPAN * MOE_BLOCK
    half = xbuf.shape[2]

    def span_in(s):
        return pltpu.make_async_copy(xs_ref.at[pl.ds(s * span_rows, span_rows)], xbuf.at[s % 2], sem_in.at[s % 2])

    def span_out(s):
        return pltpu.make_async_copy(ybuf.at[s % 2], y_ref.at[pl.ds(s * span_rows, span_rows)], sem_out.at[s % 2])

    @pl.when(e == 0)
    def _():
        ybuf[...] = jnp.zeros_like(ybuf)

    @pl.when(jnp.logical_and(e == 0, n_span > 0))
    def _():
        span_in(0).start()

    wgb_ref[...] = wg_ref[...].astype(_bf16)
    wub_ref[...] = wu_ref[...].astype(_bf16)
    wdb_ref[...] = wd_ref[...].astype(_bf16)

    def block(j, carry):
        g = blk0 + j
        s = g // EXPERT_SPAN
        k = g % EXPERT_SPAN
        slot = s % 2
        row0 = pl.multiple_of(k * MOE_BLOCK, MOE_BLOCK)

        @pl.when(k == 0)
        def _():
            span_in(s).wait()

            @pl.when(s + 1 < n_span)
            def _():
                span_in(s + 1).start()

            @pl.when(s >= 2)
            def _():
                span_out(s - 2).wait()

        rowid = lax.broadcasted_iota(jnp.int32, (MOE_BLOCK, half), 0)
        xb = xbuf[slot, pl.ds(row0, MOE_BLOCK), :]
        lo, hi = _unpack_halves(jnp.where(rowid < count - j * MOE_BLOCK, xb, jnp.uint32(0)))
        lo = lo.astype(_bf16)
        hi = hi.astype(_bf16)
        gate = _dot(lo, wgb_ref[:half, :]) + _dot(hi, wgb_ref[half:, :])
        up = _dot(lo, wub_ref[:half, :]) + _dot(hi, wub_ref[half:, :])
        hid = (gate * _sigmoid(gate) * up).astype(_bf16)
        ybuf[slot, pl.ds(row0, MOE_BLOCK), :] = _pack_halves(_dot(hid, wdb_ref[...]))

        @pl.when(jnp.logical_or(k == EXPERT_SPAN - 1, g == total - 1))
        def _():
            span_out(s).start()
        return carry

    lax.fori_loop(0, nblk, block, 0)

    @pl.when(jnp.logical_and(e == last, n_span >= 2))
    def _():
        span_out(n_span - 2).wait()

    @pl.when(jnp.logical_and(e == last, n_span >= 1))
    def _():
        span_out(n_span - 1).wait()


def _experts(xs, w_gate, w_up, w_down, blk0, nblk, counts):
    n_rows, half = xs.shape
    d = 2 * half
    ne, _, de = w_gate.shape
    grid_spec = pltpu.PrefetchScalarGridSpec(
        num_scalar_prefetch=3,
        grid=(ne,),
        in_specs=[
            pl.BlockSpec(memory_space=pl.ANY),
            pl.BlockSpec((None, d, de), lambda e, *_: (e, 0, 0)),
            pl.BlockSpec((None, d, de), lambda e, *_: (e, 0, 0)),
            pl.BlockSpec((None, de, d), lambda e, *_: (e, 0, 0)),
        ],
        out_specs=pl.BlockSpec(memory_space=pl.ANY),
        scratch_shapes=[
            pltpu.VMEM((2, EXPERT_SPAN * MOE_BLOCK, half), jnp.uint32),
            pltpu.VMEM((2, EXPERT_SPAN * MOE_BLOCK, half), jnp.uint32),
            pltpu.VMEM((d, de), _bf16), pltpu.VMEM((d, de), _bf16), pltpu.VMEM((de, d), _bf16),
            pltpu.SemaphoreType.DMA((2,)), pltpu.SemaphoreType.DMA((2,)),
        ],
    )
    return pl.pallas_call(
        _expert_kernel,
        grid_spec=grid_spec,
        out_shape=jax.ShapeDtypeStruct((n_rows, half), jnp.uint32),
        compiler_params=pltpu.CompilerParams(dimension_semantics=("arbitrary",), vmem_limit_bytes=VMEM_LIMIT),
        name="experts",
    )(blk0, nblk, counts, xs, w_gate, w_up, w_down)


def _combine_kernel(ew_ref, x1_ref, p_ref, y2_ref, gin_ref, wpg_ref, wpp_ref, gpost_ref, gfin_ref, *rest):
    out_ref = rest[-1]
    tc, d = x1_ref.shape
    x2 = x1_ref[...]
    for k in range(2):
        wcol = jnp.broadcast_to(ew_ref[k:k + 1, :], (LANES, tc)).T
        yk = jnp.concatenate(_unpack_halves(y2_ref[k]), axis=1)
        x2 = x2 + jnp.tile(wcol, (1, d // LANES)) * yk
    gate = _sigmoid(_dot(_rms(x2, gin_ref[...]).astype(_bf16), wpg_ref[...]))
    ple = _rms(_dot(p_ref[...].astype(_bf16), wpp_ref[...]), gpost_ref[...])
    out_ref[...] = _rms(x2 + gate * ple, gfin_ref[...])


def _combine(x1, p0, y2, ew, gains_and_weights, b0, b_total, prev_out):
    b, s, d = x1.shape
    tc = MIX_TILE
    nt = s // tc
    pdim = p0.shape[-1]
    vec = lambda: pl.BlockSpec((1, d), lambda bi, ti: (0, 0))
    in_specs = [
        pl.BlockSpec((None, 2, tc), lambda bi, ti: (bi * nt + ti, 0, 0)),
        pl.BlockSpec((None, tc, d), lambda bi, ti: (bi, ti, 0)),
        pl.BlockSpec((None, tc, pdim), lambda bi, ti: (b0 + bi, ti, 0)),
        pl.BlockSpec((None, 2, tc, d // 2), lambda bi, ti: (bi * nt + ti, 0, 0, 0)),
        vec(),
        pl.BlockSpec((d, d), lambda bi, ti: (0, 0)),
        pl.BlockSpec((pdim, d), lambda bi, ti: (0, 0)),
        vec(), vec(),
    ]
    args = [ew, x1, p0, y2, *gains_and_weights]
    aliases = {}
    if prev_out is not None:
        in_specs.append(pl.BlockSpec(memory_space=pl.ANY))
        aliases = {len(args): 0}
        args.append(prev_out)
    return pl.pallas_call(
        _combine_kernel,
        grid=(b, s // tc),
        in_specs=in_specs,
        out_specs=pl.BlockSpec((None, tc, d), lambda bi, ti: (b0 + bi, ti, 0)),
        out_shape=jax.ShapeDtypeStruct((b_total, s, d), _f32),
        input_output_aliases=aliases,
        compiler_params=pltpu.CompilerParams(
            dimension_semantics=("arbitrary", "arbitrary"), vmem_limit_bytes=VMEM_LIMIT),
        name="combine_ple",
    )(*args)


def _layer(x, p_i, g_mix, w_in, conv_w, g_ret, w_out_conv, w_out_ret, w_o, g_moe, w_rg, b_rg, w_re, b_re,
           w_exp_gate, w_exp_up, w_exp_down, g_ple_in, w_ple_gate, w_ple_proj, g_ple_post, g_out):
    b_total, s, d = x.shape
    b = b_total // TOKEN_SPLITS
    n_tok = b * s
    operands = _mixer_operands(s, d, g_mix, w_in, conv_w, g_ret, w_out_conv, w_out_ret, w_o, g_moe,
                               w_rg, b_rg, w_re, b_re)
    tail = (g_ple_in.reshape(1, d), w_ple_gate.astype(_bf16), w_ple_proj.astype(_bf16),
            g_ple_post.reshape(1, d), g_out.reshape(1, d))
    nblk = (2 * n_tok + N_EXPERTS * (MOE_BLOCK - 1) + MOE_BLOCK - 1) // MOE_BLOCK
    nblk = (nblk + EXPERT_SPAN - 1) // EXPERT_SPAN * EXPERT_SPAN
    n_rows = nblk * MOE_BLOCK
    win = SC_WINDOW

    def route(eid, rank, cnt):
        counts = cnt[:, 0]
        padded = (counts + MOE_BLOCK - 1) // MOE_BLOCK * MOE_BLOCK
        pends = jnp.cumsum(padded)
        pstarts = pends - padded
        eids = jnp.arange(N_EXPERTS, dtype=jnp.int32).reshape(N_EXPERTS, 1, 1, 1)
        seg = jnp.sum(jnp.where(eid[None] == eids, pstarts.reshape(N_EXPERTS, 1, 1, 1), 0), axis=0)
        dest = rank + seg
        tiles, _, ts = dest.shape
        idx3 = dest.reshape(tiles, 2, ts // win, win).transpose(0, 2, 1, 3).reshape(n_tok // win, 2, win)
        return dest, idx3, pstarts // MOE_BLOCK, padded // MOE_BLOCK, counts

    st = [dict() for _ in range(TOKEN_SPLITS)]
    out = None

    def mix(h):
        x1, h2p, eid, ew, rank, cnt = _mixer(x, h * b, b, operands)
        dest, idx3, blk0, nblk_e, counts = route(eid, rank, cnt)
        st[h].update(x1=x1, h2p=h2p, ew=ew, dest=dest, idx3=idx3, seg=(blk0, nblk_e, counts))

    def dispatch(h):
        st[h]["xs"] = _sc_dispatch(st[h]["h2p"].reshape(n_tok, d // 2), st[h]["idx3"], n_rows)

    def experts(h):
        st[h]["y"] = _experts(st[h]["xs"], w_exp_gate, w_exp_up, w_exp_down, *st[h]["seg"])

    def gather(h):
        y2 = _sc_gather(st[h]["y"], st[h]["dest"].reshape(-1))
        st[h]["y2"] = y2.reshape(n_tok // MIX_TILE, 2, MIX_TILE, d // 2)

    def combine(h):
        nonlocal out
        out = _combine(st[h]["x1"], p_i, st[h]["y2"], st[h]["ew"], tail, h * b, b_total, out)

    stages = (mix, dispatch, experts, gather, combine)
    for step in range(TOKEN_SPLITS + len(stages) - 1):
        for k in reversed(range(len(stages))):
            if 0 <= step - k < TOKEN_SPLITS:
                stages[k](step - k)
    return out


def kernel(x, p, g_mix, w_in, conv_w, g_ret, w_out_conv, w_out_ret, w_o, g_moe, w_rg, b_rg, w_re, b_re, w_exp_gate, w_exp_up, w_exp_down, g_ple_in, w_ple_gate, w_ple_proj, g_ple_post, g_final):
    depth = p.shape[0]
    assert depth == 1, "the final norm is fused into the single layer's combine kernel"
    return _layer(x, p[0], g_mix[0], w_in[0], conv_w[0], g_ret[0], w_out_conv[0], w_out_ret[0], w_o[0],
                  g_moe[0], w_rg[0], b_rg[0], w_re[0], b_re[0], w_exp_gate[0], w_exp_up[0], w_exp_down[0],
                  g_ple_in[0], w_ple_gate[0], w_ple_proj[0], g_ple_post[0], g_final)
```

```python
import functools

import jax
import jax.numpy as jnp
import numpy as np
from jax import lax
from jax.experimental import pallas as pl
from jax.experimental.pallas import tpu as pltpu
from jax.experimental.pallas import tpu_sc as plsc

EPS = 1e-6
CONV_K = 3
RET_HEADS = 8
RET_DK = 64
RET_DV = 128
RET_CHUNK = 128
ROPE_BASE = 10000.0
N_GROUPS = 4
EXPERTS_PER_GROUP = 8
N_EXPERTS = N_GROUPS * EXPERTS_PER_GROUP
MOE_BLOCK = 256
LANES = 128
SUBLANES = 8
ROUTER_ROWS = 48
MIX_TILE = 512
EXPERT_SPAN = 8
SLICE_PARTS = (5, 3)
SC_CORES = 2
SC_SUBCORES = 16
SC_WORKERS = SC_CORES * SC_SUBCORES
SC_WINDOW = 128
BF16_BITS = 16
HIGH_HALF = np.uint32(0xFFFF0000)
VMEM_LIMIT = 56 * 1024 * 1024

_bf16 = jnp.bfloat16
_f32 = jnp.float32


def _sigmoid(v):
    return 0.5 * jnp.tanh(0.5 * v) + 0.5


def _rms(v, g):
    ms = jnp.mean(v * v, axis=-1, keepdims=True)
    return v * lax.rsqrt(ms + EPS) * g


def _dot(a, b):
    return jnp.dot(a, b, preferred_element_type=_f32)


def _pack_halves(v):
    bits = lax.bitcast_convert_type(v.astype(_bf16).astype(_f32), jnp.uint32)
    c = v.shape[1] // 2
    return (bits[:, :c] >> BF16_BITS) | (bits[:, c:] & HIGH_HALF)


def _unpack_halves(w):
    lo = lax.bitcast_convert_type(w << BF16_BITS, _f32)
    hi = lax.bitcast_convert_type(w & HIGH_HALF, _f32)
    return lo, hi


def _const_spec(shape):
    nd = len(shape)
    return pl.BlockSpec(shape, lambda *_: (0,) * nd, pipeline_mode=pl.Buffered(1))


def _mixer_kernel(x_ref, gmix_ref, win_ref, convw_ref, cos_ref, sin_ref, dmask_ref, qd_ref, kdt_ref,
                  sdec_ref, bmask_ref, gret_ref, woc_ref, wor_ref, wo_ref, gmoe_ref, wrt_ref, rbias_ref,
                  tri_ref,
                  x1_ref, h2p_ref, eid_ref, ew_ref, rank_ref, cnt_ref,
                  hb_ref, qr_ref, kr_ref, vb_ref, o_ref, og_ref, acc_ref, cuc_ref, state_ref, carry_ref):
    ts, d = x_ref.shape
    t = pl.program_id(1)
    n = pl.program_id(0) * pl.num_programs(1) + t
    router_refs = (wrt_ref, rbias_ref, tri_ref, carry_ref, eid_ref, ew_ref, rank_ref, cnt_ref)

    @pl.when(t == 0)
    def _():
        cuc_ref[...] = jnp.zeros_like(cuc_ref)
        state_ref[...] = jnp.zeros_like(state_ref)

    @pl.when(n == 0)
    def _():
        carry_ref[...] = jnp.zeros_like(carry_ref)

    x = x_ref[...]
    hb_ref[...] = _rms(x, gmix_ref[...]).astype(_bf16)
    hb = hb_ref[...]

    def proj(lo, hi):
        return _dot(hb, win_ref[:, lo:hi])

    cu = proj(d, 2 * d) * proj(0, d)
    prev = cuc_ref[...]
    p1 = prev[SUBLANES - 1:SUBLANES, :]
    p2 = prev[SUBLANES - 2:SUBLANES - 1, :]
    rows = lax.broadcasted_iota(jnp.int32, (ts, d), 0)
    s1 = jnp.where(rows == 0, p1, pltpu.roll(cu, 1, 0))
    s2 = jnp.where(rows == 0, p2, jnp.where(rows == 1, p1, pltpu.roll(cu, 2, 0)))
    conv = convw_ref[0:1, :] * s2 + convw_ref[1:2, :] * s1 + convw_ref[2:3, :] * cu
    cuc_ref[...] = cu[ts - SUBLANES:ts, :]
    a = (proj(2 * d, 3 * d) * conv).astype(_bf16)
    acc_ref[...] = _sigmoid(proj(6 * d, 7 * d)) * _dot(a, woc_ref[...])

    qk0 = 3 * d
    cosv = cos_ref[...]
    sinv = sin_ref[...]
    lane = lax.broadcasted_iota(jnp.int32, (ts, LANES), 1)
    first_half = (lane % RET_DK) < (RET_DK // 2)
    qk_w = RET_HEADS * RET_DK
    for dst, base, scale in ((qr_ref, qk0, None), (kr_ref, qk0 + qk_w, RET_DK ** -0.5)):
        zw = proj(base, base + qk_w)
        for g in range(qk_w // LANES):
            z = zw[:, g * LANES:(g + 1) * LANES]
            zs = jnp.where(first_half, pltpu.roll(z, LANES - RET_DK // 2, 1), pltpu.roll(z, RET_DK // 2, 1))
            r = z * cosv + zs * sinv
            if scale is not None:
                r = r * scale
            dst[:, g * LANES:(g + 1) * LANES] = r
    vb_ref[...] = proj(4 * d, 5 * d).astype(_bf16)

    c = RET_CHUNK
    lane_c = lax.broadcasted_iota(jnp.int32, (c, LANES), 1)
    even = lane_c < RET_DK
    for ci in range(ts // c):
        r0 = ci * c
        for j in range(RET_HEADS // 2):
            q2 = qr_ref[r0:r0 + c, j * LANES:(j + 1) * LANES]
            k2 = kr_ref[r0:r0 + c, j * LANES:(j + 1) * LANES]
            v2 = vb_ref[r0:r0 + c, 2 * j * RET_DV:(2 * j + 2) * RET_DV]
            kt = k2.T
            qq = jnp.concatenate([jnp.where(even, q2, 0.0), jnp.where(even, 0.0, q2)], axis=0).astype(_bf16)
            sc = _dot(qq, kt.astype(_bf16))
            pe = (sc[:c] * dmask_ref[2 * j]).astype(_bf16)
            po = (sc[c:] * dmask_ref[2 * j + 1]).astype(_bf16)
            inner = jnp.concatenate([_dot(pe, v2[:, :RET_DV]), _dot(po, v2[:, RET_DV:])], axis=1)
            st = state_ref[j]
            cross = _dot(q2.astype(_bf16), st.astype(_bf16)) * qd_ref[:, 2 * j * RET_DV:(2 * j + 2) * RET_DV]
            o_ref[r0:r0 + c, 2 * j * RET_DV:(2 * j + 2) * RET_DV] = inner + cross
            upd = _dot((kt * kdt_ref[j]).astype(_bf16), v2)
            state_ref[j] = st * sdec_ref[j] + upd * bmask_ref[...]

    for hp in range(RET_HEADS // 2):
        sg2 = proj(5 * d + 2 * hp * RET_DV, 5 * d + (2 * hp + 2) * RET_DV)
        for h in (2 * hp, 2 * hp + 1):
            sl = slice(h * RET_DV, (h + 1) * RET_DV)
            sg = sg2[:, (h - 2 * hp) * RET_DV:(h - 2 * hp + 1) * RET_DV]
            on = _rms(o_ref[:, sl], gret_ref[:, sl])
            og_ref[:, sl] = (sg * _sigmoid(sg) * on).astype(_bf16)
    yr = _dot(og_ref[...], wor_ref[...])
    mixed = acc_ref[...] + _sigmoid(proj(7 * d, 8 * d)) * yr
    x1 = x + _dot(mixed.astype(_bf16), wo_ref[...])
    x1_ref[...] = x1

    h2 = _rms(x1, gmoe_ref[...]).astype(_bf16)
    h2p_ref[...] = _pack_halves(h2)
    _router(h2, n, *router_refs)


def _router(h2, tile, wrt_ref, rbias_ref, tri_ref, carry_ref, eid_ref, ew_ref, rank_ref, cnt_ref):
    ts = h2.shape[0]
    lt = lax.dot_general(wrt_ref[...], h2, (((1,), (1,)), ((), ())), preferred_element_type=_f32)
    lt = lt + rbias_ref[...]
    g0, g1, g2, g3 = (lt[i:i + 1, :] for i in range(N_GROUPS))
    gmax = jnp.maximum(jnp.maximum(g0, g1), jnp.maximum(g2, g3))
    grp = jnp.where(g0 == gmax, 0, jnp.where(g1 == gmax, 1, jnp.where(g2 == gmax, 2, 3)))
    gsum = jnp.exp(g0 - gmax) + jnp.exp(g1 - gmax) + jnp.exp(g2 - gmax) + jnp.exp(g3 - gmax)
    g_w = 1.0 / gsum
    e_in = lt[SUBLANES:2 * SUBLANES, :]
    for g in range(1, N_GROUPS):
        e_in = jnp.where(grp == g, lt[(g + 1) * SUBLANES:(g + 2) * SUBLANES, :], e_in)
    ridx = lax.broadcasted_iota(jnp.int32, (EXPERTS_PER_GROUP, ts), 0)
    top1 = jnp.max(e_in, axis=0, keepdims=True)
    i1 = jnp.min(jnp.where(e_in == top1, ridx, EXPERTS_PER_GROUP), axis=0, keepdims=True)
    e_m = jnp.where(ridx == i1, -jnp.inf, e_in)
    top2 = jnp.max(e_m, axis=0, keepdims=True)
    i2 = jnp.min(jnp.where(e_m == top2, ridx, EXPERTS_PER_GROUP), axis=0, keepdims=True)
    ex = jnp.exp(top2 - top1)
    den = 1.0 + ex
    id0 = grp * EXPERTS_PER_GROUP + i1
    id1 = grp * EXPERTS_PER_GROUP + i2
    eid_ref[tile, 0:1, :] = id0
    eid_ref[tile, 1:2, :] = id1
    ew_ref[tile, 0:1, :] = (1.0 / den) * g_w
    ew_ref[tile, 1:2, :] = (ex / den) * g_w

    eidx = lax.broadcasted_iota(jnp.int32, (N_EXPERTS, ts), 0)
    oh0 = (eidx == id0).astype(_f32)
    oh1 = (eidx == id1).astype(_f32)
    cnt = (oh0 + oh1).astype(_bf16)
    before = carry_ref[...] + _dot(cnt, tri_ref[0])
    rank_ref[tile, 0:1, :] = jnp.sum(oh0 * before, axis=0, keepdims=True).astype(jnp.int32)
    rank_ref[tile, 1:2, :] = jnp.sum(oh1 * before, axis=0, keepdims=True).astype(jnp.int32)
    total = carry_ref[...] + _dot(cnt, tri_ref[1])
    carry_ref[...] = total
    cnt_ref[...] = total[:, :LANES].astype(jnp.int32)


def _retention_tables():
    c = RET_CHUNK
    log_gamma = np.log1p(-np.exp2(-5.0 - np.arange(RET_HEADS, dtype=np.float64)))
    pos = np.arange(c, dtype=np.float64)
    diff = pos[:, None] - pos[None, :]
    dmask = np.where((diff >= 0)[None], np.exp(log_gamma[:, None, None] * np.maximum(diff, 0.0)[None]), 0.0)
    q_decay = np.exp(log_gamma[:, None] * (pos[None, :] + 1.0))
    k_decay = np.exp(log_gamma[:, None] * (c - 1.0 - pos[None, :]))
    chunk_decay = np.exp(log_gamma * c)
    qd = np.repeat(q_decay.T, RET_DV, axis=1)
    kdt = np.repeat(k_decay.reshape(RET_HEADS // 2, 2, 1, c), RET_DK, axis=2).reshape(RET_HEADS // 2, 2 * RET_DK, c)
    sdec = np.repeat(chunk_decay.reshape(RET_HEADS // 2, 2, 1), RET_DK, axis=2).reshape(RET_HEADS // 2, 2 * RET_DK, 1)
    sdec = np.broadcast_to(sdec, (RET_HEADS // 2, 2 * RET_DK, 2 * RET_DV))
    rr = np.arange(2 * RET_DK)[:, None] // RET_DK
    cc = np.arange(2 * RET_DV)[None, :] // RET_DV
    bmask = (rr == cc).astype(np.float64)
    return tuple(jnp.asarray(np.ascontiguousarray(v), _f32) for v in (dmask, qd, kdt, sdec, bmask))


def _rope_tables(s_len):
    inv = ROPE_BASE ** (-jnp.arange(0, RET_DK, 2, dtype=_f32) / RET_DK)
    ang = jnp.arange(s_len, dtype=_f32)[:, None] * inv[None, :]
    cos, sin = jnp.cos(ang), jnp.sin(ang)
    cos_t = jnp.tile(cos, (1, LANES // (RET_DK // 2)))
    sin_t = jnp.tile(jnp.concatenate([-sin, sin], axis=1), (1, LANES // RET_DK))
    return cos_t, sin_t


def _mixer_operands(s, d, g_mix, w_in, conv_w, g_ret, w_out_conv, w_out_ret, w_o, g_moe, w_rg, b_rg, w_re, b_re):
    ts = MIX_TILE
    cos_t, sin_t = _rope_tables(s)
    dmask, qd, kdt, sdec, bmask = _retention_tables()
    wrt = jnp.zeros((ROUTER_ROWS, d), _f32)
    wrt = wrt.at[:N_GROUPS].set(w_rg.T).at[SUBLANES:SUBLANES + N_EXPERTS].set(w_re.T).astype(_bf16)
    rb = jnp.zeros((ROUTER_ROWS,), _f32).at[:N_GROUPS].set(b_rg).at[SUBLANES:SUBLANES + N_EXPERTS].set(b_re)
    rbias = jnp.broadcast_to(rb[:, None], (ROUTER_ROWS, ts))
    ii = np.arange(ts)
    tri = jnp.asarray(np.stack([(ii[:, None] < ii[None, :]), np.ones((ts, ts), bool)]), _bf16)
    return (g_mix.reshape(1, d), w_in.astype(_bf16), conv_w, cos_t, sin_t, dmask, qd, kdt, sdec, bmask,
            g_ret.reshape(1, d), w_out_conv.astype(_bf16), w_out_ret.astype(_bf16), w_o.astype(_bf16),
            g_moe.reshape(1, d), wrt, rbias, tri)


def _mixer(x, b0, b, operands):
    _, s, d = x.shape
    ts = MIX_TILE
    nt = s // ts
    tile3 = lambda w: pl.BlockSpec((None, ts, w), lambda bi, ti: (bi, ti, 0))
    route = lambda: pl.BlockSpec((b * nt, 2, ts), lambda bi, ti: (0, 0, 0))
    rope_spec = pl.BlockSpec((ts, LANES), lambda bi, ti: (ti, 0))
    in_specs = [pl.BlockSpec((None, ts, d), lambda bi, ti: (b0 + bi, ti, 0))]
    in_specs += [rope_spec if i in (3, 4) else _const_spec(op.shape) for i, op in enumerate(operands)]
    out_shape = [
        jax.ShapeDtypeStruct((b, s, d), _f32),
        jax.ShapeDtypeStruct((b, s, d // 2), jnp.uint32),
        jax.ShapeDtypeStruct((b * nt, 2, ts), jnp.int32),
        jax.ShapeDtypeStruct((b * nt, 2, ts), _f32),
        jax.ShapeDtypeStruct((b * nt, 2, ts), jnp.int32),
        jax.ShapeDtypeStruct((N_EXPERTS, LANES), jnp.int32),
    ]
    out_specs = [tile3(d), tile3(d // 2), route(), route(), route(),
                 pl.BlockSpec((N_EXPERTS, LANES), lambda bi, ti: (0, 0))]
    scratch = [
        pltpu.VMEM((ts, d), _bf16),
        pltpu.VMEM((ts, RET_HEADS * RET_DK), _f32),
        pltpu.VMEM((ts, RET_HEADS * RET_DK), _f32),
        pltpu.VMEM((ts, RET_HEADS * RET_DV), _bf16),
        pltpu.VMEM((ts, RET_HEADS * RET_DV), _f32),
        pltpu.VMEM((ts, RET_HEADS * RET_DV), _bf16),
        pltpu.VMEM((ts, d), _f32),
        pltpu.VMEM((SUBLANES, d), _f32),
        pltpu.VMEM((RET_HEADS // 2, 2 * RET_DK, 2 * RET_DV), _f32),
        pltpu.VMEM((N_EXPERTS, ts), _f32),
    ]
    return pl.pallas_call(
        _mixer_kernel,
        grid=(b, nt),
        in_specs=in_specs,
        out_specs=out_specs,
        out_shape=out_shape,
        scratch_shapes=scratch,
        compiler_params=pltpu.CompilerParams(
            dimension_semantics=("arbitrary", "arbitrary"), vmem_limit_bytes=VMEM_LIMIT),
        name="mixer_router",
    )(x, *operands)


def _sc_worker_id():
    return lax.axis_index("s") * SC_CORES + lax.axis_index("c")


def _sc_mesh():
    return plsc.VectorSubcoreMesh(core_axis_name="c", subcore_axis_name="s")


def _sc_dispatch(src, idx3, n_rows):
    t, d = src.shape
    n_win_total, _, win = idx3.shape
    n_win = n_win_total // SC_WORKERS

    @functools.partial(
        pl.kernel, mesh=_sc_mesh(),
        out_type=jax.ShapeDtypeStruct((n_rows, d), src.dtype),
        scratch_types=[pltpu.VMEM((2, win), jnp.int32), pltpu.VMEM((win, d), src.dtype)],
    )
    def dispatch(src_hbm, idx_hbm, out_hbm, idx_v, rows_v):
        wid = _sc_worker_id()

        @pl.loop(0, n_win)
        def _(i):
            w = wid * n_win + i
            off = pl.multiple_of(w * win, SUBLANES)
            pltpu.sync_copy(idx_hbm.at[w], idx_v)
            pltpu.sync_copy(src_hbm.at[pl.ds(off, win)], rows_v)
            pltpu.sync_copy(rows_v, out_hbm.at[idx_v.at[0]])
            pltpu.sync_copy(rows_v, out_hbm.at[idx_v.at[1]])

    return dispatch(src, idx3)


def _sc_gather(table, idx):
    n = idx.shape[0]
    d = table.shape[1]
    win = SC_WINDOW
    per_w = n // SC_WORKERS
    n_win = per_w // win

    @functools.partial(
        pl.kernel, mesh=_sc_mesh(),
        out_type=jax.ShapeDtypeStruct((n, d), table.dtype),
        scratch_types=[pltpu.VMEM((win,), jnp.int32), pltpu.VMEM((win, d), table.dtype),
                       pltpu.SemaphoreType.DMA],
    )
    def gather(table_hbm, idx_hbm, out_hbm, idx_v, rows_v, sem):
        base = _sc_worker_id() * per_w

        @pl.loop(0, n_win)
        def _(i):
            off = pl.multiple_of(base + i * win, SUBLANES)
            pltpu.sync_copy(idx_hbm.at[pl.ds(off, win)], idx_v)
            pltpu.async_copy(table_hbm.at[idx_v], rows_v, sem).wait()
            pltpu.sync_copy(rows_v, out_hbm.at[pl.ds(off, win)])

    return gather(table, idx)


def _expert_kernel(blk0_ref, nblk_ref, count_ref, xs_ref, wg_ref, wu_ref, wd_ref, y_ref,
                   xbuf, ybuf, wgb_ref, wub_ref, wdb_ref, sem_in, sem_out):
    e = pl.program_id(0)
    last = pl.num_programs(0) - 1
    blk0 = blk0_ref[e]
    nblk = nblk_ref[e]
    count = count_ref[e]
    total = blk0_ref[last] + nblk_ref[last]
    n_span = (total + EXPERT_SPAN - 1) // EXPERT_SPAN
    span_rows = EXPERT_SPAN * MOE_BLOCK
    half = xbuf.shape[2]

    def span_in(s):
        return pltpu.make_async_copy(xs_ref.at[pl.ds(s * span_rows, span_rows)], xbuf.at[s % 2], sem_in.at[s % 2])

    def span_out(s):
        return pltpu.make_async_copy(ybuf.at[s % 2], y_ref.at[pl.ds(s * span_rows, span_rows)], sem_out.at[s % 2])

    @pl.when(e == 0)
    def _():
        ybuf[...] = jnp.zeros_like(ybuf)

    @pl.when(jnp.logical_and(e == 0, n_span > 0))
    def _():
        span_in(0).start()

    wgb_ref[...] = wg_ref[...].astype(_bf16)
    wub_ref[...] = wu_ref[...].astype(_bf16)
    wdb_ref[...] = wd_ref[...].astype(_bf16)

    def block(j, carry):
        g = blk0 + j
        s = g // EXPERT_SPAN
        k = g % EXPERT_SPAN
        slot = s % 2
        row0 = pl.multiple_of(k * MOE_BLOCK, MOE_BLOCK)

        @pl.when(k == 0)
        def _():
            span_in(s).wait()

            @pl.when(s + 1 < n_span)
            def _():
                span_in(s + 1).start()

            @pl.when(s >= 2)
            def _():
                span_out(s - 2).wait()

        rowid = lax.broadcasted_iota(jnp.int32, (MOE_BLOCK, half), 0)
        xb = xbuf[slot, pl.ds(row0, MOE_BLOCK), :]
        lo, hi = _unpack_halves(jnp.where(rowid < count - j * MOE_BLOCK, xb, jnp.uint32(0)))
        lo = lo.astype(_bf16)
        hi = hi.astype(_bf16)
        gate = _dot(lo, wgb_ref[:half, :]) + _dot(hi, wgb_ref[half:, :])
        up = _dot(lo, wub_ref[:half, :]) + _dot(hi, wub_ref[half:, :])
        hid = (gate * _sigmoid(gate) * up).astype(_bf16)
        ybuf[slot, pl.ds(row0, MOE_BLOCK), :] = _pack_halves(_dot(hid, wdb_ref[...]))

        @pl.when(jnp.logical_or(k == EXPERT_SPAN - 1, g == total - 1))
        def _():
            span_out(s).start()
        return carry

    lax.fori_loop(0, nblk, block, 0)

    @pl.when(jnp.logical_and(e == last, n_span >= 2))
    def _():
        span_out(n_span - 2).wait()

    @pl.when(jnp.logical_and(e == last, n_span >= 1))
    def _():
        span_out(n_span - 1).wait()


def _experts(xs, w_gate, w_up, w_down, blk0, nblk, counts):
    n_rows, half = xs.shape
    d = 2 * half
    ne, _, de = w_gate.shape
    grid_spec = pltpu.PrefetchScalarGridSpec(
        num_scalar_prefetch=3,
        grid=(ne,),
        in_specs=[
            pl.BlockSpec(memory_space=pl.ANY),
            pl.BlockSpec((None, d, de), lambda e, *_: (e, 0, 0)),
            pl.BlockSpec((None, d, de), lambda e, *_: (e, 0, 0)),
            pl.BlockSpec((None, de, d), lambda e, *_: (e, 0, 0)),
        ],
        out_specs=pl.BlockSpec(memory_space=pl.ANY),
        scratch_shapes=[
            pltpu.VMEM((2, EXPERT_SPAN * MOE_BLOCK, half), jnp.uint32),
            pltpu.VMEM((2, EXPERT_SPAN * MOE_BLOCK, half), jnp.uint32),
            pltpu.VMEM((d, de), _bf16), pltpu.VMEM((d, de), _bf16), pltpu.VMEM((de, d), _bf16),
            pltpu.SemaphoreType.DMA((2,)), pltpu.SemaphoreType.DMA((2,)),
        ],
    )
    return pl.pallas_call(
        _expert_kernel,
        grid_spec=grid_spec,
        out_shape=jax.ShapeDtypeStruct((n_rows, half), jnp.uint32),
        compiler_params=pltpu.CompilerParams(dimension_semantics=("arbitrary",), vmem_limit_bytes=VMEM_LIMIT),
        name="experts",
    )(blk0, nblk, counts, xs, w_gate, w_up, w_down)


def _combine_kernel(ew_ref, x1_ref, p_ref, y2_ref, gin_ref, wpg_ref, wpp_ref, gpost_ref, gfin_ref, *rest):
    out_ref = rest[-1]
    tc, d = x1_ref.shape
    x2 = x1_ref[...]
    for k in range(2):
        wcol = jnp.broadcast_to(ew_ref[k:k + 1, :], (LANES, tc)).T
        yk = jnp.concatenate(_unpack_halves(y2_ref[k]), axis=1)
        x2 = x2 + jnp.tile(wcol, (1, d // LANES)) * yk
    gate = _sigmoid(_dot(_rms(x2, gin_ref[...]).astype(_bf16), wpg_ref[...]))
    ple = _rms(_dot(p_ref[...].astype(_bf16), wpp_ref[...]), gpost_ref[...])
    out_ref[...] = _rms(x2 + gate * ple, gfin_ref[...])


def _combine(x1, p0, y2, ew, gains_and_weights, b0, b_total, prev_out):
    b, s, d = x1.shape
    tc = MIX_TILE
    nt = s // tc
    pdim = p0.shape[-1]
    vec = lambda: pl.BlockSpec((1, d), lambda bi, ti: (0, 0))
    in_specs = [
        pl.BlockSpec((None, 2, tc), lambda bi, ti: (bi * nt + ti, 0, 0)),
        pl.BlockSpec((None, tc, d), lambda bi, ti: (bi, ti, 0)),
        pl.BlockSpec((None, tc, pdim), lambda bi, ti: (b0 + bi, ti, 0)),
        pl.BlockSpec((None, 2, tc, d // 2), lambda bi, ti: (bi * nt + ti, 0, 0, 0)),
        vec(),
        pl.BlockSpec((d, d), lambda bi, ti: (0, 0)),
        pl.BlockSpec((pdim, d), lambda bi, ti: (0, 0)),
        vec(), vec(),
    ]
    args = [ew, x1, p0, y2, *gains_and_weights]
    aliases = {}
    if prev_out is not None:
        in_specs.append(pl.BlockSpec(memory_space=pl.ANY))
        aliases = {len(args): 0}
        args.append(prev_out)
    return pl.pallas_call(
        _combine_kernel,
        grid=(b, s // tc),
        in_specs=in_specs,
        out_specs=pl.BlockSpec((None, tc, d), lambda bi, ti: (b0 + bi, ti, 0)),
        out_shape=jax.ShapeDtypeStruct((b_total, s, d), _f32),
        input_output_aliases=aliases,
        compiler_params=pltpu.CompilerParams(
            dimension_semantics=("arbitrary", "arbitrary"), vmem_limit_bytes=VMEM_LIMIT),
        name="combine_ple",
    )(*args)


def _layer(x, p_i, g_mix, w_in, conv_w, g_ret, w_out_conv, w_out_ret, w_o, g_moe, w_rg, b_rg, w_re, b_re,
           w_exp_gate, w_exp_up, w_exp_down, g_ple_in, w_ple_gate, w_ple_proj, g_ple_post, g_out):
    b_total, s, d = x.shape
    assert b_total % sum(SLICE_PARTS) == 0
    sizes = [b_total // sum(SLICE_PARTS) * part for part in SLICE_PARTS]
    starts = [sum(sizes[:h]) for h in range(len(sizes))]
    operands = _mixer_operands(s, d, g_mix, w_in, conv_w, g_ret, w_out_conv, w_out_ret, w_o, g_moe,
                               w_rg, b_rg, w_re, b_re)
    tail = (g_ple_in.reshape(1, d), w_ple_gate.astype(_bf16), w_ple_proj.astype(_bf16),
            g_ple_post.reshape(1, d), g_out.reshape(1, d))
    win = SC_WINDOW

    def route(eid, rank, cnt, n_tok):
        counts = cnt[:, 0]
        padded = (counts + MOE_BLOCK - 1) // MOE_BLOCK * MOE_BLOCK
        pends = jnp.cumsum(padded)
        pstarts = pends - padded
        eids = jnp.arange(N_EXPERTS, dtype=jnp.int32).reshape(N_EXPERTS, 1, 1, 1)
        seg = jnp.sum(jnp.where(eid[None] == eids, pstarts.reshape(N_EXPERTS, 1, 1, 1), 0), axis=0)
        dest = rank + seg
        tiles, _, ts = dest.shape
        idx3 = dest.reshape(tiles, 2, ts // win, win).transpose(0, 2, 1, 3).reshape(n_tok // win, 2, win)
        return dest, idx3, pstarts // MOE_BLOCK, padded // MOE_BLOCK, counts

    st = [dict() for _ in sizes]
    out = None

    def mix(h):
        n_tok = sizes[h] * s
        nblk = (2 * n_tok + N_EXPERTS * (MOE_BLOCK - 1) + MOE_BLOCK - 1) // MOE_BLOCK
        nblk = (nblk + EXPERT_SPAN - 1) // EXPERT_SPAN * EXPERT_SPAN
        x1, h2p, eid, ew, rank, cnt = _mixer(x, starts[h], sizes[h], operands)
        dest, idx3, blk0, nblk_e, counts = route(eid, rank, cnt, n_tok)
        st[h].update(x1=x1, h2p=h2p, ew=ew, dest=dest, idx3=idx3, seg=(blk0, nblk_e, counts),
                     n_tok=n_tok, n_rows=nblk * MOE_BLOCK)

    def dispatch(h):
        st[h]["xs"] = _sc_dispatch(st[h]["h2p"].reshape(st[h]["n_tok"], d // 2), st[h]["idx3"], st[h]["n_rows"])

    def experts(h):
        st[h]["y"] = _experts(st[h]["xs"], w_exp_gate, w_exp_up, w_exp_down, *st[h]["seg"])

    def gather(h):
        y2 = _sc_gather(st[h]["y"], st[h]["dest"].reshape(-1))
        st[h]["y2"] = y2.reshape(st[h]["n_tok"] // MIX_TILE, 2, MIX_TILE, d // 2)

    def combine(h):
        nonlocal out
        out = _combine(st[h]["x1"], p_i, st[h]["y2"], st[h]["ew"], tail, starts[h], b_total, out)

    stages = (mix, dispatch, experts, gather, combine)
    for step in range(len(sizes) + len(stages) - 1):
        for k in reversed(range(len(stages))):
            if 0 <= step - k < len(sizes):
                stages[k](step - k)
    return out


def kernel(x, p, g_mix, w_in, conv_w, g_ret, w_out_conv, w_out_ret, w_o, g_moe, w_rg, b_rg, w_re, b_re, w_exp_gate, w_exp_up, w_exp_down, g_ple_in, w_ple_gate, w_ple_proj, g_ple_post, g_final):
    depth = p.shape[0]
    assert depth == 1, "the final norm is fused into the single layer's combine kernel"
    return _layer(x, p[0], g_mix[0], w_in[0], conv_w[0], g_ret[0], w_out_conv[0], w_out_ret[0], w_o[0],
                  g_moe[0], w_rg[0], b_rg[0], w_re[0], b_re[0], w_exp_gate[0], w_exp_up[0], w_exp_down[0],
                  g_ple_in[0], w_ple_gate[0], w_ple_proj[0], g_ple_post[0], g_final)
```

```python
import functools

import jax
import jax.numpy as jnp
import numpy as np
from jax import lax
from jax.experimental import pallas as pl
from jax.experimental.pallas import tpu as pltpu
from jax.experimental.pallas import tpu_sc as plsc

EPS = 1e-6
CONV_K = 3
RET_HEADS = 8
RET_DK = 64
RET_DV = 128
RET_CHUNK = 128
ROPE_BASE = 10000.0
N_GROUPS = 4
EXPERTS_PER_GROUP = 8
N_EXPERTS = N_GROUPS * EXPERTS_PER_GROUP
MOE_BLOCK = 256
LANES = 128
SUBLANES = 8
ROUTER_ROWS = 48
MIX_TILE = 512
EXPERT_SPAN = 8
SLICE_PARTS = (5, 3)
SC_CORES = 2
SC_SUBCORES = 16
SC_WORKERS = SC_CORES * SC_SUBCORES
SC_WINDOW = 128
BF16_BITS = 16
HIGH_HALF = np.uint32(0xFFFF0000)
VMEM_LIMIT = 56 * 1024 * 1024

_bf16 = jnp.bfloat16
_f32 = jnp.float32


def _sigmoid(v):
    return 0.5 * jnp.tanh(0.5 * v) + 0.5


def _rms(v, g):
    ms = jnp.mean(v * v, axis=-1, keepdims=True)
    return v * lax.rsqrt(ms + EPS) * g


def _dot(a, b):
    return jnp.dot(a, b, preferred_element_type=_f32)


def _pack_halves(v):
    bits = lax.bitcast_convert_type(v.astype(_bf16).astype(_f32), jnp.uint32)
    c = v.shape[1] // 2
    return (bits[:, :c] >> BF16_BITS) | (bits[:, c:] & HIGH_HALF)


def _unpack_halves(w):
    lo = lax.bitcast_convert_type(w << BF16_BITS, _f32)
    hi = lax.bitcast_convert_type(w & HIGH_HALF, _f32)
    return lo, hi


def _const_spec(shape):
    nd = len(shape)
    return pl.BlockSpec(shape, lambda *_: (0,) * nd, pipeline_mode=pl.Buffered(1))


def _mixer_kernel(x_ref, gmix_ref, win_ref, convw_ref, cos_ref, sin_ref, dmask_ref, qd_ref, kdt_ref,
                  sdec_ref, bmask_ref, gret_ref, woc_ref, wor_ref, wo_ref, gmoe_ref, wrt_ref, rbias_ref,
                  tri_ref,
                  x1_ref, h2p_ref, eid_ref, ew_ref, rank_ref, cnt_ref,
                  hb_ref, qr_ref, kr_ref, vb_ref, o_ref, og_ref, acc_ref, cuc_ref, state_ref, carry_ref):
    ts, d = x_ref.shape
    t = pl.program_id(1)
    n = pl.program_id(0) * pl.num_programs(1) + t
    router_refs = (wrt_ref, rbias_ref, tri_ref, carry_ref, eid_ref, ew_ref, rank_ref, cnt_ref)

    @pl.when(t == 0)
    def _():
        cuc_ref[...] = jnp.zeros_like(cuc_ref)
        state_ref[...] = jnp.zeros_like(state_ref)

    @pl.when(n == 0)
    def _():
        carry_ref[...] = jnp.zeros_like(carry_ref)

    x = x_ref[...]
    hb_ref[...] = _rms(x, gmix_ref[...]).astype(_bf16)
    hb = hb_ref[...]

    def proj(lo, hi):
        return _dot(hb, win_ref[:, lo:hi])

    cu = proj(d, 2 * d) * proj(0, d)
    prev = cuc_ref[...]
    p1 = prev[SUBLANES - 1:SUBLANES, :]
    p2 = prev[SUBLANES - 2:SUBLANES - 1, :]
    rows = lax.broadcasted_iota(jnp.int32, (ts, d), 0)
    s1 = jnp.where(rows == 0, p1, pltpu.roll(cu, 1, 0))
    s2 = jnp.where(rows == 0, p2, jnp.where(rows == 1, p1, pltpu.roll(cu, 2, 0)))
    conv = convw_ref[0:1, :] * s2 + convw_ref[1:2, :] * s1 + convw_ref[2:3, :] * cu
    cuc_ref[...] = cu[ts - SUBLANES:ts, :]
    a = (proj(2 * d, 3 * d) * conv).astype(_bf16)
    acc_ref[...] = _sigmoid(proj(6 * d, 7 * d)) * _dot(a, woc_ref[...])

    qk0 = 3 * d
    cosv = cos_ref[...]
    sinv = sin_ref[...]
    lane = lax.broadcasted_iota(jnp.int32, (ts, LANES), 1)
    first_half = (lane % RET_DK) < (RET_DK // 2)
    qk_w = RET_HEADS * RET_DK
    for dst, base, scale in ((qr_ref, qk0, None), (kr_ref, qk0 + qk_w, RET_DK ** -0.5)):
        zw = proj(base, base + qk_w)
        for g in range(qk_w // LANES):
            z = zw[:, g * LANES:(g + 1) * LANES]
            zs = jnp.where(first_half, pltpu.roll(z, LANES - RET_DK // 2, 1), pltpu.roll(z, RET_DK // 2, 1))
            r = z * cosv + zs * sinv
            if scale is not None:
                r = r * scale
            dst[:, g * LANES:(g + 1) * LANES] = r
    vb_ref[...] = proj(4 * d, 5 * d).astype(_bf16)

    c = RET_CHUNK
    lane_c = lax.broadcasted_iota(jnp.int32, (c, LANES), 1)
    even = lane_c < RET_DK
    for ci in range(ts // c):
        r0 = ci * c
        for j in range(RET_HEADS // 2):
            q2 = qr_ref[r0:r0 + c, j * LANES:(j + 1) * LANES]
            k2 = kr_ref[r0:r0 + c, j * LANES:(j + 1) * LANES]
            v2 = vb_ref[r0:r0 + c, 2 * j * RET_DV:(2 * j + 2) * RET_DV]
            kt = k2.T
            qq = jnp.concatenate([jnp.where(even, q2, 0.0), jnp.where(even, 0.0, q2)], axis=0).astype(_bf16)
            sc = _dot(qq, kt.astype(_bf16))
            pe = (sc[:c] * dmask_ref[2 * j]).astype(_bf16)
            po = (sc[c:] * dmask_ref[2 * j + 1]).astype(_bf16)
            inner = jnp.concatenate([_dot(pe, v2[:, :RET_DV]), _dot(po, v2[:, RET_DV:])], axis=1)
            st = state_ref[j]
            cross = _dot(q2.astype(_bf16), st.astype(_bf16)) * qd_ref[:, 2 * j * RET_DV:(2 * j + 2) * RET_DV]
            o_ref[r0:r0 + c, 2 * j * RET_DV:(2 * j + 2) * RET_DV] = inner + cross
            upd = _dot((kt * kdt_ref[j]).astype(_bf16), v2)
            state_ref[j] = st * sdec_ref[j] + upd * bmask_ref[...]

    for hp in range(RET_HEADS // 2):
        sg2 = proj(5 * d + 2 * hp * RET_DV, 5 * d + (2 * hp + 2) * RET_DV)
        for h in (2 * hp, 2 * hp + 1):
            sl = slice(h * RET_DV, (h + 1) * RET_DV)
            sg = sg2[:, (h - 2 * hp) * RET_DV:(h - 2 * hp + 1) * RET_DV]
            on = _rms(o_ref[:, sl], gret_ref[:, sl])
            og_ref[:, sl] = (sg * _sigmoid(sg) * on).astype(_bf16)
    yr = _dot(og_ref[...], wor_ref[...])
    mixed = acc_ref[...] + _sigmoid(proj(7 * d, 8 * d)) * yr
    x1 = x + _dot(mixed.astype(_bf16), wo_ref[...])
    x1_ref[...] = x1

    h2 = _rms(x1, gmoe_ref[...]).astype(_bf16)
    h2p_ref[...] = _pack_halves(h2)
    _router(h2, n, *router_refs)


def _router(h2, tile, wrt_ref, rbias_ref, tri_ref, carry_ref, eid_ref, ew_ref, rank_ref, cnt_ref):
    ts = h2.shape[0]
    lt = lax.dot_general(wrt_ref[...], h2, (((1,), (1,)), ((), ())), preferred_element_type=_f32)
    lt = lt + rbias_ref[...]
    g0, g1, g2, g3 = (lt[i:i + 1, :] for i in range(N_GROUPS))
    gmax = jnp.maximum(jnp.maximum(g0, g1), jnp.maximum(g2, g3))
    grp = jnp.where(g0 == gmax, 0, jnp.where(g1 == gmax, 1, jnp.where(g2 == gmax, 2, 3)))
    gsum = jnp.exp(g0 - gmax) + jnp.exp(g1 - gmax) + jnp.exp(g2 - gmax) + jnp.exp(g3 - gmax)
    g_w = 1.0 / gsum
    e_in = lt[SUBLANES:2 * SUBLANES, :]
    for g in range(1, N_GROUPS):
        e_in = jnp.where(grp == g, lt[(g + 1) * SUBLANES:(g + 2) * SUBLANES, :], e_in)
    ridx = lax.broadcasted_iota(jnp.int32, (EXPERTS_PER_GROUP, ts), 0)
    top1 = jnp.max(e_in, axis=0, keepdims=True)
    i1 = jnp.min(jnp.where(e_in == top1, ridx, EXPERTS_PER_GROUP), axis=0, keepdims=True)
    e_m = jnp.where(ridx == i1, -jnp.inf, e_in)
    top2 = jnp.max(e_m, axis=0, keepdims=True)
    i2 = jnp.min(jnp.where(e_m == top2, ridx, EXPERTS_PER_GROUP), axis=0, keepdims=True)
    ex = jnp.exp(top2 - top1)
    den = 1.0 + ex
    id0 = grp * EXPERTS_PER_GROUP + i1
    id1 = grp * EXPERTS_PER_GROUP + i2
    eid_ref[tile, 0:1, :] = id0
    eid_ref[tile, 1:2, :] = id1
    ew_ref[tile, 0:1, :] = (1.0 / den) * g_w
    ew_ref[tile, 1:2, :] = (ex / den) * g_w

    eidx = lax.broadcasted_iota(jnp.int32, (N_EXPERTS, ts), 0)
    oh0 = (eidx == id0).astype(_f32)
    oh1 = (eidx == id1).astype(_f32)
    cnt = (oh0 + oh1).astype(_bf16)
    before = carry_ref[...] + _dot(cnt, tri_ref[0])
    rank_ref[tile, 0:1, :] = jnp.sum(oh0 * before, axis=0, keepdims=True).astype(jnp.int32)
    rank_ref[tile, 1:2, :] = jnp.sum(oh1 * before, axis=0, keepdims=True).astype(jnp.int32)
    total = carry_ref[...] + _dot(cnt, tri_ref[1])
    carry_ref[...] = total
    cnt_ref[...] = total[:, :LANES].astype(jnp.int32)


def _retention_tables():
    c = RET_CHUNK
    log_gamma = np.log1p(-np.exp2(-5.0 - np.arange(RET_HEADS, dtype=np.float64)))
    pos = np.arange(c, dtype=np.float64)
    diff = pos[:, None] - pos[None, :]
    dmask = np.where((diff >= 0)[None], np.exp(log_gamma[:, None, None] * np.maximum(diff, 0.0)[None]), 0.0)
    q_decay = np.exp(log_gamma[:, None] * (pos[None, :] + 1.0))
    k_decay = np.exp(log_gamma[:, None] * (c - 1.0 - pos[None, :]))
    chunk_decay = np.exp(log_gamma * c)
    qd = np.repeat(q_decay.T, RET_DV, axis=1)
    kdt = np.repeat(k_decay.reshape(RET_HEADS // 2, 2, 1, c), RET_DK, axis=2).reshape(RET_HEADS // 2, 2 * RET_DK, c)
    sdec = np.repeat(chunk_decay.reshape(RET_HEADS // 2, 2, 1), RET_DK, axis=2).reshape(RET_HEADS // 2, 2 * RET_DK, 1)
    sdec = np.broadcast_to(sdec, (RET_HEADS // 2, 2 * RET_DK, 2 * RET_DV))
    rr = np.arange(2 * RET_DK)[:, None] // RET_DK
    cc = np.arange(2 * RET_DV)[None, :] // RET_DV
    bmask = (rr == cc).astype(np.float64)
    return tuple(jnp.asarray(np.ascontiguousarray(v), _f32) for v in (dmask, qd, kdt, sdec, bmask))


def _rope_tables(s_len):
    inv = ROPE_BASE ** (-jnp.arange(0, RET_DK, 2, dtype=_f32) / RET_DK)
    ang = jnp.arange(s_len, dtype=_f32)[:, None] * inv[None, :]
    cos, sin = jnp.cos(ang), jnp.sin(ang)
    cos_t = jnp.tile(cos, (1, LANES // (RET_DK // 2)))
    sin_t = jnp.tile(jnp.concatenate([-sin, sin], axis=1), (1, LANES // RET_DK))
    return cos_t, sin_t


def _mixer_operands(s, d, g_mix, w_in, conv_w, g_ret, w_out_conv, w_out_ret, w_o, g_moe, w_rg, b_rg, w_re, b_re):
    ts = MIX_TILE
    cos_t, sin_t = _rope_tables(s)
    dmask, qd, kdt, sdec, bmask = _retention_tables()
    wrt = jnp.zeros((ROUTER_ROWS, d), _f32)
    wrt = wrt.at[:N_GROUPS].set(w_rg.T).at[SUBLANES:SUBLANES + N_EXPERTS].set(w_re.T).astype(_bf16)
    rb = jnp.zeros((ROUTER_ROWS,), _f32).at[:N_GROUPS].set(b_rg).at[SUBLANES:SUBLANES + N_EXPERTS].set(b_re)
    rbias = jnp.broadcast_to(rb[:, None], (ROUTER_ROWS, ts))
    ii = np.arange(ts)
    tri = jnp.asarray(np.stack([(ii[:, None] < ii[None, :]), np.ones((ts, ts), bool)]), _bf16)
    return (g_mix.reshape(1, d), w_in.astype(_bf16), conv_w, cos_t, sin_t, dmask, qd, kdt, sdec, bmask,
            g_ret.reshape(1, d), w_out_conv.astype(_bf16), w_out_ret.astype(_bf16), w_o.astype(_bf16),
            g_moe.reshape(1, d), wrt, rbias, tri)


def _mixer(x, b0, b, operands):
    _, s, d = x.shape
    ts = MIX_TILE
    nt = s // ts
    tile3 = lambda w: pl.BlockSpec((None, ts, w), lambda bi, ti: (bi, ti, 0))
    route = lambda: pl.BlockSpec((b * nt, 2, ts), lambda bi, ti: (0, 0, 0))
    rope_spec = pl.BlockSpec((ts, LANES), lambda bi, ti: (ti, 0))
    in_specs = [pl.BlockSpec((None, ts, d), lambda bi, ti: (b0 + bi, ti, 0))]
    in_specs += [rope_spec if i in (3, 4) else _const_spec(op.shape) for i, op in enumerate(operands)]
    out_shape = [
        jax.ShapeDtypeStruct((b, s, d), _f32),
        jax.ShapeDtypeStruct((b, s, d // 2), jnp.uint32),
        jax.ShapeDtypeStruct((b * nt, 2, ts), jnp.int32),
        jax.ShapeDtypeStruct((b * nt, 2, ts), _f32),
        jax.ShapeDtypeStruct((b * nt, 2, ts), jnp.int32),
        jax.ShapeDtypeStruct((N_EXPERTS, LANES), jnp.int32),
    ]
    out_specs = [tile3(d), tile3(d // 2), route(), route(), route(),
                 pl.BlockSpec((N_EXPERTS, LANES), lambda bi, ti: (0, 0))]
    scratch = [
        pltpu.VMEM((ts, d), _bf16),
        pltpu.VMEM((ts, RET_HEADS * RET_DK), _f32),
        pltpu.VMEM((ts, RET_HEADS * RET_DK), _f32),
        pltpu.VMEM((ts, RET_HEADS * RET_DV), _bf16),
        pltpu.VMEM((ts, RET_HEADS * RET_DV), _f32),
        pltpu.VMEM((ts, RET_HEADS * RET_DV), _bf16),
        pltpu.VMEM((ts, d), _f32),
        pltpu.VMEM((SUBLANES, d), _f32),
        pltpu.VMEM((RET_HEADS // 2, 2 * RET_DK, 2 * RET_DV), _f32),
        pltpu.VMEM((N_EXPERTS, ts), _f32),
    ]
    return pl.pallas_call(
        _mixer_kernel,
        grid=(b, nt),
        in_specs=in_specs,
        out_specs=out_specs,
        out_shape=out_shape,
        scratch_shapes=scratch,
        compiler_params=pltpu.CompilerParams(
            dimension_semantics=("arbitrary", "arbitrary"), vmem_limit_bytes=VMEM_LIMIT),
        name="mixer_router",
    )(x, *operands)


def _sc_worker_id():
    return lax.axis_index("s") * SC_CORES + lax.axis_index("c")


def _sc_mesh():
    return plsc.VectorSubcoreMesh(core_axis_name="c", subcore_axis_name="s")


def _sc_dispatch(src, idx3, n_rows):
    t, d = src.shape
    n_win_total, _, win = idx3.shape
    n_win = n_win_total // SC_WORKERS

    @functools.partial(
        pl.kernel, mesh=_sc_mesh(),
        out_type=jax.ShapeDtypeStruct((n_rows, d), src.dtype),
        scratch_types=[pltpu.VMEM((2, win), jnp.int32), pltpu.VMEM((win, d), src.dtype)],
    )
    def dispatch(src_hbm, idx_hbm, out_hbm, idx_v, rows_v):
        wid = _sc_worker_id()

        @pl.loop(0, n_win)
        def _(i):
            w = wid * n_win + i
            off = pl.multiple_of(w * win, SUBLANES)
            pltpu.sync_copy(idx_hbm.at[w], idx_v)
            pltpu.sync_copy(src_hbm.at[pl.ds(off, win)], rows_v)
            pltpu.sync_copy(rows_v, out_hbm.at[idx_v.at[0]])
            pltpu.sync_copy(rows_v, out_hbm.at[idx_v.at[1]])

    return dispatch(src, idx3)


def _sc_gather(table, idx):
    n = idx.shape[0]
    d = table.shape[1]
    win = SC_WINDOW
    per_w = n // SC_WORKERS
    n_win = per_w // win

    @functools.partial(
        pl.kernel, mesh=_sc_mesh(),
        out_type=jax.ShapeDtypeStruct((n, d), table.dtype),
        scratch_types=[pltpu.VMEM((win,), jnp.int32), pltpu.VMEM((win, d), table.dtype),
                       pltpu.SemaphoreType.DMA],
    )
    def gather(table_hbm, idx_hbm, out_hbm, idx_v, rows_v, sem):
        base = _sc_worker_id() * per_w

        @pl.loop(0, n_win)
        def _(i):
            off = pl.multiple_of(base + i * win, SUBLANES)
            pltpu.sync_copy(idx_hbm.at[pl.ds(off, win)], idx_v)
            pltpu.async_copy(table_hbm.at[idx_v], rows_v, sem).wait()
            pltpu.sync_copy(rows_v, out_hbm.at[pl.ds(off, win)])

    return gather(table, idx)


def _expert_kernel(blk0_ref, nblk_ref, count_ref, xs_ref, wg_ref, wu_ref, wd_ref, y_ref,
                   xbuf, ybuf, wgb_ref, wub_ref, wdb_ref, hid_ref, sem_in, sem_out):
    e = pl.program_id(0)
    last = pl.num_programs(0) - 1
    blk0 = blk0_ref[e]
    nblk = nblk_ref[e]
    count = count_ref[e]
    total = blk0_ref[last] + nblk_ref[last]
    n_span = (total + EXPERT_SPAN - 1) // EXPERT_SPAN
    span_rows = EXPERT_SPAN * MOE_BLOCK
    half = xbuf.shape[2]

    def span_in(s):
        return pltpu.make_async_copy(xs_ref.at[pl.ds(s * span_rows, span_rows)], xbuf.at[s % 2], sem_in.at[s % 2])

    def span_out(s):
        return pltpu.make_async_copy(ybuf.at[s % 2], y_ref.at[pl.ds(s * span_rows, span_rows)], sem_out.at[s % 2])

    @pl.when(e == 0)
    def _():
        ybuf[...] = jnp.zeros_like(ybuf)

    @pl.when(jnp.logical_and(e == 0, n_span > 0))
    def _():
        span_in(0).start()

    wgb_ref[...] = wg_ref[...].astype(_bf16)
    wub_ref[...] = wu_ref[...].astype(_bf16)
    wdb_ref[...] = wd_ref[...].astype(_bf16)

    def place(g):
        return (g // EXPERT_SPAN) % 2, pl.multiple_of((g % EXPERT_SPAN) * MOE_BLOCK, MOE_BLOCK)

    def fetch(g):
        s = g // EXPERT_SPAN

        @pl.when(g % EXPERT_SPAN == 0)
        def _():
            span_in(s).wait()

            @pl.when(s + 1 < n_span)
            def _():
                span_in(s + 1).start()

    def reserve(g):
        s = g // EXPERT_SPAN

        @pl.when(jnp.logical_and(g % EXPERT_SPAN == 0, s >= 2))
        def _():
            span_out(s - 2).wait()

    def release(g):
        @pl.when(jnp.logical_or(g % EXPERT_SPAN == EXPERT_SPAN - 1, g == total - 1))
        def _():
            span_out(g // EXPERT_SPAN).start()

    def up_proj(g, j):
        slot, row0 = place(g)
        rowid = lax.broadcasted_iota(jnp.int32, (MOE_BLOCK, half), 0)
        xb = xbuf[slot, pl.ds(row0, MOE_BLOCK), :]
        lo, hi = _unpack_halves(jnp.where(rowid < count - j * MOE_BLOCK, xb, jnp.uint32(0)))
        hb = jnp.concatenate([lo.astype(_bf16), hi.astype(_bf16)], axis=1)
        gate = _dot(hb, wgb_ref[...])
        up = _dot(hb, wub_ref[...])
        return (gate * _sigmoid(gate) * up).astype(_bf16)

    def down_proj(g, hid_slot):
        slot, row0 = place(g)
        ybuf[slot, pl.ds(row0, MOE_BLOCK), :] = _pack_halves(_dot(hid_ref[hid_slot], wdb_ref[...]))

    @pl.when(nblk > 0)
    def _():
        fetch(blk0)
        hid_ref[0] = up_proj(blk0, 0)

    def steady(j, carry):
        g = blk0 + j
        fetch(g)
        reserve(g - 1)
        down_proj(g - 1, (j - 1) % 2)
        hid_ref[j % 2] = up_proj(g, j)
        release(g - 1)
        return carry

    lax.fori_loop(1, nblk, steady, 0)

    @pl.when(nblk > 0)
    def _():
        g = blk0 + nblk - 1
        reserve(g)
        down_proj(g, (nblk - 1) % 2)
        release(g)

    @pl.when(jnp.logical_and(e == last, n_span >= 2))
    def _():
        span_out(n_span - 2).wait()

    @pl.when(jnp.logical_and(e == last, n_span >= 1))
    def _():
        span_out(n_span - 1).wait()


def _experts(xs, w_gate, w_up, w_down, blk0, nblk, counts):
    n_rows, half = xs.shape
    d = 2 * half
    ne, _, de = w_gate.shape
    grid_spec = pltpu.PrefetchScalarGridSpec(
        num_scalar_prefetch=3,
        grid=(ne,),
        in_specs=[
            pl.BlockSpec(memory_space=pl.ANY),
            pl.BlockSpec((None, d, de), lambda e, *_: (e, 0, 0)),
            pl.BlockSpec((None, d, de), lambda e, *_: (e, 0, 0)),
            pl.BlockSpec((None, de, d), lambda e, *_: (e, 0, 0)),
        ],
        out_specs=pl.BlockSpec(memory_space=pl.ANY),
        scratch_shapes=[
            pltpu.VMEM((2, EXPERT_SPAN * MOE_BLOCK, half), jnp.uint32),
            pltpu.VMEM((2, EXPERT_SPAN * MOE_BLOCK, half), jnp.uint32),
            pltpu.VMEM((d, de), _bf16), pltpu.VMEM((d, de), _bf16), pltpu.VMEM((de, d), _bf16),
            pltpu.VMEM((2, MOE_BLOCK, de), _bf16),
            pltpu.SemaphoreType.DMA((2,)), pltpu.SemaphoreType.DMA((2,)),
        ],
    )
    return pl.pallas_call(
        _expert_kernel,
        grid_spec=grid_spec,
        out_shape=jax.ShapeDtypeStruct((n_rows, half), jnp.uint32),
        compiler_params=pltpu.CompilerParams(dimension_semantics=("arbitrary",), vmem_limit_bytes=VMEM_LIMIT),
        name="experts",
    )(blk0, nblk, counts, xs, w_gate, w_up, w_down)


def _combine_kernel(ew_ref, x1_ref, p_ref, y2_ref, gin_ref, wpg_ref, wpp_ref, gpost_ref, gfin_ref, *rest):
    out_ref = rest[-1]
    tc, d = x1_ref.shape
    x2 = x1_ref[...]
    for k in range(2):
        wcol = jnp.broadcast_to(ew_ref[k:k + 1, :], (LANES, tc)).T
        yk = jnp.concatenate(_unpack_halves(y2_ref[k]), axis=1)
        x2 = x2 + jnp.tile(wcol, (1, d // LANES)) * yk
    gate = _sigmoid(_dot(_rms(x2, gin_ref[...]).astype(_bf16), wpg_ref[...]))
    ple = _rms(_dot(p_ref[...].astype(_bf16), wpp_ref[...]), gpost_ref[...])
    out_ref[...] = _rms(x2 + gate * ple, gfin_ref[...])


def _combine(x1, p0, y2, ew, gains_and_weights, b0, b_total, prev_out):
    b, s, d = x1.shape
    tc = MIX_TILE
    nt = s // tc
    pdim = p0.shape[-1]
    vec = lambda: pl.BlockSpec((1, d), lambda bi, ti: (0, 0))
    in_specs = [
        pl.BlockSpec((None, 2, tc), lambda bi, ti: (bi * nt + ti, 0, 0)),
        pl.BlockSpec((None, tc, d), lambda bi, ti: (bi, ti, 0)),
        pl.BlockSpec((None, tc, pdim), lambda bi, ti: (b0 + bi, ti, 0)),
        pl.BlockSpec((None, 2, tc, d // 2), lambda bi, ti: (bi * nt + ti, 0, 0, 0)),
        vec(),
        pl.BlockSpec((d, d), lambda bi, ti: (0, 0)),
        pl.BlockSpec((pdim, d), lambda bi, ti: (0, 0)),
        vec(), vec(),
    ]
    args = [ew, x1, p0, y2, *gains_and_weights]
    aliases = {}
    if prev_out is not None:
        in_specs.append(pl.BlockSpec(memory_space=pl.ANY))
        aliases = {len(args): 0}
        args.append(prev_out)
    return pl.pallas_call(
        _combine_kernel,
        grid=(b, s // tc),
        in_specs=in_specs,
        out_specs=pl.BlockSpec((None, tc, d), lambda bi, ti: (b0 + bi, ti, 0)),
        out_shape=jax.ShapeDtypeStruct((b_total, s, d), _f32),
        input_output_aliases=aliases,
        compiler_params=pltpu.CompilerParams(
            dimension_semantics=("arbitrary", "arbitrary"), vmem_limit_bytes=VMEM_LIMIT),
        name="combine_ple",
    )(*args)


def _layer(x, p_i, g_mix, w_in, conv_w, g_ret, w_out_conv, w_out_ret, w_o, g_moe, w_rg, b_rg, w_re, b_re,
           w_exp_gate, w_exp_up, w_exp_down, g_ple_in, w_ple_gate, w_ple_proj, g_ple_post, g_out):
    b_total, s, d = x.shape
    assert b_total % sum(SLICE_PARTS) == 0
    sizes = [b_total // sum(SLICE_PARTS) * part for part in SLICE_PARTS]
    starts = [sum(sizes[:h]) for h in range(len(sizes))]
    operands = _mixer_operands(s, d, g_mix, w_in, conv_w, g_ret, w_out_conv, w_out_ret, w_o, g_moe,
                               w_rg, b_rg, w_re, b_re)
    tail = (g_ple_in.reshape(1, d), w_ple_gate.astype(_bf16), w_ple_proj.astype(_bf16),
            g_ple_post.reshape(1, d), g_out.reshape(1, d))
    win = SC_WINDOW

    def route(eid, rank, cnt, n_tok):
        counts = cnt[:, 0]
        padded = (counts + MOE_BLOCK - 1) // MOE_BLOCK * MOE_BLOCK
        pends = jnp.cumsum(padded)
        pstarts = pends - padded
        eids = jnp.arange(N_EXPERTS, dtype=jnp.int32).reshape(N_EXPERTS, 1, 1, 1)
        seg = jnp.sum(jnp.where(eid[None] == eids, pstarts.reshape(N_EXPERTS, 1, 1, 1), 0), axis=0)
        dest = rank + seg
        tiles, _, ts = dest.shape
        idx3 = dest.reshape(tiles, 2, ts // win, win).transpose(0, 2, 1, 3).reshape(n_tok // win, 2, win)
        return dest, idx3, pstarts // MOE_BLOCK, padded // MOE_BLOCK, counts

    st = [dict() for _ in sizes]
    out = None

    def mix(h):
        n_tok = sizes[h] * s
        nblk = (2 * n_tok + N_EXPERTS * (MOE_BLOCK - 1) + MOE_BLOCK - 1) // MOE_BLOCK
        nblk = (nblk + EXPERT_SPAN - 1) // EXPERT_SPAN * EXPERT_SPAN
        x1, h2p, eid, ew, rank, cnt = _mixer(x, starts[h], sizes[h], operands)
        dest, idx3, blk0, nblk_e, counts = route(eid, rank, cnt, n_tok)
        st[h].update(x1=x1, h2p=h2p, ew=ew, dest=dest, idx3=idx3, seg=(blk0, nblk_e, counts),
                     n_tok=n_tok, n_rows=nblk * MOE_BLOCK)

    def dispatch(h):
        st[h]["xs"] = _sc_dispatch(st[h]["h2p"].reshape(st[h]["n_tok"], d // 2), st[h]["idx3"], st[h]["n_rows"])

    def experts(h):
        st[h]["y"] = _experts(st[h]["xs"], w_exp_gate, w_exp_up, w_exp_down, *st[h]["seg"])

    def gather(h):
        y2 = _sc_gather(st[h]["y"], st[h]["dest"].reshape(-1))
        st[h]["y2"] = y2.reshape(st[h]["n_tok"] // MIX_TILE, 2, MIX_TILE, d // 2)

    def combine(h):
        nonlocal out
        out = _combine(st[h]["x1"], p_i, st[h]["y2"], st[h]["ew"], tail, starts[h], b_total, out)

    stages = (mix, dispatch, experts, gather, combine)
    for step in range(len(sizes) + len(stages) - 1):
        for k in reversed(range(len(stages))):
            if 0 <= step - k < len(sizes):
                stages[k](step - k)
    return out


def kernel(x, p, g_mix, w_in, conv_w, g_ret, w_out_conv, w_out_ret, w_o, g_moe, w_rg, b_rg, w_re, b_re, w_exp_gate, w_exp_up, w_exp_down, g_ple_in, w_ple_gate, w_ple_proj, g_ple_post, g_final):
    depth = p.shape[0]
    assert depth == 1, "the final norm is fused into the single layer's combine kernel"
    return _layer(x, p[0], g_mix[0], w_in[0], conv_w[0], g_ret[0], w_out_conv[0], w_out_ret[0], w_o[0],
                  g_moe[0], w_rg[0], b_rg[0], w_re[0], b_re[0], w_exp_gate[0], w_exp_up[0], w_exp_down[0],
                  g_ple_in[0], w_ple_gate[0], w_ple_proj[0], g_ple_post[0], g_final)
```

```python
import functools

import jax
import jax.numpy as jnp
import numpy as np
from jax import lax
from jax.experimental import pallas as pl
from jax.experimental.pallas import tpu as pltpu
from jax.experimental.pallas import tpu_sc as plsc

EPS = 1e-6
RET_HEADS = 8
RET_DK = 64
RET_DV = 128
RET_CHUNK = 128
ROPE_BASE = 10000.0
N_GROUPS = 4
EXPERTS_PER_GROUP = 8
N_EXPERTS = N_GROUPS * EXPERTS_PER_GROUP
MOE_BLOCK = 256
LANES = 128
SUBLANES = 8
ROUTER_ROWS = 48
MIX_TILE = 512
EXPERT_SPAN = 8
SLICE_PARTS = (5, 3)
SC_CORES = 2
SC_SUBCORES = 16
SC_WORKERS = SC_CORES * SC_SUBCORES
SC_WINDOW = 128
BF16_BITS = 16
HIGH_HALF = np.uint32(0xFFFF0000)
V7X_VMEM_BYTES = 64 * 1024 * 1024
VMEM_LIMIT = V7X_VMEM_BYTES // 8 * 7

_bf16 = jnp.bfloat16
_f32 = jnp.float32


def _sigmoid(v):
    return 0.5 * jnp.tanh(0.5 * v) + 0.5


def _rms(v, g):
    ms = jnp.mean(v * v, axis=-1, keepdims=True)
    return v * lax.rsqrt(ms + EPS) * g


def _dot(a, b):
    return jnp.dot(a, b, preferred_element_type=_f32)


def _pack_halves(v):
    bits = lax.bitcast_convert_type(v.astype(_bf16).astype(_f32), jnp.uint32)
    c = v.shape[1] // 2
    return (bits[:, :c] >> BF16_BITS) | (bits[:, c:] & HIGH_HALF)


def _unpack_halves(w):
    lo = lax.bitcast_convert_type(w << BF16_BITS, _f32)
    hi = lax.bitcast_convert_type(w & HIGH_HALF, _f32)
    return lo, hi


def _const_spec(shape):
    nd = len(shape)
    return pl.BlockSpec(shape, lambda *_: (0,) * nd, pipeline_mode=pl.Buffered(1))


def _mixer_kernel(x_ref, gmix_ref, win_ref, convw_ref, cos_ref, sin_ref, dmask_ref, qd_ref, kdt_ref,
                  sdec_ref, bmask_ref, gret_ref, woc_ref, wor_ref, wo_ref, gmoe_ref, wrt_ref, rbias_ref,
                  tri_ref,
                  x1_ref, h2p_ref, eid_ref, ew_ref, rank_ref, cnt_ref,
                  hb_ref, qr_ref, kr_ref, vb_ref, o_ref, og_ref, acc_ref, cuc_ref, state_ref, carry_ref):
    ts, d = x_ref.shape
    t = pl.program_id(1)
    n = pl.program_id(0) * pl.num_programs(1) + t
    router_refs = (wrt_ref, rbias_ref, tri_ref, carry_ref, eid_ref, ew_ref, rank_ref, cnt_ref)

    @pl.when(t == 0)
    def _():
        cuc_ref[...] = jnp.zeros_like(cuc_ref)
        state_ref[...] = jnp.zeros_like(state_ref)

    @pl.when(n == 0)
    def _():
        carry_ref[...] = jnp.zeros_like(carry_ref)

    x = x_ref[...]
    hb_ref[...] = _rms(x, gmix_ref[...]).astype(_bf16)
    hb = hb_ref[...]

    def proj(lo, hi):
        return _dot(hb, win_ref[:, lo:hi])

    cu = proj(d, 2 * d) * proj(0, d)
    prev = cuc_ref[...]
    p1 = prev[SUBLANES - 1:SUBLANES, :]
    p2 = prev[SUBLANES - 2:SUBLANES - 1, :]
    rows = lax.broadcasted_iota(jnp.int32, (ts, d), 0)
    s1 = jnp.where(rows == 0, p1, pltpu.roll(cu, 1, 0))
    s2 = jnp.where(rows == 0, p2, jnp.where(rows == 1, p1, pltpu.roll(cu, 2, 0)))
    conv = convw_ref[0:1, :] * s2 + convw_ref[1:2, :] * s1 + convw_ref[2:3, :] * cu
    cuc_ref[...] = cu[ts - SUBLANES:ts, :]
    a = (proj(2 * d, 3 * d) * conv).astype(_bf16)
    acc_ref[...] = _sigmoid(proj(6 * d, 7 * d)) * _dot(a, woc_ref[...])

    qk0 = 3 * d
    cosv = cos_ref[...]
    sinv = sin_ref[...]
    lane = lax.broadcasted_iota(jnp.int32, (ts, LANES), 1)
    first_half = (lane % RET_DK) < (RET_DK // 2)
    qk_w = RET_HEADS * RET_DK
    for dst, base, scale in ((qr_ref, qk0, None), (kr_ref, qk0 + qk_w, RET_DK ** -0.5)):
        zw = proj(base, base + qk_w)
        for g in range(qk_w // LANES):
            z = zw[:, g * LANES:(g + 1) * LANES]
            zs = jnp.where(first_half, pltpu.roll(z, LANES - RET_DK // 2, 1), pltpu.roll(z, RET_DK // 2, 1))
            r = z * cosv + zs * sinv
            if scale is not None:
                r = r * scale
            dst[:, g * LANES:(g + 1) * LANES] = r
    vb_ref[...] = proj(4 * d, 5 * d).astype(_bf16)

    c = RET_CHUNK
    lane_c = lax.broadcasted_iota(jnp.int32, (c, LANES), 1)
    even = lane_c < RET_DK
    for ci in range(ts // c):
        r0 = ci * c
        for j in range(RET_HEADS // 2):
            q2 = qr_ref[r0:r0 + c, j * LANES:(j + 1) * LANES]
            k2 = kr_ref[r0:r0 + c, j * LANES:(j + 1) * LANES]
            v2 = vb_ref[r0:r0 + c, 2 * j * RET_DV:(2 * j + 2) * RET_DV]
            kt = k2.T
            qq = jnp.concatenate([jnp.where(even, q2, 0.0), jnp.where(even, 0.0, q2)], axis=0).astype(_bf16)
            sc = _dot(qq, kt.astype(_bf16))
            pe = (sc[:c] * dmask_ref[2 * j]).astype(_bf16)
            po = (sc[c:] * dmask_ref[2 * j + 1]).astype(_bf16)
            inner = jnp.concatenate([_dot(pe, v2[:, :RET_DV]), _dot(po, v2[:, RET_DV:])], axis=1)
            st = state_ref[j]
            cross = _dot(q2.astype(_bf16), st.astype(_bf16)) * qd_ref[:, 2 * j * RET_DV:(2 * j + 2) * RET_DV]
            o_ref[r0:r0 + c, 2 * j * RET_DV:(2 * j + 2) * RET_DV] = inner + cross
            upd = _dot((kt * kdt_ref[j]).astype(_bf16), v2)
            state_ref[j] = st * sdec_ref[j] + upd * bmask_ref[...]

    for hp in range(RET_HEADS // 2):
        sg2 = proj(5 * d + 2 * hp * RET_DV, 5 * d + (2 * hp + 2) * RET_DV)
        for h in (2 * hp, 2 * hp + 1):
            sl = slice(h * RET_DV, (h + 1) * RET_DV)
            sg = sg2[:, (h - 2 * hp) * RET_DV:(h - 2 * hp + 1) * RET_DV]
            on = _rms(o_ref[:, sl], gret_ref[:, sl])
            og_ref[:, sl] = (sg * _sigmoid(sg) * on).astype(_bf16)
    yr = _dot(og_ref[...], wor_ref[...])
    mixed = acc_ref[...] + _sigmoid(proj(7 * d, 8 * d)) * yr
    x1 = x + _dot(mixed.astype(_bf16), wo_ref[...])
    x1_ref[...] = x1

    h2 = _rms(x1, gmoe_ref[...]).astype(_bf16)
    h2p_ref[...] = _pack_halves(h2)
    _router(h2, n, *router_refs)


def _router(h2, tile, wrt_ref, rbias_ref, tri_ref, carry_ref, eid_ref, ew_ref, rank_ref, cnt_ref):
    ts = h2.shape[0]
    lt = lax.dot_general(wrt_ref[...], h2, (((1,), (1,)), ((), ())), preferred_element_type=_f32)
    lt = lt + rbias_ref[...]
    g0, g1, g2, g3 = (lt[i:i + 1, :] for i in range(N_GROUPS))
    gmax = jnp.maximum(jnp.maximum(g0, g1), jnp.maximum(g2, g3))
    grp = jnp.where(g0 == gmax, 0, jnp.where(g1 == gmax, 1, jnp.where(g2 == gmax, 2, 3)))
    gsum = jnp.exp(g0 - gmax) + jnp.exp(g1 - gmax) + jnp.exp(g2 - gmax) + jnp.exp(g3 - gmax)
    g_w = 1.0 / gsum
    e_in = lt[SUBLANES:2 * SUBLANES, :]
    for g in range(1, N_GROUPS):
        e_in = jnp.where(grp == g, lt[(g + 1) * SUBLANES:(g + 2) * SUBLANES, :], e_in)
    ridx = lax.broadcasted_iota(jnp.int32, (EXPERTS_PER_GROUP, ts), 0)
    top1 = jnp.max(e_in, axis=0, keepdims=True)
    i1 = jnp.min(jnp.where(e_in == top1, ridx, EXPERTS_PER_GROUP), axis=0, keepdims=True)
    e_m = jnp.where(ridx == i1, -jnp.inf, e_in)
    top2 = jnp.max(e_m, axis=0, keepdims=True)
    i2 = jnp.min(jnp.where(e_m == top2, ridx, EXPERTS_PER_GROUP), axis=0, keepdims=True)
    ex = jnp.exp(top2 - top1)
    den = 1.0 + ex
    id0 = grp * EXPERTS_PER_GROUP + i1
    id1 = grp * EXPERTS_PER_GROUP + i2
    eid_ref[tile, 0:1, :] = id0
    eid_ref[tile, 1:2, :] = id1
    ew_ref[tile, 0:1, :] = (1.0 / den) * g_w
    ew_ref[tile, 1:2, :] = (ex / den) * g_w

    eidx = lax.broadcasted_iota(jnp.int32, (N_EXPERTS, ts), 0)
    oh0 = (eidx == id0).astype(_f32)
    oh1 = (eidx == id1).astype(_f32)
    cnt = (oh0 + oh1).astype(_bf16)
    before = carry_ref[...] + _dot(cnt, tri_ref[0])
    rank_ref[tile, 0:1, :] = jnp.sum(oh0 * before, axis=0, keepdims=True).astype(jnp.int32)
    rank_ref[tile, 1:2, :] = jnp.sum(oh1 * before, axis=0, keepdims=True).astype(jnp.int32)
    total = carry_ref[...] + _dot(cnt, tri_ref[1])
    carry_ref[...] = total
    cnt_ref[...] = total[:, :LANES].astype(jnp.int32)


def _retention_tables():
    c = RET_CHUNK
    log_gamma = np.log1p(-np.exp2(-5.0 - np.arange(RET_HEADS, dtype=np.float64)))
    pos = np.arange(c, dtype=np.float64)
    diff = pos[:, None] - pos[None, :]
    dmask = np.where((diff >= 0)[None], np.exp(log_gamma[:, None, None] * np.maximum(diff, 0.0)[None]), 0.0)
    q_decay = np.exp(log_gamma[:, None] * (pos[None, :] + 1.0))
    k_decay = np.exp(log_gamma[:, None] * (c - 1.0 - pos[None, :]))
    chunk_decay = np.exp(log_gamma * c)
    qd = np.repeat(q_decay.T, RET_DV, axis=1)
    kdt = np.repeat(k_decay.reshape(RET_HEADS // 2, 2, 1, c), RET_DK, axis=2).reshape(RET_HEADS // 2, 2 * RET_DK, c)
    sdec = np.repeat(chunk_decay.reshape(RET_HEADS // 2, 2, 1), RET_DK, axis=2).reshape(RET_HEADS // 2, 2 * RET_DK, 1)
    sdec = np.broadcast_to(sdec, (RET_HEADS // 2, 2 * RET_DK, 2 * RET_DV))
    rr = np.arange(2 * RET_DK)[:, None] // RET_DK
    cc = np.arange(2 * RET_DV)[None, :] // RET_DV
    bmask = (rr == cc).astype(np.float64)
    return tuple(jnp.asarray(np.ascontiguousarray(v), _f32) for v in (dmask, qd, kdt, sdec, bmask))


def _rope_tables(s_len):
    inv = ROPE_BASE ** (-jnp.arange(0, RET_DK, 2, dtype=_f32) / RET_DK)
    ang = jnp.arange(s_len, dtype=_f32)[:, None] * inv[None, :]
    cos, sin = jnp.cos(ang), jnp.sin(ang)
    cos_t = jnp.tile(cos, (1, LANES // (RET_DK // 2)))
    sin_t = jnp.tile(jnp.concatenate([-sin, sin], axis=1), (1, LANES // RET_DK))
    return cos_t, sin_t


def _mixer_operands(s, d, g_mix, w_in, conv_w, g_ret, w_out_conv, w_out_ret, w_o, g_moe, w_rg, b_rg, w_re, b_re):
    ts = MIX_TILE
    cos_t, sin_t = _rope_tables(s)
    dmask, qd, kdt, sdec, bmask = _retention_tables()
    wrt = jnp.zeros((ROUTER_ROWS, d), _f32)
    wrt = wrt.at[:N_GROUPS].set(w_rg.T).at[SUBLANES:SUBLANES + N_EXPERTS].set(w_re.T).astype(_bf16)
    rb = jnp.zeros((ROUTER_ROWS,), _f32).at[:N_GROUPS].set(b_rg).at[SUBLANES:SUBLANES + N_EXPERTS].set(b_re)
    rbias = jnp.broadcast_to(rb[:, None], (ROUTER_ROWS, ts))
    ii = np.arange(ts)
    tri = jnp.asarray(np.stack([(ii[:, None] < ii[None, :]), np.ones((ts, ts), bool)]), _bf16)
    return (g_mix.reshape(1, d), w_in.astype(_bf16), conv_w, cos_t, sin_t, dmask, qd, kdt, sdec, bmask,
            g_ret.reshape(1, d), w_out_conv.astype(_bf16), w_out_ret.astype(_bf16), w_o.astype(_bf16),
            g_moe.reshape(1, d), wrt, rbias, tri)


def _mixer(x, b0, b, operands):
    _, s, d = x.shape
    ts = MIX_TILE
    assert s % ts == 0 and ts % RET_CHUNK == 0 and ts % SC_WINDOW == 0
    nt = s // ts
    tile3 = lambda w: pl.BlockSpec((None, ts, w), lambda bi, ti: (bi, ti, 0))
    route = lambda: pl.BlockSpec((b * nt, 2, ts), lambda bi, ti: (0, 0, 0))
    rope_spec = pl.BlockSpec((ts, LANES), lambda bi, ti: (ti, 0))
    in_specs = [pl.BlockSpec((None, ts, d), lambda bi, ti: (b0 + bi, ti, 0))]
    in_specs += [rope_spec if i in (3, 4) else _const_spec(op.shape) for i, op in enumerate(operands)]
    out_shape = [
        jax.ShapeDtypeStruct((b, s, d), _f32),
        jax.ShapeDtypeStruct((b, s, d // 2), jnp.uint32),
        jax.ShapeDtypeStruct((b * nt, 2, ts), jnp.int32),
        jax.ShapeDtypeStruct((b * nt, 2, ts), _f32),
        jax.ShapeDtypeStruct((b * nt, 2, ts), jnp.int32),
        jax.ShapeDtypeStruct((N_EXPERTS, LANES), jnp.int32),
    ]
    out_specs = [tile3(d), tile3(d // 2), route(), route(), route(),
                 pl.BlockSpec((N_EXPERTS, LANES), lambda bi, ti: (0, 0))]
    scratch = [
        pltpu.VMEM((ts, d), _bf16),
        pltpu.VMEM((ts, RET_HEADS * RET_DK), _f32),
        pltpu.VMEM((ts, RET_HEADS * RET_DK), _f32),
        pltpu.VMEM((ts, RET_HEADS * RET_DV), _bf16),
        pltpu.VMEM((ts, RET_HEADS * RET_DV), _f32),
        pltpu.VMEM((ts, RET_HEADS * RET_DV), _bf16),
        pltpu.VMEM((ts, d), _f32),
        pltpu.VMEM((SUBLANES, d), _f32),
        pltpu.VMEM((RET_HEADS // 2, 2 * RET_DK, 2 * RET_DV), _f32),
        pltpu.VMEM((N_EXPERTS, ts), _f32),
    ]
    return pl.pallas_call(
        _mixer_kernel,
        grid=(b, nt),
        in_specs=in_specs,
        out_specs=out_specs,
        out_shape=out_shape,
        scratch_shapes=scratch,
        compiler_params=pltpu.CompilerParams(
            dimension_semantics=("arbitrary", "arbitrary"), vmem_limit_bytes=VMEM_LIMIT),
        name="mixer_router",
    )(x, *operands)


def _sc_worker_id():
    return lax.axis_index("s") * SC_CORES + lax.axis_index("c")


def _sc_mesh():
    return plsc.VectorSubcoreMesh(core_axis_name="c", subcore_axis_name="s")


def _sc_dispatch(src, idx3, n_rows):
    t, d = src.shape
    n_win_total, _, win = idx3.shape
    assert t == n_win_total * win and n_win_total % SC_WORKERS == 0 and win <= LANES
    n_win = n_win_total // SC_WORKERS

    @functools.partial(
        pl.kernel, mesh=_sc_mesh(),
        out_type=jax.ShapeDtypeStruct((n_rows, d), src.dtype),
        scratch_types=[pltpu.VMEM((2, win), jnp.int32), pltpu.VMEM((win, d), src.dtype)],
    )
    def dispatch(src_hbm, idx_hbm, out_hbm, idx_v, rows_v):
        wid = _sc_worker_id()

        @pl.loop(0, n_win)
        def _(i):
            w = wid * n_win + i
            off = pl.multiple_of(w * win, SUBLANES)
            pltpu.sync_copy(idx_hbm.at[w], idx_v)
            pltpu.sync_copy(src_hbm.at[pl.ds(off, win)], rows_v)
            pltpu.sync_copy(rows_v, out_hbm.at[idx_v.at[0]])
            pltpu.sync_copy(rows_v, out_hbm.at[idx_v.at[1]])

    return dispatch(src, idx3)


def _sc_gather(table, idx):
    n = idx.shape[0]
    d = table.shape[1]
    win = SC_WINDOW
    assert n % (SC_WORKERS * win) == 0
    per_w = n // SC_WORKERS
    n_win = per_w // win

    @functools.partial(
        pl.kernel, mesh=_sc_mesh(),
        out_type=jax.ShapeDtypeStruct((n, d), table.dtype),
        scratch_types=[pltpu.VMEM((win,), jnp.int32), pltpu.VMEM((win, d), table.dtype),
                       pltpu.SemaphoreType.DMA],
    )
    def gather(table_hbm, idx_hbm, out_hbm, idx_v, rows_v, sem):
        base = _sc_worker_id() * per_w

        @pl.loop(0, n_win)
        def _(i):
            off = pl.multiple_of(base + i * win, SUBLANES)
            pltpu.sync_copy(idx_hbm.at[pl.ds(off, win)], idx_v)
            pltpu.async_copy(table_hbm.at[idx_v], rows_v, sem).wait()
            pltpu.sync_copy(rows_v, out_hbm.at[pl.ds(off, win)])

    return gather(table, idx)


def _expert_kernel(blk0_ref, nblk_ref, count_ref, xs_ref, wg_ref, wu_ref, wd_ref, y_ref,
                   xbuf, ybuf, wgb_ref, wub_ref, wdb_ref, hid_ref, sem_in, sem_out):
    e = pl.program_id(0)
    last = pl.num_programs(0) - 1
    blk0 = blk0_ref[e]
    nblk = nblk_ref[e]
    count = count_ref[e]
    total = blk0_ref[last] + nblk_ref[last]
    n_span = (total + EXPERT_SPAN - 1) // EXPERT_SPAN
    span_rows = EXPERT_SPAN * MOE_BLOCK
    half = xbuf.shape[2]

    def span_in(s):
        return pltpu.make_async_copy(xs_ref.at[pl.ds(s * span_rows, span_rows)], xbuf.at[s % 2], sem_in.at[s % 2])

    def span_out(s):
        return pltpu.make_async_copy(ybuf.at[s % 2], y_ref.at[pl.ds(s * span_rows, span_rows)], sem_out.at[s % 2])

    @pl.when(e == 0)
    def _():
        ybuf[...] = jnp.zeros_like(ybuf)

    @pl.when(jnp.logical_and(e == 0, n_span > 0))
    def _():
        span_in(0).start()

    wgb_ref[...] = wg_ref[...].astype(_bf16)
    wub_ref[...] = wu_ref[...].astype(_bf16)
    wdb_ref[...] = wd_ref[...].astype(_bf16)

    def place(g):
        return (g // EXPERT_SPAN) % 2, pl.multiple_of((g % EXPERT_SPAN) * MOE_BLOCK, MOE_BLOCK)

    def fetch(g):
        s = g // EXPERT_SPAN

        @pl.when(g % EXPERT_SPAN == 0)
        def _():
            span_in(s).wait()

            @pl.when(s + 1 < n_span)
            def _():
                span_in(s + 1).start()

    def reserve(g):
        s = g // EXPERT_SPAN

        @pl.when(jnp.logical_and(g % EXPERT_SPAN == 0, s >= 2))
        def _():
            span_out(s - 2).wait()

    def release(g):
        @pl.when(jnp.logical_or(g % EXPERT_SPAN == EXPERT_SPAN - 1, g == total - 1))
        def _():
            span_out(g // EXPERT_SPAN).start()

    def up_proj(g, j):
        slot, row0 = place(g)
        rowid = lax.broadcasted_iota(jnp.int32, (MOE_BLOCK, half), 0)
        xb = xbuf[slot, pl.ds(row0, MOE_BLOCK), :]
        lo, hi = _unpack_halves(jnp.where(rowid < count - j * MOE_BLOCK, xb, jnp.uint32(0)))
        hb = jnp.concatenate([lo.astype(_bf16), hi.astype(_bf16)], axis=1)
        gate = _dot(hb, wgb_ref[...])
        up = _dot(hb, wub_ref[...])
        return (gate * _sigmoid(gate) * up).astype(_bf16)

    def down_proj(g, hid_slot):
        slot, row0 = place(g)
        ybuf[slot, pl.ds(row0, MOE_BLOCK), :] = _pack_halves(_dot(hid_ref[hid_slot], wdb_ref[...]))

    @pl.when(nblk > 0)
    def _():
        fetch(blk0)
        hid_ref[0] = up_proj(blk0, 0)

    def steady(j, carry):
        g = blk0 + j
        fetch(g)
        reserve(g - 1)
        down_proj(g - 1, (j - 1) % 2)
        hid_ref[j % 2] = up_proj(g, j)
        release(g - 1)
        return carry

    lax.fori_loop(1, nblk, steady, 0)

    @pl.when(nblk > 0)
    def _():
        g = blk0 + nblk - 1
        reserve(g)
        down_proj(g, (nblk - 1) % 2)
        release(g)

    @pl.when(jnp.logical_and(e == last, n_span >= 2))
    def _():
        span_out(n_span - 2).wait()

    @pl.when(jnp.logical_and(e == last, n_span >= 1))
    def _():
        span_out(n_span - 1).wait()


def _experts(xs, w_gate, w_up, w_down, blk0, nblk, counts):
    n_rows, half = xs.shape
    d = 2 * half
    ne, _, de = w_gate.shape
    grid_spec = pltpu.PrefetchScalarGridSpec(
        num_scalar_prefetch=3,
        grid=(ne,),
        in_specs=[
            pl.BlockSpec(memory_space=pl.ANY),
            pl.BlockSpec((None, d, de), lambda e, *_: (e, 0, 0)),
            pl.BlockSpec((None, d, de), lambda e, *_: (e, 0, 0)),
            pl.BlockSpec((None, de, d), lambda e, *_: (e, 0, 0)),
        ],
        out_specs=pl.BlockSpec(memory_space=pl.ANY),
        scratch_shapes=[
            pltpu.VMEM((2, EXPERT_SPAN * MOE_BLOCK, half), jnp.uint32),
            pltpu.VMEM((2, EXPERT_SPAN * MOE_BLOCK, half), jnp.uint32),
            pltpu.VMEM((d, de), _bf16), pltpu.VMEM((d, de), _bf16), pltpu.VMEM((de, d), _bf16),
            pltpu.VMEM((2, MOE_BLOCK, de), _bf16),
            pltpu.SemaphoreType.DMA((2,)), pltpu.SemaphoreType.DMA((2,)),
        ],
    )
    return pl.pallas_call(
        _expert_kernel,
        grid_spec=grid_spec,
        out_shape=jax.ShapeDtypeStruct((n_rows, half), jnp.uint32),
        compiler_params=pltpu.CompilerParams(dimension_semantics=("arbitrary",), vmem_limit_bytes=VMEM_LIMIT),
        name="experts",
    )(blk0, nblk, counts, xs, w_gate, w_up, w_down)


def _combine_kernel(ew_ref, x1_ref, p_ref, y2_ref, gin_ref, wpg_ref, wpp_ref, gpost_ref, gfin_ref, *rest):
    out_ref = rest[-1]
    tc, d = x1_ref.shape
    x2 = x1_ref[...]
    for k in range(2):
        wcol = jnp.broadcast_to(ew_ref[k:k + 1, :], (LANES, tc)).T
        yk = jnp.concatenate(_unpack_halves(y2_ref[k]), axis=1)
        x2 = x2 + jnp.tile(wcol, (1, d // LANES)) * yk
    gate = _sigmoid(_dot(_rms(x2, gin_ref[...]).astype(_bf16), wpg_ref[...]))
    ple = _rms(_dot(p_ref[...].astype(_bf16), wpp_ref[...]), gpost_ref[...])
    out_ref[...] = _rms(x2 + gate * ple, gfin_ref[...])


def _combine(x1, p0, y2, ew, gains_and_weights, b0, b_total, prev_out):
    b, s, d = x1.shape
    tc = MIX_TILE
    nt = s // tc
    pdim = p0.shape[-1]
    vec = lambda: pl.BlockSpec((1, d), lambda bi, ti: (0, 0))
    in_specs = [
        pl.BlockSpec((None, 2, tc), lambda bi, ti: (bi * nt + ti, 0, 0)),
        pl.BlockSpec((None, tc, d), lambda bi, ti: (bi, ti, 0)),
        pl.BlockSpec((None, tc, pdim), lambda bi, ti: (b0 + bi, ti, 0)),
        pl.BlockSpec((None, 2, tc, d // 2), lambda bi, ti: (bi * nt + ti, 0, 0, 0)),
        vec(),
        pl.BlockSpec((d, d), lambda bi, ti: (0, 0)),
        pl.BlockSpec((pdim, d), lambda bi, ti: (0, 0)),
        vec(), vec(),
    ]
    args = [ew, x1, p0, y2, *gains_and_weights]
    aliases = {}
    if prev_out is not None:
        in_specs.append(pl.BlockSpec(memory_space=pl.ANY))
        aliases = {len(args): 0}
        args.append(prev_out)
    return pl.pallas_call(
        _combine_kernel,
        grid=(b, s // tc),
        in_specs=in_specs,
        out_specs=pl.BlockSpec((None, tc, d), lambda bi, ti: (b0 + bi, ti, 0)),
        out_shape=jax.ShapeDtypeStruct((b_total, s, d), _f32),
        input_output_aliases=aliases,
        compiler_params=pltpu.CompilerParams(
            dimension_semantics=("arbitrary", "arbitrary"), vmem_limit_bytes=VMEM_LIMIT),
        name="combine_ple",
    )(*args)


def _layer(x, p_i, g_mix, w_in, conv_w, g_ret, w_out_conv, w_out_ret, w_o, g_moe, w_rg, b_rg, w_re, b_re,
           w_exp_gate, w_exp_up, w_exp_down, g_ple_in, w_ple_gate, w_ple_proj, g_ple_post, g_out):
    b_total, s, d = x.shape
    assert b_total % sum(SLICE_PARTS) == 0
    sizes = [b_total // sum(SLICE_PARTS) * part for part in SLICE_PARTS]
    starts = [sum(sizes[:h]) for h in range(len(sizes))]
    operands = _mixer_operands(s, d, g_mix, w_in, conv_w, g_ret, w_out_conv, w_out_ret, w_o, g_moe,
                               w_rg, b_rg, w_re, b_re)
    tail = (g_ple_in.reshape(1, d), w_ple_gate.astype(_bf16), w_ple_proj.astype(_bf16),
            g_ple_post.reshape(1, d), g_out.reshape(1, d))
    win = SC_WINDOW

    def route(eid, rank, cnt, n_tok):
        counts = cnt[:, 0]
        padded = (counts + MOE_BLOCK - 1) // MOE_BLOCK * MOE_BLOCK
        pends = jnp.cumsum(padded)
        pstarts = pends - padded
        eids = jnp.arange(N_EXPERTS, dtype=jnp.int32).reshape(N_EXPERTS, 1, 1, 1)
        seg = jnp.sum(jnp.where(eid[None] == eids, pstarts.reshape(N_EXPERTS, 1, 1, 1), 0), axis=0)
        dest = rank + seg
        tiles, _, ts = dest.shape
        idx3 = dest.reshape(tiles, 2, ts // win, win).transpose(0, 2, 1, 3).reshape(n_tok // win, 2, win)
        return dest, idx3, pstarts // MOE_BLOCK, padded // MOE_BLOCK, counts

    st = [dict() for _ in sizes]
    out = None

    def mix(h):
        n_tok = sizes[h] * s
        nblk = (2 * n_tok + N_EXPERTS * (MOE_BLOCK - 1) + MOE_BLOCK - 1) // MOE_BLOCK
        nblk = (nblk + EXPERT_SPAN - 1) // EXPERT_SPAN * EXPERT_SPAN
        x1, h2p, eid, ew, rank, cnt = _mixer(x, starts[h], sizes[h], operands)
        dest, idx3, blk0, nblk_e, counts = route(eid, rank, cnt, n_tok)
        st[h].update(x1=x1, h2p=h2p, ew=ew, dest=dest, idx3=idx3, seg=(blk0, nblk_e, counts),
                     n_tok=n_tok, n_rows=nblk * MOE_BLOCK)

    def dispatch(h):
        st[h]["xs"] = _sc_dispatch(st[h]["h2p"].reshape(st[h]["n_tok"], d // 2), st[h]["idx3"], st[h]["n_rows"])

    def experts(h):
        st[h]["y"] = _experts(st[h]["xs"], w_exp_gate, w_exp_up, w_exp_down, *st[h]["seg"])

    def gather(h):
        y2 = _sc_gather(st[h]["y"], st[h]["dest"].reshape(-1))
        st[h]["y2"] = y2.reshape(st[h]["n_tok"] // MIX_TILE, 2, MIX_TILE, d // 2)

    def combine(h):
        nonlocal out
        out = _combine(st[h]["x1"], p_i, st[h]["y2"], st[h]["ew"], tail, starts[h], b_total, out)

    stages = (mix, dispatch, experts, gather, combine)
    for step in range(len(sizes) + len(stages) - 1):
        for k in reversed(range(len(stages))):
            if 0 <= step - k < len(sizes):
                stages[k](step - k)
    return out


def kernel(x, p, g_mix, w_in, conv_w, g_ret, w_out_conv, w_out_ret, w_o, g_moe, w_rg, b_rg, w_re, b_re, w_exp_gate, w_exp_up, w_exp_down, g_ple_in, w_ple_gate, w_ple_proj, g_ple_post, g_final):
    depth = p.shape[0]
    assert depth == 1, "the final norm is fused into the single layer's combine kernel"
    return _layer(x, p[0], g_mix[0], w_in[0], conv_w[0], g_ret[0], w_out_conv[0], w_out_ret[0], w_o[0],
                  g_moe[0], w_rg[0], b_rg[0], w_re[0], b_re[0], w_exp_gate[0], w_exp_up[0], w_exp_down[0],
                  g_ple_in[0], w_ple_gate[0], w_ple_proj[0], g_ple_post[0], g_final)
```

```python
import functools

import jax
import jax.numpy as jnp
import numpy as np
from jax import lax
from jax.experimental import pallas as pl
from jax.experimental.pallas import tpu as pltpu
from jax.experimental.pallas import tpu_sc as plsc

EPS = 1e-6
RET_HEADS = 8
RET_DK = 64
RET_DV = 128
RET_CHUNK = 128
ROPE_BASE = 10000.0
N_GROUPS = 4
EXPERTS_PER_GROUP = 8
N_EXPERTS = N_GROUPS * EXPERTS_PER_GROUP
MOE_BLOCK = 256
LANES = 128
SUBLANES = 8
ROUTER_ROWS = 48
MIX_TILE = 512
COMBINE_TILES = 2
EXPERT_SPAN = 8
SLICE_PARTS = (5, 3)
SC_CORES = 2
SC_SUBCORES = 16
SC_WORKERS = SC_CORES * SC_SUBCORES
SC_WINDOW = 128
BF16_BITS = 16
HIGH_HALF = np.uint32(0xFFFF0000)
V7X_VMEM_BYTES = 64 * 1024 * 1024
VMEM_LIMIT = V7X_VMEM_BYTES // 8 * 7

_bf16 = jnp.bfloat16
_f32 = jnp.float32


def _sigmoid(v):
    return 0.5 * jnp.tanh(0.5 * v) + 0.5


def _rms(v, g):
    ms = jnp.mean(v * v, axis=-1, keepdims=True)
    return v * lax.rsqrt(ms + EPS) * g


def _dot(a, b):
    return jnp.dot(a, b, preferred_element_type=_f32)


def _pack_halves(v):
    bits = lax.bitcast_convert_type(v.astype(_bf16).astype(_f32), jnp.uint32)
    c = v.shape[1] // 2
    return (bits[:, :c] >> BF16_BITS) | (bits[:, c:] & HIGH_HALF)


def _unpack_halves(w):
    lo = lax.bitcast_convert_type(w << BF16_BITS, _f32)
    hi = lax.bitcast_convert_type(w & HIGH_HALF, _f32)
    return lo, hi


def _const_spec(shape):
    nd = len(shape)
    return pl.BlockSpec(shape, lambda *_: (0,) * nd, pipeline_mode=pl.Buffered(1))


def _mixer_kernel(x_ref, gmix_ref, win_ref, convw_ref, cos_ref, sin_ref, dmask_ref, qd_ref, kdt_ref,
                  sdec_ref, bmask_ref, gret_ref, woc_ref, wor_ref, wo_ref, gmoe_ref, wrt_ref, rbias_ref,
                  tri_ref,
                  x1_ref, h2p_ref, eid_ref, ew_ref, rank_ref, cnt_ref,
                  hb_ref, qr_ref, kr_ref, vb_ref, o_ref, og_ref, acc_ref, cuc_ref, state_ref, carry_ref):
    ts, d = x_ref.shape
    t = pl.program_id(1)
    n = pl.program_id(0) * pl.num_programs(1) + t
    router_refs = (wrt_ref, rbias_ref, tri_ref, carry_ref, eid_ref, ew_ref, rank_ref, cnt_ref)

    @pl.when(t == 0)
    def _():
        cuc_ref[...] = jnp.zeros_like(cuc_ref)
        state_ref[...] = jnp.zeros_like(state_ref)

    @pl.when(n == 0)
    def _():
        carry_ref[...] = jnp.zeros_like(carry_ref)

    x = x_ref[...]
    hb_ref[...] = _rms(x, gmix_ref[...]).astype(_bf16)
    hb = hb_ref[...]

    def proj(lo, hi):
        return _dot(hb, win_ref[:, lo:hi])

    cu = proj(d, 2 * d) * proj(0, d)
    prev = cuc_ref[...]
    p1 = prev[SUBLANES - 1:SUBLANES, :]
    p2 = prev[SUBLANES - 2:SUBLANES - 1, :]
    rows = lax.broadcasted_iota(jnp.int32, (ts, d), 0)
    s1 = jnp.where(rows == 0, p1, pltpu.roll(cu, 1, 0))
    s2 = jnp.where(rows == 0, p2, jnp.where(rows == 1, p1, pltpu.roll(cu, 2, 0)))
    conv = convw_ref[0:1, :] * s2 + convw_ref[1:2, :] * s1 + convw_ref[2:3, :] * cu
    cuc_ref[...] = cu[ts - SUBLANES:ts, :]
    a = (proj(2 * d, 3 * d) * conv).astype(_bf16)
    acc_ref[...] = _sigmoid(proj(6 * d, 7 * d)) * _dot(a, woc_ref[...])

    qk0 = 3 * d
    cosv = cos_ref[...]
    sinv = sin_ref[...]
    lane = lax.broadcasted_iota(jnp.int32, (ts, LANES), 1)
    first_half = (lane % RET_DK) < (RET_DK // 2)
    qk_w = RET_HEADS * RET_DK
    for dst, base, scale in ((qr_ref, qk0, None), (kr_ref, qk0 + qk_w, RET_DK ** -0.5)):
        zw = proj(base, base + qk_w)
        for g in range(qk_w // LANES):
            z = zw[:, g * LANES:(g + 1) * LANES]
            zs = jnp.where(first_half, pltpu.roll(z, LANES - RET_DK // 2, 1), pltpu.roll(z, RET_DK // 2, 1))
            r = z * cosv + zs * sinv
            if scale is not None:
                r = r * scale
            dst[:, g * LANES:(g + 1) * LANES] = r
    vb_ref[...] = proj(4 * d, 5 * d).astype(_bf16)

    c = RET_CHUNK
    lane_c = lax.broadcasted_iota(jnp.int32, (c, LANES), 1)
    even = lane_c < RET_DK
    for ci in range(ts // c):
        r0 = ci * c
        for j in range(RET_HEADS // 2):
            q2 = qr_ref[r0:r0 + c, j * LANES:(j + 1) * LANES]
            k2 = kr_ref[r0:r0 + c, j * LANES:(j + 1) * LANES]
            v2 = vb_ref[r0:r0 + c, 2 * j * RET_DV:(2 * j + 2) * RET_DV]
            kt = k2.T
            qq = jnp.concatenate([jnp.where(even, q2, 0.0), jnp.where(even, 0.0, q2)], axis=0).astype(_bf16)
            sc = _dot(qq, kt.astype(_bf16))
            pe = (sc[:c] * dmask_ref[2 * j]).astype(_bf16)
            po = (sc[c:] * dmask_ref[2 * j + 1]).astype(_bf16)
            inner = jnp.concatenate([_dot(pe, v2[:, :RET_DV]), _dot(po, v2[:, RET_DV:])], axis=1)
            st = state_ref[j]
            cross = _dot(q2.astype(_bf16), st.astype(_bf16)) * qd_ref[:, 2 * j * RET_DV:(2 * j + 2) * RET_DV]
            o_ref[r0:r0 + c, 2 * j * RET_DV:(2 * j + 2) * RET_DV] = inner + cross
            upd = _dot((kt * kdt_ref[j]).astype(_bf16), v2)
            state_ref[j] = st * sdec_ref[j] + upd * bmask_ref[...]

    for hp in range(RET_HEADS // 2):
        sg2 = proj(5 * d + 2 * hp * RET_DV, 5 * d + (2 * hp + 2) * RET_DV)
        for h in (2 * hp, 2 * hp + 1):
            sl = slice(h * RET_DV, (h + 1) * RET_DV)
            sg = sg2[:, (h - 2 * hp) * RET_DV:(h - 2 * hp + 1) * RET_DV]
            on = _rms(o_ref[:, sl], gret_ref[:, sl])
            og_ref[:, sl] = (sg * _sigmoid(sg) * on).astype(_bf16)
    yr = _dot(og_ref[...], wor_ref[...])
    mixed = acc_ref[...] + _sigmoid(proj(7 * d, 8 * d)) * yr
    x1 = x + _dot(mixed.astype(_bf16), wo_ref[...])
    x1_ref[...] = x1

    h2 = _rms(x1, gmoe_ref[...]).astype(_bf16)
    h2p_ref[...] = _pack_halves(h2)
    _router(h2, n, *router_refs)


def _router(h2, tile, wrt_ref, rbias_ref, tri_ref, carry_ref, eid_ref, ew_ref, rank_ref, cnt_ref):
    ts = h2.shape[0]
    lt = lax.dot_general(wrt_ref[...], h2, (((1,), (1,)), ((), ())), preferred_element_type=_f32)
    lt = lt + rbias_ref[...]
    g0, g1, g2, g3 = (lt[i:i + 1, :] for i in range(N_GROUPS))
    gmax = jnp.maximum(jnp.maximum(g0, g1), jnp.maximum(g2, g3))
    grp = jnp.where(g0 == gmax, 0, jnp.where(g1 == gmax, 1, jnp.where(g2 == gmax, 2, 3)))
    gsum = jnp.exp(g0 - gmax) + jnp.exp(g1 - gmax) + jnp.exp(g2 - gmax) + jnp.exp(g3 - gmax)
    g_w = 1.0 / gsum
    e_in = lt[SUBLANES:2 * SUBLANES, :]
    for g in range(1, N_GROUPS):
        e_in = jnp.where(grp == g, lt[(g + 1) * SUBLANES:(g + 2) * SUBLANES, :], e_in)
    ridx = lax.broadcasted_iota(jnp.int32, (EXPERTS_PER_GROUP, ts), 0)
    top1 = jnp.max(e_in, axis=0, keepdims=True)
    i1 = jnp.min(jnp.where(e_in == top1, ridx, EXPERTS_PER_GROUP), axis=0, keepdims=True)
    e_m = jnp.where(ridx == i1, -jnp.inf, e_in)
    top2 = jnp.max(e_m, axis=0, keepdims=True)
    i2 = jnp.min(jnp.where(e_m == top2, ridx, EXPERTS_PER_GROUP), axis=0, keepdims=True)
    ex = jnp.exp(top2 - top1)
    den = 1.0 + ex
    id0 = grp * EXPERTS_PER_GROUP + i1
    id1 = grp * EXPERTS_PER_GROUP + i2
    eid_ref[tile, 0:1, :] = id0
    eid_ref[tile, 1:2, :] = id1
    ew_ref[tile, 0:1, :] = (1.0 / den) * g_w
    ew_ref[tile, 1:2, :] = (ex / den) * g_w

    eidx = lax.broadcasted_iota(jnp.int32, (N_EXPERTS, ts), 0)
    oh0 = (eidx == id0).astype(_f32)
    oh1 = (eidx == id1).astype(_f32)
    cnt = (oh0 + oh1).astype(_bf16)
    before = carry_ref[...] + _dot(cnt, tri_ref[0])
    rank_ref[tile, 0:1, :] = jnp.sum(oh0 * before, axis=0, keepdims=True).astype(jnp.int32)
    rank_ref[tile, 1:2, :] = jnp.sum(oh1 * before, axis=0, keepdims=True).astype(jnp.int32)
    total = carry_ref[...] + _dot(cnt, tri_ref[1])
    carry_ref[...] = total
    cnt_ref[...] = total[:, :LANES].astype(jnp.int32)


def _retention_tables():
    c = RET_CHUNK
    log_gamma = np.log1p(-np.exp2(-5.0 - np.arange(RET_HEADS, dtype=np.float64)))
    pos = np.arange(c, dtype=np.float64)
    diff = pos[:, None] - pos[None, :]
    dmask = np.where((diff >= 0)[None], np.exp(log_gamma[:, None, None] * np.maximum(diff, 0.0)[None]), 0.0)
    q_decay = np.exp(log_gamma[:, None] * (pos[None, :] + 1.0))
    k_decay = np.exp(log_gamma[:, None] * (c - 1.0 - pos[None, :]))
    chunk_decay = np.exp(log_gamma * c)
    qd = np.repeat(q_decay.T, RET_DV, axis=1)
    kdt = np.repeat(k_decay.reshape(RET_HEADS // 2, 2, 1, c), RET_DK, axis=2).reshape(RET_HEADS // 2, 2 * RET_DK, c)
    sdec = np.repeat(chunk_decay.reshape(RET_HEADS // 2, 2, 1), RET_DK, axis=2).reshape(RET_HEADS // 2, 2 * RET_DK, 1)
    sdec = np.broadcast_to(sdec, (RET_HEADS // 2, 2 * RET_DK, 2 * RET_DV))
    rr = np.arange(2 * RET_DK)[:, None] // RET_DK
    cc = np.arange(2 * RET_DV)[None, :] // RET_DV
    bmask = (rr == cc).astype(np.float64)
    return tuple(jnp.asarray(np.ascontiguousarray(v), _f32) for v in (dmask, qd, kdt, sdec, bmask))


def _rope_tables(s_len):
    inv = ROPE_BASE ** (-jnp.arange(0, RET_DK, 2, dtype=_f32) / RET_DK)
    ang = jnp.arange(s_len, dtype=_f32)[:, None] * inv[None, :]
    cos, sin = jnp.cos(ang), jnp.sin(ang)
    cos_t = jnp.tile(cos, (1, LANES // (RET_DK // 2)))
    sin_t = jnp.tile(jnp.concatenate([-sin, sin], axis=1), (1, LANES // RET_DK))
    return cos_t, sin_t


def _mixer_operands(s, d, g_mix, w_in, conv_w, g_ret, w_out_conv, w_out_ret, w_o, g_moe, w_rg, b_rg, w_re, b_re):
    ts = MIX_TILE
    cos_t, sin_t = _rope_tables(s)
    dmask, qd, kdt, sdec, bmask = _retention_tables()
    wrt = jnp.zeros((ROUTER_ROWS, d), _f32)
    wrt = wrt.at[:N_GROUPS].set(w_rg.T).at[SUBLANES:SUBLANES + N_EXPERTS].set(w_re.T).astype(_bf16)
    rb = jnp.zeros((ROUTER_ROWS,), _f32).at[:N_GROUPS].set(b_rg).at[SUBLANES:SUBLANES + N_EXPERTS].set(b_re)
    rbias = jnp.broadcast_to(rb[:, None], (ROUTER_ROWS, ts))
    ii = np.arange(ts)
    tri = jnp.asarray(np.stack([(ii[:, None] < ii[None, :]), np.ones((ts, ts), bool)]), _bf16)
    return (g_mix.reshape(1, d), w_in.astype(_bf16), conv_w, cos_t, sin_t, dmask, qd, kdt, sdec, bmask,
            g_ret.reshape(1, d), w_out_conv.astype(_bf16), w_out_ret.astype(_bf16), w_o.astype(_bf16),
            g_moe.reshape(1, d), wrt, rbias, tri)


def _mixer(x, b0, b, operands):
    _, s, d = x.shape
    ts = MIX_TILE
    assert s % ts == 0 and ts % RET_CHUNK == 0 and ts % SC_WINDOW == 0
    nt = s // ts
    tile3 = lambda w: pl.BlockSpec((None, ts, w), lambda bi, ti: (bi, ti, 0))
    route = lambda: pl.BlockSpec((b * nt, 2, ts), lambda bi, ti: (0, 0, 0))
    rope_spec = pl.BlockSpec((ts, LANES), lambda bi, ti: (ti, 0))
    in_specs = [pl.BlockSpec((None, ts, d), lambda bi, ti: (b0 + bi, ti, 0))]
    in_specs += [rope_spec if i in (3, 4) else _const_spec(op.shape) for i, op in enumerate(operands)]
    out_shape = [
        jax.ShapeDtypeStruct((b, s, d), _f32),
        jax.ShapeDtypeStruct((b, s, d // 2), jnp.uint32),
        jax.ShapeDtypeStruct((b * nt, 2, ts), jnp.int32),
        jax.ShapeDtypeStruct((b * nt, 2, ts), _f32),
        jax.ShapeDtypeStruct((b * nt, 2, ts), jnp.int32),
        jax.ShapeDtypeStruct((N_EXPERTS, LANES), jnp.int32),
    ]
    out_specs = [tile3(d), tile3(d // 2), route(), route(), route(),
                 pl.BlockSpec((N_EXPERTS, LANES), lambda bi, ti: (0, 0))]
    scratch = [
        pltpu.VMEM((ts, d), _bf16),
        pltpu.VMEM((ts, RET_HEADS * RET_DK), _f32),
        pltpu.VMEM((ts, RET_HEADS * RET_DK), _f32),
        pltpu.VMEM((ts, RET_HEADS * RET_DV), _bf16),
        pltpu.VMEM((ts, RET_HEADS * RET_DV), _f32),
        pltpu.VMEM((ts, RET_HEADS * RET_DV), _bf16),
        pltpu.VMEM((ts, d), _f32),
        pltpu.VMEM((SUBLANES, d), _f32),
        pltpu.VMEM((RET_HEADS // 2, 2 * RET_DK, 2 * RET_DV), _f32),
        pltpu.VMEM((N_EXPERTS, ts), _f32),
    ]
    return pl.pallas_call(
        _mixer_kernel,
        grid=(b, nt),
        in_specs=in_specs,
        out_specs=out_specs,
        out_shape=out_shape,
        scratch_shapes=scratch,
        compiler_params=pltpu.CompilerParams(
            dimension_semantics=("arbitrary", "arbitrary"), vmem_limit_bytes=VMEM_LIMIT),
        name="mixer_router",
    )(x, *operands)


def _sc_worker_id():
    return lax.axis_index("s") * SC_CORES + lax.axis_index("c")


def _sc_mesh():
    return plsc.VectorSubcoreMesh(core_axis_name="c", subcore_axis_name="s")


def _sc_dispatch(src, idx3, n_rows):
    t, d = src.shape
    n_win_total, _, win = idx3.shape
    assert t == n_win_total * win and n_win_total % SC_WORKERS == 0 and win <= LANES
    n_win = n_win_total // SC_WORKERS

    @functools.partial(
        pl.kernel, mesh=_sc_mesh(),
        out_type=jax.ShapeDtypeStruct((n_rows, d), src.dtype),
        scratch_types=[pltpu.VMEM((2, win), jnp.int32), pltpu.VMEM((win, d), src.dtype)],
    )
    def dispatch(src_hbm, idx_hbm, out_hbm, idx_v, rows_v):
        wid = _sc_worker_id()

        @pl.loop(0, n_win)
        def _(i):
            w = wid * n_win + i
            off = pl.multiple_of(w * win, SUBLANES)
            pltpu.sync_copy(idx_hbm.at[w], idx_v)
            pltpu.sync_copy(src_hbm.at[pl.ds(off, win)], rows_v)
            pltpu.sync_copy(rows_v, out_hbm.at[idx_v.at[0]])
            pltpu.sync_copy(rows_v, out_hbm.at[idx_v.at[1]])

    return dispatch(src, idx3)


def _sc_gather(table, idx):
    n = idx.shape[0]
    d = table.shape[1]
    win = SC_WINDOW
    assert n % (SC_WORKERS * win) == 0
    per_w = n // SC_WORKERS
    n_win = per_w // win

    @functools.partial(
        pl.kernel, mesh=_sc_mesh(),
        out_type=jax.ShapeDtypeStruct((n, d), table.dtype),
        scratch_types=[pltpu.VMEM((win,), jnp.int32), pltpu.VMEM((win, d), table.dtype),
                       pltpu.SemaphoreType.DMA],
    )
    def gather(table_hbm, idx_hbm, out_hbm, idx_v, rows_v, sem):
        base = _sc_worker_id() * per_w

        @pl.loop(0, n_win)
        def _(i):
            off = pl.multiple_of(base + i * win, SUBLANES)
            pltpu.sync_copy(idx_hbm.at[pl.ds(off, win)], idx_v)
            pltpu.async_copy(table_hbm.at[idx_v], rows_v, sem).wait()
            pltpu.sync_copy(rows_v, out_hbm.at[pl.ds(off, win)])

    return gather(table, idx)


def _expert_kernel(blk0_ref, nblk_ref, count_ref, xs_ref, wg_ref, wu_ref, wd_ref, y_ref,
                   xbuf, ybuf, wgb_ref, wub_ref, wdb_ref, hid_ref, sem_in, sem_out):
    e = pl.program_id(0)
    last = pl.num_programs(0) - 1
    blk0 = blk0_ref[e]
    nblk = nblk_ref[e]
    count = count_ref[e]
    total = blk0_ref[last] + nblk_ref[last]
    n_span = (total + EXPERT_SPAN - 1) // EXPERT_SPAN
    span_rows = EXPERT_SPAN * MOE_BLOCK
    half = xbuf.shape[2]

    def span_in(s):
        return pltpu.make_async_copy(xs_ref.at[pl.ds(s * span_rows, span_rows)], xbuf.at[s % 2], sem_in.at[s % 2])

    def span_out(s):
        return pltpu.make_async_copy(ybuf.at[s % 2], y_ref.at[pl.ds(s * span_rows, span_rows)], sem_out.at[s % 2])

    @pl.when(e == 0)
    def _():
        ybuf[...] = jnp.zeros_like(ybuf)

    @pl.when(jnp.logical_and(e == 0, n_span > 0))
    def _():
        span_in(0).start()

    wgb_ref[...] = wg_ref[...].astype(_bf16)
    wub_ref[...] = wu_ref[...].astype(_bf16)
    wdb_ref[...] = wd_ref[...].astype(_bf16)

    def place(g):
        return (g // EXPERT_SPAN) % 2, pl.multiple_of((g % EXPERT_SPAN) * MOE_BLOCK, MOE_BLOCK)

    def fetch(g):
        s = g // EXPERT_SPAN

        @pl.when(g % EXPERT_SPAN == 0)
        def _():
            span_in(s).wait()

            @pl.when(s + 1 < n_span)
            def _():
                span_in(s + 1).start()

    def reserve(g):
        s = g // EXPERT_SPAN

        @pl.when(jnp.logical_and(g % EXPERT_SPAN == 0, s >= 2))
        def _():
            span_out(s - 2).wait()

    def release(g):
        @pl.when(jnp.logical_or(g % EXPERT_SPAN == EXPERT_SPAN - 1, g == total - 1))
        def _():
            span_out(g // EXPERT_SPAN).start()

    def up_proj(g, j):
        slot, row0 = place(g)
        rowid = lax.broadcasted_iota(jnp.int32, (MOE_BLOCK, half), 0)
        xb = xbuf[slot, pl.ds(row0, MOE_BLOCK), :]
        lo, hi = _unpack_halves(jnp.where(rowid < count - j * MOE_BLOCK, xb, jnp.uint32(0)))
        hb = jnp.concatenate([lo.astype(_bf16), hi.astype(_bf16)], axis=1)
        gate = _dot(hb, wgb_ref[...])
        up = _dot(hb, wub_ref[...])
        return (gate * _sigmoid(gate) * up).astype(_bf16)

    def down_proj(g, hid_slot):
        slot, row0 = place(g)
        ybuf[slot, pl.ds(row0, MOE_BLOCK), :] = _pack_halves(_dot(hid_ref[hid_slot], wdb_ref[...]))

    @pl.when(nblk > 0)
    def _():
        fetch(blk0)
        hid_ref[0] = up_proj(blk0, 0)

    def steady(j, carry):
        g = blk0 + j
        fetch(g)
        reserve(g - 1)
        down_proj(g - 1, (j - 1) % 2)
        hid_ref[j % 2] = up_proj(g, j)
        release(g - 1)
        return carry

    lax.fori_loop(1, nblk, steady, 0)

    @pl.when(nblk > 0)
    def _():
        g = blk0 + nblk - 1
        reserve(g)
        down_proj(g, (nblk - 1) % 2)
        release(g)

    @pl.when(jnp.logical_and(e == last, n_span >= 2))
    def _():
        span_out(n_span - 2).wait()

    @pl.when(jnp.logical_and(e == last, n_span >= 1))
    def _():
        span_out(n_span - 1).wait()


def _experts(xs, w_gate, w_up, w_down, blk0, nblk, counts):
    n_rows, half = xs.shape
    d = 2 * half
    ne, _, de = w_gate.shape
    grid_spec = pltpu.PrefetchScalarGridSpec(
        num_scalar_prefetch=3,
        grid=(ne,),
        in_specs=[
            pl.BlockSpec(memory_space=pl.ANY),
            pl.BlockSpec((None, d, de), lambda e, *_: (e, 0, 0)),
            pl.BlockSpec((None, d, de), lambda e, *_: (e, 0, 0)),
            pl.BlockSpec((None, de, d), lambda e, *_: (e, 0, 0)),
        ],
        out_specs=pl.BlockSpec(memory_space=pl.ANY),
        scratch_shapes=[
            pltpu.VMEM((2, EXPERT_SPAN * MOE_BLOCK, half), jnp.uint32),
            pltpu.VMEM((2, EXPERT_SPAN * MOE_BLOCK, half), jnp.uint32),
            pltpu.VMEM((d, de), _bf16), pltpu.VMEM((d, de), _bf16), pltpu.VMEM((de, d), _bf16),
            pltpu.VMEM((2, MOE_BLOCK, de), _bf16),
            pltpu.SemaphoreType.DMA((2,)), pltpu.SemaphoreType.DMA((2,)),
        ],
    )
    return pl.pallas_call(
        _expert_kernel,
        grid_spec=grid_spec,
        out_shape=jax.ShapeDtypeStruct((n_rows, half), jnp.uint32),
        compiler_params=pltpu.CompilerParams(dimension_semantics=("arbitrary",), vmem_limit_bytes=VMEM_LIMIT),
        name="experts",
    )(blk0, nblk, counts, xs, w_gate, w_up, w_down)


def _combine_kernel(ew_ref, x1_ref, p_ref, y2_ref, gin_ref, wpg_ref, wpp_ref, gpost_ref, gfin_ref, *rest):
    out_ref = rest[-1]
    d = x1_ref.shape[1]
    n_sub, _, tc = ew_ref.shape
    for sub in range(n_sub):
        rows = slice(sub * tc, (sub + 1) * tc)
        x2 = x1_ref[rows, :]
        for k in range(2):
            wcol = jnp.broadcast_to(ew_ref[sub, k:k + 1, :], (LANES, tc)).T
            yk = jnp.concatenate(_unpack_halves(y2_ref[sub, k]), axis=1)
            x2 = x2 + jnp.tile(wcol, (1, d // LANES)) * yk
        gate = _sigmoid(_dot(_rms(x2, gin_ref[...]).astype(_bf16), wpg_ref[...]))
        ple = _rms(_dot(p_ref[rows, :].astype(_bf16), wpp_ref[...]), gpost_ref[...])
        out_ref[rows, :] = _rms(x2 + gate * ple, gfin_ref[...])


def _combine(x1, p0, y2, ew, gains_and_weights, b0, b_total, prev_out):
    b, s, d = x1.shape
    ts = MIX_TILE
    n_sub = COMBINE_TILES
    tc = n_sub * ts
    assert s % tc == 0
    nt = s // tc
    pdim = p0.shape[-1]
    vec = lambda: pl.BlockSpec((1, d), lambda bi, ti: (0, 0))
    in_specs = [
        pl.BlockSpec((n_sub, 2, ts), lambda bi, ti: (bi * nt + ti, 0, 0)),
        pl.BlockSpec((None, tc, d), lambda bi, ti: (bi, ti, 0)),
        pl.BlockSpec((None, tc, pdim), lambda bi, ti: (b0 + bi, ti, 0)),
        pl.BlockSpec((n_sub, 2, ts, d // 2), lambda bi, ti: (bi * nt + ti, 0, 0, 0)),
        vec(),
        pl.BlockSpec((d, d), lambda bi, ti: (0, 0)),
        pl.BlockSpec((pdim, d), lambda bi, ti: (0, 0)),
        vec(), vec(),
    ]
    args = [ew, x1, p0, y2, *gains_and_weights]
    aliases = {}
    if prev_out is not None:
        in_specs.append(pl.BlockSpec(memory_space=pl.ANY))
        aliases = {len(args): 0}
        args.append(prev_out)
    return pl.pallas_call(
        _combine_kernel,
        grid=(b, s // tc),
        in_specs=in_specs,
        out_specs=pl.BlockSpec((None, tc, d), lambda bi, ti: (b0 + bi, ti, 0)),
        out_shape=jax.ShapeDtypeStruct((b_total, s, d), _f32),
        input_output_aliases=aliases,
        compiler_params=pltpu.CompilerParams(
            dimension_semantics=("arbitrary", "arbitrary"), vmem_limit_bytes=VMEM_LIMIT),
        name="combine_ple",
    )(*args)


def _layer(x, p_i, g_mix, w_in, conv_w, g_ret, w_out_conv, w_out_ret, w_o, g_moe, w_rg, b_rg, w_re, b_re,
           w_exp_gate, w_exp_up, w_exp_down, g_ple_in, w_ple_gate, w_ple_proj, g_ple_post, g_out):
    b_total, s, d = x.shape
    assert b_total % sum(SLICE_PARTS) == 0
    sizes = [b_total // sum(SLICE_PARTS) * part for part in SLICE_PARTS]
    starts = [sum(sizes[:h]) for h in range(len(sizes))]
    operands = _mixer_operands(s, d, g_mix, w_in, conv_w, g_ret, w_out_conv, w_out_ret, w_o, g_moe,
                               w_rg, b_rg, w_re, b_re)
    tail = (g_ple_in.reshape(1, d), w_ple_gate.astype(_bf16), w_ple_proj.astype(_bf16),
            g_ple_post.reshape(1, d), g_out.reshape(1, d))
    win = SC_WINDOW

    def route(eid, rank, cnt, n_tok):
        counts = cnt[:, 0]
        padded = (counts + MOE_BLOCK - 1) // MOE_BLOCK * MOE_BLOCK
        pends = jnp.cumsum(padded)
        pstarts = pends - padded
        eids = jnp.arange(N_EXPERTS, dtype=jnp.int32).reshape(N_EXPERTS, 1, 1, 1)
        seg = jnp.sum(jnp.where(eid[None] == eids, pstarts.reshape(N_EXPERTS, 1, 1, 1), 0), axis=0)
        dest = rank + seg
        tiles, _, ts = dest.shape
        idx3 = dest.reshape(tiles, 2, ts // win, win).transpose(0, 2, 1, 3).reshape(n_tok // win, 2, win)
        return dest, idx3, pstarts // MOE_BLOCK, padded // MOE_BLOCK, counts

    st = [dict() for _ in sizes]
    out = None

    def mix(h):
        n_tok = sizes[h] * s
        nblk = (2 * n_tok + N_EXPERTS * (MOE_BLOCK - 1) + MOE_BLOCK - 1) // MOE_BLOCK
        nblk = (nblk + EXPERT_SPAN - 1) // EXPERT_SPAN * EXPERT_SPAN
        x1, h2p, eid, ew, rank, cnt = _mixer(x, starts[h], sizes[h], operands)
        dest, idx3, blk0, nblk_e, counts = route(eid, rank, cnt, n_tok)
        st[h].update(x1=x1, h2p=h2p, ew=ew, dest=dest, idx3=idx3, seg=(blk0, nblk_e, counts),
                     n_tok=n_tok, n_rows=nblk * MOE_BLOCK)

    def dispatch(h):
        st[h]["xs"] = _sc_dispatch(st[h]["h2p"].reshape(st[h]["n_tok"], d // 2), st[h]["idx3"], st[h]["n_rows"])

    def experts(h):
        st[h]["y"] = _experts(st[h]["xs"], w_exp_gate, w_exp_up, w_exp_down, *st[h]["seg"])

    def gather(h):
        y2 = _sc_gather(st[h]["y"], st[h]["dest"].reshape(-1))
        st[h]["y2"] = y2.reshape(st[h]["n_tok"] // MIX_TILE, 2, MIX_TILE, d // 2)

    def combine(h):
        nonlocal out
        out = _combine(st[h]["x1"], p_i, st[h]["y2"], st[h]["ew"], tail, starts[h], b_total, out)

    stages = (mix, dispatch, experts, gather, combine)
    for step in range(len(sizes) + len(stages) - 1):
        for k in reversed(range(len(stages))):
            if 0 <= step - k < len(sizes):
                stages[k](step - k)
    return out


def kernel(x, p, g_mix, w_in, conv_w, g_ret, w_out_conv, w_out_ret, w_o, g_moe, w_rg, b_rg, w_re, b_re, w_exp_gate, w_exp_up, w_exp_down, g_ple_in, w_ple_gate, w_ple_proj, g_ple_post, g_final):
    depth = p.shape[0]
    assert depth == 1, "the final norm is fused into the single layer's combine kernel"
    return _layer(x, p[0], g_mix[0], w_in[0], conv_w[0], g_ret[0], w_out_conv[0], w_out_ret[0], w_o[0],
                  g_moe[0], w_rg[0], b_rg[0], w_re[0], b_re[0], w_exp_gate[0], w_exp_up[0], w_exp_down[0],
                  g_ple_in[0], w_ple_gate[0], w_ple_proj[0], g_ple_post[0], g_final)
```

```python
import functools

import jax
import jax.numpy as jnp
import numpy as np
from jax import lax
from jax.experimental import pallas as pl
from jax.experimental.pallas import tpu as pltpu
from jax.experimental.pallas import tpu_sc as plsc

EPS = 1e-6
RET_HEADS = 8
RET_DK = 64
RET_DV = 128
RET_CHUNK = 128
ROPE_BASE = 10000.0
N_GROUPS = 4
EXPERTS_PER_GROUP = 8
N_EXPERTS = N_GROUPS * EXPERTS_PER_GROUP
MOE_BLOCK = 256
LANES = 128
SUBLANES = 8
ROUTER_ROWS = 48
MIX_TILE = 512
COMBINE_TILES = 2
EXPERT_SPAN = 8
SLICE_PARTS = (5, 3)
SC_CORES = 2
SC_SUBCORES = 16
SC_WORKERS = SC_CORES * SC_SUBCORES
SC_WINDOW = 32
SC_GATHER_WINDOW = 64
BF16_BITS = 16
HIGH_HALF = np.uint32(0xFFFF0000)
V7X_VMEM_BYTES = 64 * 1024 * 1024
VMEM_LIMIT = V7X_VMEM_BYTES // 8 * 7

_bf16 = jnp.bfloat16
_f32 = jnp.float32


def _sigmoid(v):
    return 0.5 * jnp.tanh(0.5 * v) + 0.5


def _rms(v, g):
    ms = jnp.mean(v * v, axis=-1, keepdims=True)
    return v * lax.rsqrt(ms + EPS) * g


def _dot(a, b):
    return jnp.dot(a, b, preferred_element_type=_f32)


def _pack_halves(v):
    bits = lax.bitcast_convert_type(v.astype(_bf16).astype(_f32), jnp.uint32)
    c = v.shape[1] // 2
    return (bits[:, :c] >> BF16_BITS) | (bits[:, c:] & HIGH_HALF)


def _unpack_halves(w):
    lo = lax.bitcast_convert_type(w << BF16_BITS, _f32)
    hi = lax.bitcast_convert_type(w & HIGH_HALF, _f32)
    return lo, hi


def _const_spec(shape):
    nd = len(shape)
    return pl.BlockSpec(shape, lambda *_: (0,) * nd, pipeline_mode=pl.Buffered(1))


def _mixer_kernel(x_ref, gmix_ref, win_ref, convw_ref, cos_ref, sin_ref, dmask_ref, qd_ref, kdt_ref,
                  sdec_ref, bmask_ref, gret_ref, woc_ref, wor_ref, wo_ref, gmoe_ref, wrt_ref, rbias_ref,
                  tri_ref,
                  x1_ref, h2p_ref, eid_ref, ew_ref, rank_ref, cnt_ref,
                  hb_ref, qr_ref, kr_ref, vb_ref, o_ref, og_ref, acc_ref, cuc_ref, state_ref, carry_ref):
    ts, d = x_ref.shape
    t = pl.program_id(1)
    n = pl.program_id(0) * pl.num_programs(1) + t
    router_refs = (wrt_ref, rbias_ref, tri_ref, carry_ref, eid_ref, ew_ref, rank_ref, cnt_ref)

    @pl.when(t == 0)
    def _():
        cuc_ref[...] = jnp.zeros_like(cuc_ref)
        state_ref[...] = jnp.zeros_like(state_ref)

    @pl.when(n == 0)
    def _():
        carry_ref[...] = jnp.zeros_like(carry_ref)

    x = x_ref[...]
    hb_ref[...] = _rms(x, gmix_ref[...]).astype(_bf16)
    hb = hb_ref[...]

    def proj(lo, hi):
        return _dot(hb, win_ref[:, lo:hi])

    cu = proj(d, 2 * d) * proj(0, d)
    prev = cuc_ref[...]
    p1 = prev[SUBLANES - 1:SUBLANES, :]
    p2 = prev[SUBLANES - 2:SUBLANES - 1, :]
    rows = lax.broadcasted_iota(jnp.int32, (ts, d), 0)
    s1 = jnp.where(rows == 0, p1, pltpu.roll(cu, 1, 0))
    s2 = jnp.where(rows == 0, p2, jnp.where(rows == 1, p1, pltpu.roll(cu, 2, 0)))
    conv = convw_ref[0:1, :] * s2 + convw_ref[1:2, :] * s1 + convw_ref[2:3, :] * cu
    cuc_ref[...] = cu[ts - SUBLANES:ts, :]
    a = (proj(2 * d, 3 * d) * conv).astype(_bf16)
    acc_ref[...] = _sigmoid(proj(6 * d, 7 * d)) * _dot(a, woc_ref[...])

    qk0 = 3 * d
    cosv = cos_ref[...]
    sinv = sin_ref[...]
    lane = lax.broadcasted_iota(jnp.int32, (ts, LANES), 1)
    first_half = (lane % RET_DK) < (RET_DK // 2)
    qk_w = RET_HEADS * RET_DK
    for dst, base, scale in ((qr_ref, qk0, None), (kr_ref, qk0 + qk_w, RET_DK ** -0.5)):
        zw = proj(base, base + qk_w)
        for g in range(qk_w // LANES):
            z = zw[:, g * LANES:(g + 1) * LANES]
            zs = jnp.where(first_half, pltpu.roll(z, LANES - RET_DK // 2, 1), pltpu.roll(z, RET_DK // 2, 1))
            r = z * cosv + zs * sinv
            if scale is not None:
                r = r * scale
            dst[:, g * LANES:(g + 1) * LANES] = r
    vb_ref[...] = proj(4 * d, 5 * d).astype(_bf16)

    c = RET_CHUNK
    lane_c = lax.broadcasted_iota(jnp.int32, (c, LANES), 1)
    even = lane_c < RET_DK
    for ci in range(ts // c):
        r0 = ci * c
        for j in range(RET_HEADS // 2):
            q2 = qr_ref[r0:r0 + c, j * LANES:(j + 1) * LANES]
            k2 = kr_ref[r0:r0 + c, j * LANES:(j + 1) * LANES]
            v2 = vb_ref[r0:r0 + c, 2 * j * RET_DV:(2 * j + 2) * RET_DV]
            kt = k2.T
            qq = jnp.concatenate([jnp.where(even, q2, 0.0), jnp.where(even, 0.0, q2)], axis=0).astype(_bf16)
            sc = _dot(qq, kt.astype(_bf16))
            pe = (sc[:c] * dmask_ref[2 * j]).astype(_bf16)
            po = (sc[c:] * dmask_ref[2 * j + 1]).astype(_bf16)
            inner = jnp.concatenate([_dot(pe, v2[:, :RET_DV]), _dot(po, v2[:, RET_DV:])], axis=1)
            st = state_ref[j]
            cross = _dot(q2.astype(_bf16), st.astype(_bf16)) * qd_ref[:, 2 * j * RET_DV:(2 * j + 2) * RET_DV]
            o_ref[r0:r0 + c, 2 * j * RET_DV:(2 * j + 2) * RET_DV] = inner + cross
            upd = _dot((kt * kdt_ref[j]).astype(_bf16), v2)
            state_ref[j] = st * sdec_ref[j] + upd * bmask_ref[...]

    for hp in range(RET_HEADS // 2):
        sg2 = proj(5 * d + 2 * hp * RET_DV, 5 * d + (2 * hp + 2) * RET_DV)
        for h in (2 * hp, 2 * hp + 1):
            sl = slice(h * RET_DV, (h + 1) * RET_DV)
            sg = sg2[:, (h - 2 * hp) * RET_DV:(h - 2 * hp + 1) * RET_DV]
            on = _rms(o_ref[:, sl], gret_ref[:, sl])
            og_ref[:, sl] = (sg * _sigmoid(sg) * on).astype(_bf16)
    yr = _dot(og_ref[...], wor_ref[...])
    mixed = acc_ref[...] + _sigmoid(proj(7 * d, 8 * d)) * yr
    x1 = x + _dot(mixed.astype(_bf16), wo_ref[...])
    x1_ref[...] = x1

    h2 = _rms(x1, gmoe_ref[...]).astype(_bf16)
    h2p_ref[...] = _pack_halves(h2)
    _router(h2, n, *router_refs)


def _router(h2, tile, wrt_ref, rbias_ref, tri_ref, carry_ref, eid_ref, ew_ref, rank_ref, cnt_ref):
    ts = h2.shape[0]
    lt = lax.dot_general(wrt_ref[...], h2, (((1,), (1,)), ((), ())), preferred_element_type=_f32)
    lt = lt + rbias_ref[...]
    g0, g1, g2, g3 = (lt[i:i + 1, :] for i in range(N_GROUPS))
    gmax = jnp.maximum(jnp.maximum(g0, g1), jnp.maximum(g2, g3))
    grp = jnp.where(g0 == gmax, 0, jnp.where(g1 == gmax, 1, jnp.where(g2 == gmax, 2, 3)))
    gsum = jnp.exp(g0 - gmax) + jnp.exp(g1 - gmax) + jnp.exp(g2 - gmax) + jnp.exp(g3 - gmax)
    g_w = 1.0 / gsum
    e_in = lt[SUBLANES:2 * SUBLANES, :]
    for g in range(1, N_GROUPS):
        e_in = jnp.where(grp == g, lt[(g + 1) * SUBLANES:(g + 2) * SUBLANES, :], e_in)
    ridx = lax.broadcasted_iota(jnp.int32, (EXPERTS_PER_GROUP, ts), 0)
    top1 = jnp.max(e_in, axis=0, keepdims=True)
    i1 = jnp.min(jnp.where(e_in == top1, ridx, EXPERTS_PER_GROUP), axis=0, keepdims=True)
    e_m = jnp.where(ridx == i1, -jnp.inf, e_in)
    top2 = jnp.max(e_m, axis=0, keepdims=True)
    i2 = jnp.min(jnp.where(e_m == top2, ridx, EXPERTS_PER_GROUP), axis=0, keepdims=True)
    ex = jnp.exp(top2 - top1)
    den = 1.0 + ex
    id0 = grp * EXPERTS_PER_GROUP + i1
    id1 = grp * EXPERTS_PER_GROUP + i2
    eid_ref[tile, 0:1, :] = id0
    eid_ref[tile, 1:2, :] = id1
    ew_ref[tile, 0:1, :] = (1.0 / den) * g_w
    ew_ref[tile, 1:2, :] = (ex / den) * g_w

    eidx = lax.broadcasted_iota(jnp.int32, (N_EXPERTS, ts), 0)
    oh0 = (eidx == id0).astype(_f32)
    oh1 = (eidx == id1).astype(_f32)
    cnt = (oh0 + oh1).astype(_bf16)
    before = carry_ref[...] + _dot(cnt, tri_ref[0])
    rank_ref[tile, 0:1, :] = jnp.sum(oh0 * before, axis=0, keepdims=True).astype(jnp.int32)
    rank_ref[tile, 1:2, :] = jnp.sum(oh1 * before, axis=0, keepdims=True).astype(jnp.int32)
    total = carry_ref[...] + _dot(cnt, tri_ref[1])
    carry_ref[...] = total
    cnt_ref[...] = total[:, :LANES].astype(jnp.int32)


def _retention_tables():
    c = RET_CHUNK
    log_gamma = np.log1p(-np.exp2(-5.0 - np.arange(RET_HEADS, dtype=np.float64)))
    pos = np.arange(c, dtype=np.float64)
    diff = pos[:, None] - pos[None, :]
    dmask = np.where((diff >= 0)[None], np.exp(log_gamma[:, None, None] * np.maximum(diff, 0.0)[None]), 0.0)
    q_decay = np.exp(log_gamma[:, None] * (pos[None, :] + 1.0))
    k_decay = np.exp(log_gamma[:, None] * (c - 1.0 - pos[None, :]))
    chunk_decay = np.exp(log_gamma * c)
    qd = np.repeat(q_decay.T, RET_DV, axis=1)
    kdt = np.repeat(k_decay.reshape(RET_HEADS // 2, 2, 1, c), RET_DK, axis=2).reshape(RET_HEADS // 2, 2 * RET_DK, c)
    sdec = np.repeat(chunk_decay.reshape(RET_HEADS // 2, 2, 1), RET_DK, axis=2).reshape(RET_HEADS // 2, 2 * RET_DK, 1)
    sdec = np.broadcast_to(sdec, (RET_HEADS // 2, 2 * RET_DK, 2 * RET_DV))
    rr = np.arange(2 * RET_DK)[:, None] // RET_DK
    cc = np.arange(2 * RET_DV)[None, :] // RET_DV
    bmask = (rr == cc).astype(np.float64)
    return tuple(jnp.asarray(np.ascontiguousarray(v), _f32) for v in (dmask, qd, kdt, sdec, bmask))


def _rope_tables(s_len):
    inv = ROPE_BASE ** (-jnp.arange(0, RET_DK, 2, dtype=_f32) / RET_DK)
    ang = jnp.arange(s_len, dtype=_f32)[:, None] * inv[None, :]
    cos, sin = jnp.cos(ang), jnp.sin(ang)
    cos_t = jnp.tile(cos, (1, LANES // (RET_DK // 2)))
    sin_t = jnp.tile(jnp.concatenate([-sin, sin], axis=1), (1, LANES // RET_DK))
    return cos_t, sin_t


def _mixer_operands(s, d, g_mix, w_in, conv_w, g_ret, w_out_conv, w_out_ret, w_o, g_moe, w_rg, b_rg, w_re, b_re):
    ts = MIX_TILE
    cos_t, sin_t = _rope_tables(s)
    dmask, qd, kdt, sdec, bmask = _retention_tables()
    wrt = jnp.zeros((ROUTER_ROWS, d), _f32)
    wrt = wrt.at[:N_GROUPS].set(w_rg.T).at[SUBLANES:SUBLANES + N_EXPERTS].set(w_re.T).astype(_bf16)
    rb = jnp.zeros((ROUTER_ROWS,), _f32).at[:N_GROUPS].set(b_rg).at[SUBLANES:SUBLANES + N_EXPERTS].set(b_re)
    rbias = jnp.broadcast_to(rb[:, None], (ROUTER_ROWS, ts))
    ii = np.arange(ts)
    tri = jnp.asarray(np.stack([(ii[:, None] < ii[None, :]), np.ones((ts, ts), bool)]), _bf16)
    return (g_mix.reshape(1, d), w_in.astype(_bf16), conv_w, cos_t, sin_t, dmask, qd, kdt, sdec, bmask,
            g_ret.reshape(1, d), w_out_conv.astype(_bf16), w_out_ret.astype(_bf16), w_o.astype(_bf16),
            g_moe.reshape(1, d), wrt, rbias, tri)


def _mixer(x, b0, b, operands):
    _, s, d = x.shape
    ts = MIX_TILE
    assert s % ts == 0 and ts % RET_CHUNK == 0 and ts % SC_WINDOW == 0
    nt = s // ts
    tile3 = lambda w: pl.BlockSpec((None, ts, w), lambda bi, ti: (bi, ti, 0))
    route = lambda: pl.BlockSpec((b * nt, 2, ts), lambda bi, ti: (0, 0, 0))
    rope_spec = pl.BlockSpec((ts, LANES), lambda bi, ti: (ti, 0))
    in_specs = [pl.BlockSpec((None, ts, d), lambda bi, ti: (b0 + bi, ti, 0))]
    in_specs += [rope_spec if i in (3, 4) else _const_spec(op.shape) for i, op in enumerate(operands)]
    out_shape = [
        jax.ShapeDtypeStruct((b, s, d), _f32),
        jax.ShapeDtypeStruct((b, s, d // 2), jnp.uint32),
        jax.ShapeDtypeStruct((b * nt, 2, ts), jnp.int32),
        jax.ShapeDtypeStruct((b * nt, 2, ts), _f32),
        jax.ShapeDtypeStruct((b * nt, 2, ts), jnp.int32),
        jax.ShapeDtypeStruct((N_EXPERTS, LANES), jnp.int32),
    ]
    out_specs = [tile3(d), tile3(d // 2), route(), route(), route(),
                 pl.BlockSpec((N_EXPERTS, LANES), lambda bi, ti: (0, 0))]
    scratch = [
        pltpu.VMEM((ts, d), _bf16),
        pltpu.VMEM((ts, RET_HEADS * RET_DK), _f32),
        pltpu.VMEM((ts, RET_HEADS * RET_DK), _f32),
        pltpu.VMEM((ts, RET_HEADS * RET_DV), _bf16),
        pltpu.VMEM((ts, RET_HEADS * RET_DV), _f32),
        pltpu.VMEM((ts, RET_HEADS * RET_DV), _bf16),
        pltpu.VMEM((ts, d), _f32),
        pltpu.VMEM((SUBLANES, d), _f32),
        pltpu.VMEM((RET_HEADS // 2, 2 * RET_DK, 2 * RET_DV), _f32),
        pltpu.VMEM((N_EXPERTS, ts), _f32),
    ]
    return pl.pallas_call(
        _mixer_kernel,
        grid=(b, nt),
        in_specs=in_specs,
        out_specs=out_specs,
        out_shape=out_shape,
        scratch_shapes=scratch,
        compiler_params=pltpu.CompilerParams(
            dimension_semantics=("arbitrary", "arbitrary"), vmem_limit_bytes=VMEM_LIMIT),
        name="mixer_router",
    )(x, *operands)


def _sc_worker_id():
    return lax.axis_index("s") * SC_CORES + lax.axis_index("c")


def _sc_mesh():
    return plsc.VectorSubcoreMesh(core_axis_name="c", subcore_axis_name="s")


def _sc_dispatch(src, idx3, n_rows):
    t, d = src.shape
    n_win_total, _, win = idx3.shape
    assert t == n_win_total * win and n_win_total % SC_WORKERS == 0 and win <= LANES
    n_win = n_win_total // SC_WORKERS

    @functools.partial(
        pl.kernel, mesh=_sc_mesh(),
        out_type=jax.ShapeDtypeStruct((n_rows, d), src.dtype),
        scratch_types=[pltpu.VMEM((2, win), jnp.int32), pltpu.VMEM((win, d), src.dtype)],
    )
    def dispatch(src_hbm, idx_hbm, out_hbm, idx_v, rows_v):
        wid = _sc_worker_id()

        @pl.loop(0, n_win)
        def _(i):
            w = wid * n_win + i
            off = pl.multiple_of(w * win, SUBLANES)
            pltpu.sync_copy(idx_hbm.at[w], idx_v)
            pltpu.sync_copy(src_hbm.at[pl.ds(off, win)], rows_v)
            pltpu.sync_copy(rows_v, out_hbm.at[idx_v.at[0]])
            pltpu.sync_copy(rows_v, out_hbm.at[idx_v.at[1]])

    return dispatch(src, idx3)


def _sc_gather(table, idx):
    n = idx.shape[0]
    d = table.shape[1]
    win = SC_GATHER_WINDOW
    assert n % (SC_WORKERS * win) == 0
    per_w = n // SC_WORKERS
    n_win = per_w // win

    @functools.partial(
        pl.kernel, mesh=_sc_mesh(),
        out_type=jax.ShapeDtypeStruct((n, d), table.dtype),
        scratch_types=[pltpu.VMEM((win,), jnp.int32), pltpu.VMEM((win, d), table.dtype),
                       pltpu.SemaphoreType.DMA],
    )
    def gather(table_hbm, idx_hbm, out_hbm, idx_v, rows_v, sem):
        base = _sc_worker_id() * per_w

        @pl.loop(0, n_win)
        def _(i):
            off = pl.multiple_of(base + i * win, SUBLANES)
            pltpu.sync_copy(idx_hbm.at[pl.ds(off, win)], idx_v)
            pltpu.async_copy(table_hbm.at[idx_v], rows_v, sem).wait()
            pltpu.sync_copy(rows_v, out_hbm.at[pl.ds(off, win)])

    return gather(table, idx)


def _expert_kernel(blk0_ref, nblk_ref, count_ref, xs_ref, wg_ref, wu_ref, wd_ref, y_ref,
                   xbuf, ybuf, wgb_ref, wub_ref, wdb_ref, hid_ref, sem_in, sem_out):
    e = pl.program_id(0)
    last = pl.num_programs(0) - 1
    blk0 = blk0_ref[e]
    nblk = nblk_ref[e]
    count = count_ref[e]
    total = blk0_ref[last] + nblk_ref[last]
    n_span = (total + EXPERT_SPAN - 1) // EXPERT_SPAN
    span_rows = EXPERT_SPAN * MOE_BLOCK
    half = xbuf.shape[2]

    def span_in(s):
        return pltpu.make_async_copy(xs_ref.at[pl.ds(s * span_rows, span_rows)], xbuf.at[s % 2], sem_in.at[s % 2])

    def span_out(s):
        return pltpu.make_async_copy(ybuf.at[s % 2], y_ref.at[pl.ds(s * span_rows, span_rows)], sem_out.at[s % 2])

    @pl.when(e == 0)
    def _():
        ybuf[...] = jnp.zeros_like(ybuf)

    @pl.when(jnp.logical_and(e == 0, n_span > 0))
    def _():
        span_in(0).start()

    wgb_ref[...] = wg_ref[...].astype(_bf16)
    wub_ref[...] = wu_ref[...].astype(_bf16)
    wdb_ref[...] = wd_ref[...].astype(_bf16)

    def place(g):
        return (g // EXPERT_SPAN) % 2, pl.multiple_of((g % EXPERT_SPAN) * MOE_BLOCK, MOE_BLOCK)

    def fetch(g):
        s = g // EXPERT_SPAN

        @pl.when(g % EXPERT_SPAN == 0)
        def _():
            span_in(s).wait()

            @pl.when(s + 1 < n_span)
            def _():
                span_in(s + 1).start()

    def reserve(g):
        s = g // EXPERT_SPAN

        @pl.when(jnp.logical_and(g % EXPERT_SPAN == 0, s >= 2))
        def _():
            span_out(s - 2).wait()

    def release(g):
        @pl.when(jnp.logical_or(g % EXPERT_SPAN == EXPERT_SPAN - 1, g == total - 1))
        def _():
            span_out(g // EXPERT_SPAN).start()

    def up_proj(g, j):
        slot, row0 = place(g)
        rowid = lax.broadcasted_iota(jnp.int32, (MOE_BLOCK, half), 0)
        xb = xbuf[slot, pl.ds(row0, MOE_BLOCK), :]
        lo, hi = _unpack_halves(jnp.where(rowid < count - j * MOE_BLOCK, xb, jnp.uint32(0)))
        hb = jnp.concatenate([lo.astype(_bf16), hi.astype(_bf16)], axis=1)
        gate = _dot(hb, wgb_ref[...])
        up = _dot(hb, wub_ref[...])
        return (gate * _sigmoid(gate) * up).astype(_bf16)

    def down_proj(g, hid_slot):
        slot, row0 = place(g)
        ybuf[slot, pl.ds(row0, MOE_BLOCK), :] = _pack_halves(_dot(hid_ref[hid_slot], wdb_ref[...]))

    @pl.when(nblk > 0)
    def _():
        fetch(blk0)
        hid_ref[0] = up_proj(blk0, 0)

    def steady(j, carry):
        g = blk0 + j
        fetch(g)
        reserve(g - 1)
        down_proj(g - 1, (j - 1) % 2)
        hid_ref[j % 2] = up_proj(g, j)
        release(g - 1)
        return carry

    lax.fori_loop(1, nblk, steady, 0)

    @pl.when(nblk > 0)
    def _():
        g = blk0 + nblk - 1
        reserve(g)
        down_proj(g, (nblk - 1) % 2)
        release(g)

    @pl.when(jnp.logical_and(e == last, n_span >= 2))
    def _():
        span_out(n_span - 2).wait()

    @pl.when(jnp.logical_and(e == last, n_span >= 1))
    def _():
        span_out(n_span - 1).wait()


def _experts(xs, w_gate, w_up, w_down, blk0, nblk, counts):
    n_rows, half = xs.shape
    d = 2 * half
    ne, _, de = w_gate.shape
    grid_spec = pltpu.PrefetchScalarGridSpec(
        num_scalar_prefetch=3,
        grid=(ne,),
        in_specs=[
            pl.BlockSpec(memory_space=pl.ANY),
            pl.BlockSpec((None, d, de), lambda e, *_: (e, 0, 0)),
            pl.BlockSpec((None, d, de), lambda e, *_: (e, 0, 0)),
            pl.BlockSpec((None, de, d), lambda e, *_: (e, 0, 0)),
        ],
        out_specs=pl.BlockSpec(memory_space=pl.ANY),
        scratch_shapes=[
            pltpu.VMEM((2, EXPERT_SPAN * MOE_BLOCK, half), jnp.uint32),
            pltpu.VMEM((2, EXPERT_SPAN * MOE_BLOCK, half), jnp.uint32),
            pltpu.VMEM((d, de), _bf16), pltpu.VMEM((d, de), _bf16), pltpu.VMEM((de, d), _bf16),
            pltpu.VMEM((2, MOE_BLOCK, de), _bf16),
            pltpu.SemaphoreType.DMA((2,)), pltpu.SemaphoreType.DMA((2,)),
        ],
    )
    return pl.pallas_call(
        _expert_kernel,
        grid_spec=grid_spec,
        out_shape=jax.ShapeDtypeStruct((n_rows, half), jnp.uint32),
        compiler_params=pltpu.CompilerParams(dimension_semantics=("arbitrary",), vmem_limit_bytes=VMEM_LIMIT),
        name="experts",
    )(blk0, nblk, counts, xs, w_gate, w_up, w_down)


def _combine_kernel(ew_ref, x1_ref, p_ref, y2_ref, gin_ref, wpg_ref, wpp_ref, gpost_ref, gfin_ref, *rest):
    out_ref = rest[-1]
    d = x1_ref.shape[1]
    n_sub, _, tc = ew_ref.shape
    for sub in range(n_sub):
        rows = slice(sub * tc, (sub + 1) * tc)
        x2 = x1_ref[rows, :]
        for k in range(2):
            wcol = jnp.broadcast_to(ew_ref[sub, k:k + 1, :], (LANES, tc)).T
            yk = jnp.concatenate(_unpack_halves(y2_ref[sub, k]), axis=1)
            x2 = x2 + jnp.tile(wcol, (1, d // LANES)) * yk
        gate = _sigmoid(_dot(_rms(x2, gin_ref[...]).astype(_bf16), wpg_ref[...]))
        ple = _rms(_dot(p_ref[rows, :].astype(_bf16), wpp_ref[...]), gpost_ref[...])
        out_ref[rows, :] = _rms(x2 + gate * ple, gfin_ref[...])


def _combine(x1, p0, y2, ew, gains_and_weights, b0, b_total, prev_out):
    b, s, d = x1.shape
    ts = MIX_TILE
    n_sub = COMBINE_TILES
    tc = n_sub * ts
    assert s % tc == 0
    nt = s // tc
    pdim = p0.shape[-1]
    vec = lambda: pl.BlockSpec((1, d), lambda bi, ti: (0, 0))
    in_specs = [
        pl.BlockSpec((n_sub, 2, ts), lambda bi, ti: (bi * nt + ti, 0, 0)),
        pl.BlockSpec((None, tc, d), lambda bi, ti: (bi, ti, 0)),
        pl.BlockSpec((None, tc, pdim), lambda bi, ti: (b0 + bi, ti, 0)),
        pl.BlockSpec((n_sub, 2, ts, d // 2), lambda bi, ti: (bi * nt + ti, 0, 0, 0)),
        vec(),
        pl.BlockSpec((d, d), lambda bi, ti: (0, 0)),
        pl.BlockSpec((pdim, d), lambda bi, ti: (0, 0)),
        vec(), vec(),
    ]
    args = [ew, x1, p0, y2, *gains_and_weights]
    aliases = {}
    if prev_out is not None:
        in_specs.append(pl.BlockSpec(memory_space=pl.ANY))
        aliases = {len(args): 0}
        args.append(prev_out)
    return pl.pallas_call(
        _combine_kernel,
        grid=(b, s // tc),
        in_specs=in_specs,
        out_specs=pl.BlockSpec((None, tc, d), lambda bi, ti: (b0 + bi, ti, 0)),
        out_shape=jax.ShapeDtypeStruct((b_total, s, d), _f32),
        input_output_aliases=aliases,
        compiler_params=pltpu.CompilerParams(
            dimension_semantics=("arbitrary", "arbitrary"), vmem_limit_bytes=VMEM_LIMIT),
        name="combine_ple",
    )(*args)


def _layer(x, p_i, g_mix, w_in, conv_w, g_ret, w_out_conv, w_out_ret, w_o, g_moe, w_rg, b_rg, w_re, b_re,
           w_exp_gate, w_exp_up, w_exp_down, g_ple_in, w_ple_gate, w_ple_proj, g_ple_post, g_out):
    b_total, s, d = x.shape
    assert b_total % sum(SLICE_PARTS) == 0
    sizes = [b_total // sum(SLICE_PARTS) * part for part in SLICE_PARTS]
    starts = [sum(sizes[:h]) for h in range(len(sizes))]
    operands = _mixer_operands(s, d, g_mix, w_in, conv_w, g_ret, w_out_conv, w_out_ret, w_o, g_moe,
                               w_rg, b_rg, w_re, b_re)
    tail = (g_ple_in.reshape(1, d), w_ple_gate.astype(_bf16), w_ple_proj.astype(_bf16),
            g_ple_post.reshape(1, d), g_out.reshape(1, d))
    win = SC_WINDOW

    def route(eid, rank, cnt, n_tok):
        counts = cnt[:, 0]
        padded = (counts + MOE_BLOCK - 1) // MOE_BLOCK * MOE_BLOCK
        pends = jnp.cumsum(padded)
        pstarts = pends - padded
        eids = jnp.arange(N_EXPERTS, dtype=jnp.int32).reshape(N_EXPERTS, 1, 1, 1)
        seg = jnp.sum(jnp.where(eid[None] == eids, pstarts.reshape(N_EXPERTS, 1, 1, 1), 0), axis=0)
        dest = rank + seg
        tiles, _, ts = dest.shape
        idx3 = dest.reshape(tiles, 2, ts // win, win).transpose(0, 2, 1, 3).reshape(n_tok // win, 2, win)
        return dest, idx3, pstarts // MOE_BLOCK, padded // MOE_BLOCK, counts

    st = [dict() for _ in sizes]
    out = None

    def mix(h):
        n_tok = sizes[h] * s
        nblk = (2 * n_tok + N_EXPERTS * (MOE_BLOCK - 1) + MOE_BLOCK - 1) // MOE_BLOCK
        nblk = (nblk + EXPERT_SPAN - 1) // EXPERT_SPAN * EXPERT_SPAN
        x1, h2p, eid, ew, rank, cnt = _mixer(x, starts[h], sizes[h], operands)
        dest, idx3, blk0, nblk_e, counts = route(eid, rank, cnt, n_tok)
        st[h].update(x1=x1, h2p=h2p, ew=ew, dest=dest, idx3=idx3, seg=(blk0, nblk_e, counts),
                     n_tok=n_tok, n_rows=nblk * MOE_BLOCK)

    def dispatch(h):
        st[h]["xs"] = _sc_dispatch(st[h]["h2p"].reshape(st[h]["n_tok"], d // 2), st[h]["idx3"], st[h]["n_rows"])

    def experts(h):
        st[h]["y"] = _experts(st[h]["xs"], w_exp_gate, w_exp_up, w_exp_down, *st[h]["seg"])

    def gather(h):
        y2 = _sc_gather(st[h]["y"], st[h]["dest"].reshape(-1))
        st[h]["y2"] = y2.reshape(st[h]["n_tok"] // MIX_TILE, 2, MIX_TILE, d // 2)

    def combine(h):
        nonlocal out
        out = _combine(st[h]["x1"], p_i, st[h]["y2"], st[h]["ew"], tail, starts[h], b_total, out)

    stages = (mix, dispatch, experts, gather, combine)
    for step in range(len(sizes) + len(stages) - 1):
        for k in reversed(range(len(stages))):
            if 0 <= step - k < len(sizes):
                stages[k](step - k)
    return out


def kernel(x, p, g_mix, w_in, conv_w, g_ret, w_out_conv, w_out_ret, w_o, g_moe, w_rg, b_rg, w_re, b_re, w_exp_gate, w_exp_up, w_exp_down, g_ple_in, w_ple_gate, w_ple_proj, g_ple_post, g_final):
    depth = p.shape[0]
    assert depth == 1, "the final norm is fused into the single layer's combine kernel"
    return _layer(x, p[0], g_mix[0], w_in[0], conv_w[0], g_ret[0], w_out_conv[0], w_out_ret[0], w_o[0],
                  g_moe[0], w_rg[0], b_rg[0], w_re[0], b_re[0], w_exp_gate[0], w_exp_up[0], w_exp_down[0],
                  g_ple_in[0], w_ple_gate[0], w_ple_proj[0], g_ple_post[0], g_final)
```
